```python
import jax, jax.numpy as jnp
from jax import lax
import numpy as np

D_MODEL = 1024
BATCH = 8
SEQ = 2048
DEPTH = 2

GRID_W = 64
CTX_LEN = 256
MLA_HEADS = 8
MLA_NOPE = 64
MLA_ROPE = 32
MLA_V = 64
MLA_Q_RANK = 256
MLA_KV_RANK = 128
GQA_HEADS = 8
GQA_KV_HEADS = 2
GQA_GROUP = GQA_HEADS // GQA_KV_HEADS
GQA_HEAD_DIM = 64
IN_SPLITS = (MLA_Q_RANK, MLA_KV_RANK, MLA_ROPE, GQA_HEADS * GQA_HEAD_DIM,
             GQA_KV_HEADS * GQA_HEAD_DIM, GQA_KV_HEADS * GQA_HEAD_DIM)
IN_WIDTH = sum(IN_SPLITS)
MIX_WIDTH = MLA_HEADS * MLA_V + GQA_HEADS * GQA_HEAD_DIM
DENSE_FF = 2816
N_EXPERTS = 8
TOP_K = 2
EXPERT_FF = 2816
Q_BLOCK = 128
ROPE_THETA = 10000.0
NORM_EPS = 1e-6
MLA_SCALE = (MLA_NOPE + MLA_ROPE) ** -0.5
GQA_SCALE = GQA_HEAD_DIM ** -0.5

kernel_name = "hybrid_mla_gqa_moe_dit_block"


def _rmsnorm(x, g):
    xf = x.astype(jnp.float32)
    y = xf * lax.rsqrt(jnp.mean(xf * xf, axis=-1, keepdims=True) + NORM_EPS)
    return (y * g.astype(jnp.float32)).astype(x.dtype)


def _modulate(h, shift, scale):
    return h * (1 + scale) + shift


def _rope_1d(x, pos):
    half = x.shape[-1] // 2
    freqs = ROPE_THETA ** (-jnp.arange(half, dtype=jnp.float32) / half)
    ang = pos[:, None] * freqs[None, :]
    cos, sin = jnp.cos(ang), jnp.sin(ang)
    x1 = x[..., :half].astype(jnp.float32)
    x2 = x[..., half:].astype(jnp.float32)
    return jnp.concatenate([x1 * cos - x2 * sin, x2 * cos + x1 * sin], axis=-1).astype(x.dtype)


def _rope_2d(x, row, col):
    d = x.shape[-1] // 2
    return jnp.concatenate([_rope_1d(x[..., :d], row), _rope_1d(x[..., d:], col)], axis=-1)


def _rope_tail(x, n_rot, pos):
    if pos is None:
        return x
    d = x.shape[-1]
    return jnp.concatenate([x[..., :d - n_rot], _rope_2d(x[..., d - n_rot:], pos[0], pos[1])], axis=-1)


def _split_in(h, w_in):
    p = h @ w_in
    offs = [int(o) for o in np.cumsum(IN_SPLITS)[:-1]]
    return jnp.split(p, offs, axis=-1)


def _mla_q(c_q, p, pos):
    B, L, _ = c_q.shape
    q = (_rmsnorm(c_q, p["q_lat_norm"]) @ p["w_uq"]).reshape(B, L, MLA_HEADS, MLA_NOPE + MLA_ROPE)
    q = _rmsnorm(q.transpose(0, 2, 1, 3), p["mla_q_gain"])
    return _rope_tail(q, MLA_ROPE, pos)[:, :, None]


def _mla_kv(c_kv, k_rope, p, pos):
    B, L, _ = c_kv.shape
    kv = (_rmsnorm(c_kv, p["kv_lat_norm"]) @ p["w_ukv"]).reshape(B, L, MLA_HEADS, MLA_NOPE + MLA_V)
    kv = kv.transpose(0, 2, 1, 3)
    k_nope, v = kv[..., :MLA_NOPE], kv[..., MLA_NOPE:]
    k_r = jnp.broadcast_to(k_rope[:, None], (B, MLA_HEADS, L, MLA_ROPE))
    k = _rmsnorm(jnp.concatenate([k_nope, k_r], axis=-1), p["mla_k_gain"])
    return _rope_tail(k, MLA_ROPE, pos), v


def _gqa_q(qb, p, pos):
    B, L, _ = qb.shape
    q = qb.reshape(B, L, GQA_KV_HEADS, GQA_GROUP, GQA_HEAD_DIM).transpose(0, 2, 3, 1, 4)
    q = _rmsnorm(q, p["gqa_q_gain"])
    return _rope_tail(q, GQA_HEAD_DIM, pos)


def _gqa_kv(kb, vb, p, pos):
    B, L, _ = kb.shape
    k = kb.reshape(B, L, GQA_KV_HEADS, GQA_HEAD_DIM).transpose(0, 2, 1, 3)
    v = vb.reshape(B, L, GQA_KV_HEADS, GQA_HEAD_DIM).transpose(0, 2, 1, 3)
    k = _rmsnorm(k, p["gqa_k_gain"])
    return _rope_tail(k, GQA_HEAD_DIM, pos), v


def _block_attention(q, k, v, scale):
    B, Hk, G, Sq, D = q.shape
    nb = Sq // Q_BLOCK
    qb = q.reshape(B, Hk, G, nb, Q_BLOCK, D).transpose(3, 0, 1, 2, 4, 5)

    def one_block(qi):
        s = jnp.einsum("bhgqd,bhkd->bhgqk", qi, k).astype(jnp.float32) * scale
        pr = jax.nn.softmax(s, axis=-1).astype(v.dtype)
        return jnp.einsum("bhgqk,bhkd->bhgqd", pr, v)

    o = lax.map(one_block, qb)
    return o.transpose(1, 2, 3, 0, 4, 5).reshape(B, Hk, G, Sq, v.shape[-1])


def _merge(oa, ob, w_out):
    B, L = oa.shape[0], oa.shape[3]
    a = oa[:, :, 0].transpose(0, 2, 1, 3).reshape(B, L, MLA_HEADS * MLA_V)
    b = ob.transpose(0, 3, 1, 2, 4).reshape(B, L, GQA_HEADS * GQA_HEAD_DIM)
    return jnp.concatenate([a, b], axis=-1) @ w_out


def _swiglu(h, wg, wu, wd):
    return (jax.nn.silu(h @ wg) * (h @ wu)) @ wd


def _moe(h, router, wg, wu, wd):
    probs = jax.nn.softmax((h @ router).astype(jnp.float32), axis=-1)
    top_v, top_i = lax.top_k(probs, TOP_K)
    top_v = top_v / jnp.sum(top_v, axis=-1, keepdims=True)
    gates = jnp.sum(jax.nn.one_hot(top_i, N_EXPERTS, dtype=jnp.float32) * top_v[..., None], axis=-2)
    gates = gates.astype(h.dtype)
    out = jnp.zeros_like(h)
    for e in range(N_EXPERTS):
        out = out + gates[..., e:e + 1] * _swiglu(h, wg[e], wu[e], wd[e])
    return out


def _ffn(h, p, moe):
    if moe:
        return _moe(h, *p["ffn"])
    return _swiglu(h, *p["ffn"])


def _layer(x, hc, c, c_ctx, p, pos, moe, last):
    mod_x = (jax.nn.silu(c) @ p["w_mod"] + p["b_mod"])[:, None, :]
    mod_c = (jax.nn.silu(c_ctx) @ p["w_mod"] + p["b_mod"])[None, None, :]
    sh1x, sc1x, g1x, sh2x, sc2x, g2x = jnp.split(mod_x, 6, axis=-1)
    sh1c, sc1c, g1c, sh2c, sc2c, g2c = jnp.split(mod_c, 6, axis=-1)

    hx = _modulate(_rmsnorm(x, p["norm_attn"]), sh1x, sc1x)
    hcn = _modulate(_rmsnorm(hc, p["norm_attn"]), sh1c, sc1c)
    cq_x, ckv_x, kr_x, qb_x, kb_x, vb_x = _split_in(hx, p["w_in"])
    cq_c, ckv_c, kr_c, qb_c, kb_c, vb_c = _split_in(hcn, p["w_in"])

    ka_c, va_c = _mla_kv(ckv_c, kr_c, p, None)
    kb_c, vb_c = _gqa_kv(kb_c, vb_c, p, None)
    ka_x, va_x = _mla_kv(ckv_x, kr_x, p, pos)
    kb_x, vb_x = _gqa_kv(kb_x, vb_x, p, pos)

    oa = _block_attention(_mla_q(cq_x, p, pos), jnp.concatenate([ka_x, ka_c], axis=2),
                          jnp.concatenate([va_x, va_c], axis=2), MLA_SCALE)
    ob = _block_attention(_gqa_q(qb_x, p, pos), jnp.concatenate([kb_x, kb_c], axis=2),
                          jnp.concatenate([vb_x, vb_c], axis=2), GQA_SCALE)
    x = x + g1x * _merge(oa, ob, p["w_out"])
    x = x + g2x * _ffn(_modulate(_rmsnorm(x, p["norm_ffn"]), sh2x, sc2x), p, moe)

    if not last:
        oa_c = _block_attention(_mla_q(cq_c, p, None), ka_c, va_c, MLA_SCALE)
        ob_c = _block_attention(_gqa_q(qb_c, p, None), kb_c, vb_c, GQA_SCALE)
        hc = hc + g1c * _merge(oa_c, ob_c, p["w_out"])
        hc = hc + g2c * _ffn(_modulate(_rmsnorm(hc, p["norm_ffn"]), sh2c, sc2c), p, moe)
    return x, hc


def _layer_params(key, prefix, moe):
    ks = jax.random.split(key, 20)
    f32 = jnp.float32

    def nrm(k, shape, s):
        return jax.random.normal(k, shape, f32) * s

    def gain(k, n):
        return 1.0 + 0.05 * jax.random.normal(k, (n,), f32)

    p = {}
    p[prefix + "w_mod"] = nrm(ks[0], (D_MODEL, 6 * D_MODEL), 0.5 * D_MODEL ** -0.5)
    p[prefix + "b_mod"] = nrm(ks[1], (6 * D_MODEL,), 0.02)
    p[prefix + "norm_attn"] = gain(ks[2], D_MODEL)
    p[prefix + "w_in"] = nrm(ks[3], (D_MODEL, IN_WIDTH), D_MODEL ** -0.5)
    p[prefix + "q_lat_norm"] = gain(ks[4], MLA_Q_RANK)
    p[prefix + "w_uq"] = nrm(ks[5], (MLA_Q_RANK, MLA_HEADS * (MLA_NOPE + MLA_ROPE)), MLA_Q_RANK ** -0.5)
    p[prefix + "kv_lat_norm"] = gain(ks[6], MLA_KV_RANK)
    p[prefix + "w_ukv"] = nrm(ks[7], (MLA_KV_RANK, MLA_HEADS * (MLA_NOPE + MLA_V)), MLA_KV_RANK ** -0.5)
    p[prefix + "mla_q_gain"] = gain(ks[8], MLA_NOPE + MLA_ROPE)
    p[prefix + "mla_k_gain"] = gain(ks[9], MLA_NOPE + MLA_ROPE)
    p[prefix + "gqa_q_gain"] = gain(ks[10], GQA_HEAD_DIM)
    p[prefix + "gqa_k_gain"] = gain(ks[11], GQA_HEAD_DIM)
    p[prefix + "w_out"] = nrm(ks[12], (MIX_WIDTH, D_MODEL), MIX_WIDTH ** -0.5)
    p[prefix + "norm_ffn"] = gain(ks[13], D_MODEL)
    if moe:
        p[prefix + "router"] = nrm(ks[14], (D_MODEL, N_EXPERTS), D_MODEL ** -0.5)
        p[prefix + "exp_w_gate"] = nrm(ks[15], (N_EXPERTS, D_MODEL, EXPERT_FF), D_MODEL ** -0.5)
        p[prefix + "exp_w_up"] = nrm(ks[16], (N_EXPERTS, D_MODEL, EXPERT_FF), D_MODEL ** -0.5)
        p[prefix + "exp_w_down"] = nrm(ks[17], (N_EXPERTS, EXPERT_FF, D_MODEL), EXPERT_FF ** -0.5)
    else:
        p[prefix + "ffn_w_gate"] = nrm(ks[14], (D_MODEL, DENSE_FF), D_MODEL ** -0.5)
        p[prefix + "ffn_w_up"] = nrm(ks[15], (D_MODEL, DENSE_FF), D_MODEL ** -0.5)
        p[prefix + "ffn_w_down"] = nrm(ks[16], (DENSE_FF, D_MODEL), DENSE_FF ** -0.5)
    return p


def setup_inputs(seed: int = 0) -> dict:
    key = jax.random.key(seed)
    kx, kc, kctx, kcc, k0, k1 = jax.random.split(key, 6)
    inputs = {
        "x": jax.random.normal(kx, (BATCH, SEQ, D_MODEL), jnp.float32),
        "c": jax.random.normal(kc, (BATCH, D_MODEL), jnp.float32),
        "ctx": jax.random.normal(kctx, (BATCH, CTX_LEN, D_MODEL), jnp.float32),
        "c_ctx": jax.random.normal(kcc, (D_MODEL,), jnp.float32),
    }
    inputs.update(_layer_params(k0, "l0_", moe=False))
    inputs.update(_layer_params(k1, "l1_", moe=True))
    return inputs


def reference(x, c, ctx, c_ctx,
              l0_w_mod, l0_b_mod, l0_norm_attn, l0_w_in, l0_q_lat_norm, l0_w_uq, l0_kv_lat_norm,
              l0_w_ukv, l0_mla_q_gain, l0_mla_k_gain, l0_gqa_q_gain, l0_gqa_k_gain, l0_w_out,
              l0_norm_ffn, l0_ffn_w_gate, l0_ffn_w_up, l0_ffn_w_down,
              l1_w_mod, l1_b_mod, l1_norm_attn, l1_w_in, l1_q_lat_norm, l1_w_uq, l1_kv_lat_norm,
              l1_w_ukv, l1_mla_q_gain, l1_mla_k_gain, l1_gqa_q_gain, l1_gqa_k_gain, l1_w_out,
              l1_norm_ffn, l1_router, l1_exp_w_gate, l1_exp_w_up, l1_exp_w_down):
    p0 = dict(w_mod=l0_w_mod, b_mod=l0_b_mod, norm_attn=l0_norm_attn, w_in=l0_w_in,
              q_lat_norm=l0_q_lat_norm, w_uq=l0_w_uq, kv_lat_norm=l0_kv_lat_norm, w_ukv=l0_w_ukv,
              mla_q_gain=l0_mla_q_gain, mla_k_gain=l0_mla_k_gain, gqa_q_gain=l0_gqa_q_gain,
              gqa_k_gain=l0_gqa_k_gain, w_out=l0_w_out, norm_ffn=l0_norm_ffn,
              ffn=(l0_ffn_w_gate, l0_ffn_w_up, l0_ffn_w_down))
    p1 = dict(w_mod=l1_w_mod, b_mod=l1_b_mod, norm_attn=l1_norm_attn, w_in=l1_w_in,
              q_lat_norm=l1_q_lat_norm, w_uq=l1_w_uq, kv_lat_norm=l1_kv_lat_norm, w_ukv=l1_w_ukv,
              mla_q_gain=l1_mla_q_gain, mla_k_gain=l1_mla_k_gain, gqa_q_gain=l1_gqa_q_gain,
              gqa_k_gain=l1_gqa_k_gain, w_out=l1_w_out, norm_ffn=l1_norm_ffn,
              ffn=(l1_router, l1_exp_w_gate, l1_exp_w_up, l1_exp_w_down))
    layers = [p0, p1]

    n_tok = x.shape[1]
    rows = n_tok // GRID_W
    row = jnp.repeat(jnp.arange(rows, dtype=jnp.float32), GRID_W)
    col = jnp.tile(jnp.arange(GRID_W, dtype=jnp.float32), rows)
    pos = (row, col)

    hc = ctx
    for i in range(DEPTH):
        x, hc = _layer(x, hc, c, c_ctx, layers[i], pos, moe=(i % 2 == 1), last=(i == DEPTH - 1))
    return x
```

```python
import functools

import jax
import jax.numpy as jnp
from jax import lax
from jax.experimental import pallas as pl
from jax.experimental.pallas import tpu as pltpu

D_MODEL = 1024
BATCH = 8
SEQ = 2048
CTX_LEN = 256
GRID_W = 64
MLA_HEADS = 8
MLA_NOPE = 64
MLA_ROPE = 32
MLA_V = 64
MLA_QK = MLA_NOPE + MLA_ROPE
MLA_Q_RANK = 256
MLA_KV_RANK = 128
GQA_HEADS = 8
GQA_KV_HEADS = 2
GQA_HEAD_DIM = 64
DENSE_FF = 2816
N_EXPERTS = 8
EXPERT_FF = 2816
ROPE_THETA = 10000.0
NORM_EPS = 1e-6
MLA_SCALE = MLA_QK ** -0.5
GQA_SCALE = GQA_HEAD_DIM ** -0.5

LANES = 128
N_LAT = BATCH * SEQ
N_CTX = BATCH * CTX_LEN
N_TOK = N_LAT + N_CTX
TM = 512
LAT_TILES = N_LAT // TM
TOK_TILES = N_TOK // TM
SEQ_TILES = SEQ // TM
TQ = 512
IN_PAD = 1280
FF_CHUNK = 1408
VMEM_LIMIT = 56 * 1024 * 1024

F32 = jnp.float32
BF16 = jnp.bfloat16


def _silu(x):
    return x / (1.0 + jnp.exp(-x))


def _params(*sem):
    return pltpu.CompilerParams(dimension_semantics=sem, vmem_limit_bytes=VMEM_LIMIT)


def _mod_kernel(c_ref, w_ref, b_ref, o_ref):
    s = _silu(c_ref[...])
    o_ref[...] = jnp.dot(s, w_ref[...], precision=lax.Precision.HIGHEST,
                         preferred_element_type=F32) + b_ref[...]


def _modulation(cc, w_mod, b_mod):
    n = w_mod.shape[1]
    bn = 1024
    out = pl.pallas_call(
        _mod_kernel,
        grid=(n // bn,),
        in_specs=[pl.BlockSpec((16, D_MODEL), lambda j: (0, 0)),
                  pl.BlockSpec((D_MODEL, bn), lambda j: (0, j)),
                  pl.BlockSpec((1, bn), lambda j: (0, j))],
        out_specs=pl.BlockSpec((16, bn), lambda j: (0, j)),
        out_shape=jax.ShapeDtypeStruct((16, n), F32),
        compiler_params=_params("parallel"),
        name="modulation",
    )(cc, w_mod, b_mod.reshape(1, n))
    return out.reshape(16, 6, D_MODEL)


def _rope(y, tab_ref, base, shift):
    return (y * tab_ref[base] + pltpu.roll(y, LANES - shift, 1) * tab_ref[base + 1]
            + pltpu.roll(y, shift, 1) * tab_ref[base + 2])


def _proj_kernel(x_ref, mod_ref, gn_ref, win_ref, glat_ref, wuq_ref, wkv_ref, g128_ref, tab_ref,
                 qa_ref, ka_ref, va_ref, qb_ref, kb_ref, vb_ref):
    x = x_ref[...]
    r = lax.rsqrt(jnp.mean(x * x, axis=-1, keepdims=True) + NORM_EPS)
    h = (x * r * gn_ref[...]) * (1.0 + mod_ref[0, 1:2, :]) + mod_ref[0, 0:1, :]
    p = jnp.dot(h.astype(BF16), win_ref[...], preferred_element_type=F32)

    lane = lax.broadcasted_iota(jnp.int32, (TM, LANES), 1)
    lo = lane < 64

    cq = p[:, 0:256]
    rq = lax.rsqrt(jnp.mean(cq * cq, axis=-1, keepdims=True) + NORM_EPS)
    qa = jnp.dot((cq * rq * glat_ref[0:1, :]).astype(BF16), wuq_ref[...],
                 preferred_element_type=F32)
    ckv = p[:, 256:384]
    rkv = lax.rsqrt(jnp.mean(ckv * ckv, axis=-1, keepdims=True) + NORM_EPS)
    slab = jnp.concatenate([ckv * rkv * glat_ref[1:2, 0:128], p[:, 384:512]], axis=-1)
    kv = jnp.dot(slab.astype(BF16), wkv_ref[...], preferred_element_type=F32)

    gqa_a = g128_ref[0:1, :]
    gka_a = g128_ref[1:2, :]
    for hd in range(MLA_HEADS):
        blk = qa[:, hd * LANES:(hd + 1) * LANES]
        rr = lax.rsqrt(jnp.sum(blk * blk, axis=-1, keepdims=True) * (1.0 / MLA_QK) + NORM_EPS)
        qa_ref[hd] = _rope(blk * rr * gqa_a, tab_ref, 0, 8).astype(BF16)
        blk = kv[:, hd * LANES:(hd + 1) * LANES]
        rr = lax.rsqrt(jnp.sum(blk * blk, axis=-1, keepdims=True) * (1.0 / MLA_QK) + NORM_EPS)
        ka_ref[hd] = _rope(blk * rr * gka_a, tab_ref, 0, 8).astype(BF16)
    for pr in range(MLA_HEADS // 2):
        blk = kv[:, 1024 + pr * LANES:1024 + (pr + 1) * LANES]
        va_ref[2 * pr] = jnp.where(lo, blk, 0.0).astype(BF16)
        va_ref[2 * pr + 1] = jnp.where(lo, 0.0, blk).astype(BF16)

    def pair_norm(blk, gain):
        sq = blk * blk
        s_lo = jnp.sum(jnp.where(lo, sq, 0.0), axis=-1, keepdims=True)
        s_hi = jnp.sum(jnp.where(lo, 0.0, sq), axis=-1, keepdims=True)
        rr = jnp.where(lo, lax.rsqrt(s_lo * (1.0 / GQA_HEAD_DIM) + NORM_EPS),
                       lax.rsqrt(s_hi * (1.0 / GQA_HEAD_DIM) + NORM_EPS))
        return _rope(blk * rr * gain, tab_ref, 3, 16)

    gq_b = g128_ref[2:3, :]
    gk_b = g128_ref[3:4, :]
    for pr in range(GQA_HEADS // 2):
        blk = p[:, 512 + pr * LANES:512 + (pr + 1) * LANES]
        qb_ref[pr] = pair_norm(blk, gq_b).astype(BF16)
    kb = pair_norm(p[:, 1024:1152], gk_b)
    kb_sw = pltpu.roll(kb, 64, 1)
    kb_ref[0] = jnp.where(lo, kb, 0.0).astype(BF16)
    kb_ref[1] = jnp.where(lo, 0.0, kb_sw).astype(BF16)
    kb_ref[2] = jnp.where(lo, kb_sw, 0.0).astype(BF16)
    kb_ref[3] = jnp.where(lo, 0.0, kb).astype(BF16)
    vb = p[:, 1152:1280]
    vb_sw = pltpu.roll(vb, 64, 1)
    vb_ref[0] = jnp.where(lo, vb, 0.0).astype(BF16)
    vb_ref[1] = jnp.where(lo, 0.0, vb_sw).astype(BF16)
    vb_ref[2] = jnp.where(lo, vb_sw, 0.0).astype(BF16)
    vb_ref[3] = jnp.where(lo, 0.0, vb).astype(BF16)


def _mod_index(i):
    return jnp.where(i < LAT_TILES, i // SEQ_TILES, BATCH)


def _projection(xall, mod3, gn, w_in, glat, w_uq, w_kv, g128, tab):
    const2 = lambda i: (0, 0)
    head_out = lambda n: pl.BlockSpec((n, TM, LANES), lambda i: (0, i, 0))
    head_shape = lambda n: jax.ShapeDtypeStruct((n, N_TOK, LANES), BF16)
    return pl.pallas_call(
        _proj_kernel,
        grid=(TOK_TILES,),
        in_specs=[pl.BlockSpec((TM, D_MODEL), lambda i: (i, 0)),
                  pl.BlockSpec((1, 6, D_MODEL), lambda i: (_mod_index(i), 0, 0)),
                  pl.BlockSpec((1, D_MODEL), const2),
                  pl.BlockSpec((D_MODEL, IN_PAD), const2),
                  pl.BlockSpec((2, 256), const2),
                  pl.BlockSpec((MLA_Q_RANK, MLA_HEADS * LANES), const2),
                  pl.BlockSpec((256, MLA_HEADS * LANES + MLA_HEADS * MLA_V), const2),
                  pl.BlockSpec((8, LANES), const2),
                  pl.BlockSpec((6, TM, LANES),
                               lambda i: (0, jnp.where(i < LAT_TILES, i % SEQ_TILES, SEQ_TILES), 0))],
        out_specs=[head_out(8), head_out(8), head_out(8), head_out(4), head_out(4), head_out(4)],
        out_shape=[head_shape(8), head_shape(8), head_shape(8), head_shape(4), head_shape(4),
                   head_shape(4)],
        compiler_params=_params("parallel"),
        name="projection",
    )(xall, mod3, gn, w_in, glat, w_uq, w_kv, g128, tab)


def _attn_kernel(*refs, with_lat):
    if with_lat:
        q0_ref, q1_ref, kl_ref, kc_ref, vl_ref, vc_ref, o_ref = refs
    else:
        q0_ref, q1_ref, kc_ref, vc_ref, _, o_ref = refs
    nt = (((1,), (1,)), ((), ()))
    acc = None
    for u, q_ref in enumerate((q0_ref, q1_ref)):
        q = q_ref[...]
        s_c = lax.dot_general(q, kc_ref[u], nt, preferred_element_type=F32)
        m = jnp.max(s_c, axis=-1, keepdims=True)
        if with_lat:
            s_l = lax.dot_general(q, kl_ref[u], nt, preferred_element_type=F32)
            m = jnp.maximum(m, jnp.max(s_l, axis=-1, keepdims=True))
            p_l = jnp.exp(s_l - m)
        p_c = jnp.exp(s_c - m)
        l = jnp.sum(p_c, axis=-1, keepdims=True)
        o = jnp.dot(p_c.astype(BF16), vc_ref[u], preferred_element_type=F32)
        if with_lat:
            l = l + jnp.sum(p_l, axis=-1, keepdims=True)
            o = o + jnp.dot(p_l.astype(BF16), vl_ref[u], preferred_element_type=F32)
        o = o / l
        acc = o if acc is None else acc + o
    o_ref[...] = acc.astype(o_ref.dtype)


def _attention(q, k, v, q_heads, kv_pair, out_rows):
    n_units = 4
    qt = SEQ // TQ
    ctx_blk0 = N_LAT // CTX_LEN

    def qspec(which):
        return pl.BlockSpec((None, TQ, LANES), lambda b, p, i: (q_heads(p)[which], b * qt + i, 0))

    lat = pl.BlockSpec((2, SEQ, LANES), lambda b, p, i: (kv_pair(p), b, 0))
    ctx = pl.BlockSpec((2, CTX_LEN, LANES), lambda b, p, i: (kv_pair(p), ctx_blk0 + b, 0))
    return pl.pallas_call(
        functools.partial(_attn_kernel, with_lat=True),
        grid=(BATCH, n_units, qt),
        in_specs=[qspec(0), qspec(1), lat, ctx, lat, ctx],
        out_specs=pl.BlockSpec((TQ, LANES), lambda b, p, i: (b * qt + i, p)),
        out_shape=jax.ShapeDtypeStruct((out_rows, n_units * LANES), BF16),
        compiler_params=_params("parallel", "parallel", "parallel"),
        name="attention_latent",
    )(q, q, k, k, v, v)


def _attention_ctx(q, k, v, o_prev, q_heads, kv_pair):
    n_units = 4
    ctx_blk0 = N_LAT // CTX_LEN

    def qspec(which):
        return pl.BlockSpec((None, CTX_LEN, LANES), lambda b, p: (q_heads(p)[which], ctx_blk0 + b, 0))

    ctx = pl.BlockSpec((2, CTX_LEN, LANES), lambda b, p: (kv_pair(p), ctx_blk0 + b, 0))
    return pl.pallas_call(
        functools.partial(_attn_kernel, with_lat=False),
        grid=(BATCH, n_units),
        in_specs=[qspec(0), qspec(1), ctx, ctx, pl.BlockSpec(memory_space=pl.ANY)],
        out_specs=pl.BlockSpec((CTX_LEN, LANES), lambda b, p: (ctx_blk0 + b, p)),
        out_shape=jax.ShapeDtypeStruct((N_TOK, n_units * LANES), BF16),
        input_output_aliases={4: 0},
        compiler_params=_params("parallel", "parallel"),
        name="attention_context",
    )(q, q, k, v, o_prev)


_A_HEADS = lambda p: (2 * p, 2 * p + 1)
_A_KV = lambda p: p
_B_HEADS = lambda p: (p, p)
_B_KV = lambda p: p // 2


def _post_body(x_ref, oa_ref, ob_ref, mod_ref, gn_ref, wo_ref):
    y = (jnp.dot(oa_ref[...], wo_ref[0:512, :], preferred_element_type=F32)
         + jnp.dot(ob_ref[...], wo_ref[512:1024, :], preferred_element_type=F32))
    x1 = x_ref[...] + mod_ref[0, 2:3, :] * y
    r = lax.rsqrt(jnp.mean(x1 * x1, axis=-1, keepdims=True) + NORM_EPS)
    h2 = (x1 * r * gn_ref[...]) * (1.0 + mod_ref[0, 4:5, :]) + mod_ref[0, 3:4, :]
    return x1, h2


def _post_kernel(x_ref, oa_ref, ob_ref, mod_ref, gn_ref, wo_ref, x1_ref, h2_ref):
    x1, h2 = _post_body(x_ref, oa_ref, ob_ref, mod_ref, gn_ref, wo_ref)
    x1_ref[...] = x1
    h2_ref[...] = h2.astype(BF16)


def _post_router_kernel(x_ref, oa_ref, ob_ref, mod_ref, gn_ref, wo_ref, wr_ref,
                        x1_ref, h2_ref, gate_ref):
    x1, h2 = _post_body(x_ref, oa_ref, ob_ref, mod_ref, gn_ref, wo_ref)
    x1_ref[...] = x1
    h2_ref[...] = h2.astype(BF16)
    logits = jnp.dot(h2, wr_ref[...], precision=lax.Precision.HIGHEST, preferred_element_type=F32)
    lane = lax.broadcasted_iota(jnp.int32, logits.shape, 1)
    valid = lane < N_EXPERTS
    lg = jnp.where(valid, logits, -jnp.inf)
    e = jnp.exp(lg - jnp.max(lg, axis=-1, keepdims=True))
    probs = e / jnp.sum(e, axis=-1, keepdims=True)
    v1 = jnp.max(probs, axis=-1, keepdims=True)
    i1 = jnp.min(jnp.where(probs == v1, lane, LANES), axis=-1, keepdims=True)
    rest = jnp.where(valid & (lane != i1), probs, -1.0)
    v2 = jnp.max(rest, axis=-1, keepdims=True)
    i2 = jnp.min(jnp.where(rest == v2, lane, LANES), axis=-1, keepdims=True)
    sel = jnp.where(lane == i1, v1, 0.0) + jnp.where(lane == i2, v2, 0.0)
    gate_ref[...] = sel / (v1 + v2)


def _post_attention(xall, oa, ob, mod3, gn, w_out, w_router, n_tiles):
    const2 = lambda i: (0, 0)
    rows = n_tiles * TM
    in_specs = [pl.BlockSpec((TM, D_MODEL), lambda i: (i, 0)),
                pl.BlockSpec((TM, 512), lambda i: (i, 0)),
                pl.BlockSpec((TM, 512), lambda i: (i, 0)),
                pl.BlockSpec((1, 6, D_MODEL), lambda i: (_mod_index(i), 0, 0)),
                pl.BlockSpec((1, D_MODEL), const2),
                pl.BlockSpec((D_MODEL, D_MODEL), const2)]
    out_specs = [pl.BlockSpec((TM, D_MODEL), lambda i: (i, 0)),
                 pl.BlockSpec((TM, D_MODEL), lambda i: (i, 0))]
    out_shape = [jax.ShapeDtypeStruct((rows, D_MODEL), F32),
                 jax.ShapeDtypeStruct((rows, D_MODEL), BF16)]
    args = [xall, oa, ob, mod3, gn, w_out]
    body = _post_kernel
    if w_router is not None:
        in_specs.append(pl.BlockSpec((D_MODEL, LANES), const2))
        out_specs.append(pl.BlockSpec((TM, LANES), lambda i: (i, 0)))
        out_shape.append(jax.ShapeDtypeStruct((rows, LANES), F32))
        args.append(w_router)
        body = _post_router_kernel
    return pl.pallas_call(
        body, grid=(n_tiles,), in_specs=in_specs, out_specs=out_specs, out_shape=out_shape,
        compiler_params=_params("parallel"), name="post_attention",
    )(*args)


def _swiglu_chunk(h, wg, wu, wd):
    g = jnp.dot(h, wg, preferred_element_type=F32)
    u = jnp.dot(h, wu, preferred_element_type=F32)
    return jnp.dot((_silu(g) * u).astype(BF16), wd, preferred_element_type=F32)


def _ffn_kernel(h_ref, x1_ref, mod_ref, wg_ref, wu_ref, wd_ref, o_ref):
    h = h_ref[...]
    y = None
    for c in range(DENSE_FF // FF_CHUNK):
        sl = slice(c * FF_CHUNK, (c + 1) * FF_CHUNK)
        yc = _swiglu_chunk(h, wg_ref[:, sl], wu_ref[:, sl], wd_ref[sl, :])
        y = yc if y is None else y + yc
    o_ref[...] = x1_ref[...] + mod_ref[0, 5:6, :] * y


def _dense_ffn(h2, x1, mod3, wg, wu, wd):
    resident = lambda shape: pl.BlockSpec(shape, lambda i: (0, 0), pipeline_mode=pl.Buffered(1))
    return pl.pallas_call(
        _ffn_kernel,
        grid=(TOK_TILES,),
        in_specs=[pl.BlockSpec((TM, D_MODEL), lambda i: (i, 0)),
                  pl.BlockSpec((TM, D_MODEL), lambda i: (i, 0)),
                  pl.BlockSpec((1, 6, D_MODEL), lambda i: (_mod_index(i), 0, 0)),
                  resident((D_MODEL, DENSE_FF)), resident((D_MODEL, DENSE_FF)),
                  resident((DENSE_FF, D_MODEL))],
        out_specs=pl.BlockSpec((TM, D_MODEL), lambda i: (i, 0)),
        out_shape=jax.ShapeDtypeStruct((N_TOK, D_MODEL), F32),
        compiler_params=_params("parallel"),
        name="dense_ffn",
    )(h2, x1, mod3, wg, wu, wd)


def _moe_kernel(h_ref, x1_ref, gate_ref, mod_ref, wg_ref, wu_ref, wd_ref, o_ref, acc_ref):
    e = pl.program_id(1)
    f = pl.program_id(2)

    @pl.when((e == 0) & (f == 0))
    def _():
        acc_ref[...] = jnp.zeros_like(acc_ref)

    lane = lax.broadcasted_iota(jnp.int32, (TM, LANES), 1)
    gcol = jnp.sum(jnp.where(lane == e, gate_ref[...], 0.0), axis=-1, keepdims=True)
    acc_ref[...] += gcol * _swiglu_chunk(h_ref[...], wg_ref[0], wu_ref[0], wd_ref[0])

    @pl.when((e == N_EXPERTS - 1) & (f == EXPERT_FF // FF_CHUNK - 1))
    def _():
        o_ref[...] = x1_ref[...] + mod_ref[0, 5:6, :] * acc_ref[...]


def _moe_ffn(h2, x1, gates, mod3, wg, wu, wd):
    nf = EXPERT_FF // FF_CHUNK
    return pl.pallas_call(
        _moe_kernel,
        grid=(LAT_TILES, N_EXPERTS, nf),
        in_specs=[pl.BlockSpec((TM, D_MODEL), lambda i, e, f: (i, 0)),
                  pl.BlockSpec((TM, D_MODEL), lambda i, e, f: (i, 0)),
                  pl.BlockSpec((TM, LANES), lambda i, e, f: (i, 0)),
                  pl.BlockSpec((1, 6, D_MODEL), lambda i, e, f: (i // SEQ_TILES, 0, 0)),
                  pl.BlockSpec((1, D_MODEL, FF_CHUNK), lambda i, e, f: (e, 0, f)),
                  pl.BlockSpec((1, D_MODEL, FF_CHUNK), lambda i, e, f: (e, 0, f)),
                  pl.BlockSpec((1, FF_CHUNK, D_MODEL), lambda i, e, f: (e, f, 0))],
        out_specs=pl.BlockSpec((TM, D_MODEL), lambda i, e, f: (i, 0)),
        out_shape=jax.ShapeDtypeStruct((N_LAT, D_MODEL), F32),
        scratch_shapes=[pltpu.VMEM((TM, D_MODEL), F32)],
        compiler_params=_params("parallel", "arbitrary", "arbitrary"),
        name="moe_ffn",
    )(h2, x1, gates, mod3, wg, wu, wd)


def _rope_tables():
    t = jnp.arange(SEQ)
    row = (t // GRID_W).astype(F32)
    col = (t % GRID_W).astype(F32)

    def one_axis(pos, half):
        freqs = ROPE_THETA ** (-jnp.arange(half, dtype=F32) / half)
        ang = pos[:, None] * freqs[None, :]
        cos, sin = jnp.cos(ang), jnp.sin(ang)
        zero = jnp.zeros_like(sin)
        return (jnp.concatenate([cos, cos], -1), jnp.concatenate([-sin, zero], -1),
                jnp.concatenate([zero, sin], -1))

    def two_axes(half):
        r, c = one_axis(row, half), one_axis(col, half)
        return [jnp.concatenate([a, b], -1) for a, b in zip(r, c)]

    def pad_a(tbl, fill):
        return jnp.concatenate([jnp.full((SEQ, MLA_NOPE), fill, F32), tbl,
                                jnp.full((SEQ, LANES - MLA_QK), fill, F32)], -1)

    ta = two_axes(MLA_ROPE // 4)
    tb = [jnp.concatenate([x, x], -1) for x in two_axes(GQA_HEAD_DIM // 4)]
    tabs = [pad_a(ta[0], 1.0), pad_a(ta[1], 0.0), pad_a(ta[2], 0.0)] + tb
    ident = [jnp.ones((TM, LANES), F32), jnp.zeros((TM, LANES), F32), jnp.zeros((TM, LANES), F32)]
    return jnp.stack([jnp.concatenate([tbl, idn], 0) for tbl, idn in zip(tabs, ident + ident)])


def _layer_weights(w_in, q_lat_norm, w_uq, kv_lat_norm, w_ukv, mla_q_gain, mla_k_gain,
                   gqa_q_gain, gqa_k_gain):
    split = MLA_Q_RANK + MLA_KV_RANK + MLA_ROPE
    w_in_p = jnp.concatenate([w_in[:, :split], jnp.zeros((D_MODEL, 512 - split), F32),
                              w_in[:, split:]], axis=1).astype(BF16)
    w_uq_p = jnp.pad(w_uq.reshape(MLA_Q_RANK, MLA_HEADS, MLA_QK),
                     ((0, 0), (0, 0), (0, LANES - MLA_QK))).reshape(MLA_Q_RANK, MLA_HEADS * LANES)
    ukv = w_ukv.reshape(MLA_KV_RANK, MLA_HEADS, MLA_NOPE + MLA_V)
    w_k = jnp.pad(ukv[:, :, :MLA_NOPE], ((0, 0), (0, 0), (0, LANES - MLA_NOPE)))
    place = jnp.pad(jnp.eye(MLA_ROPE, dtype=F32), ((0, 0), (MLA_NOPE, LANES - MLA_QK)))
    place = jnp.broadcast_to(place[:, None, :], (MLA_ROPE, MLA_HEADS, LANES))
    w_k = jnp.concatenate([w_k, place, jnp.zeros((256 - MLA_KV_RANK - MLA_ROPE, MLA_HEADS, LANES), F32)], 0)
    w_v = jnp.pad(ukv[:, :, MLA_NOPE:].reshape(MLA_KV_RANK, MLA_HEADS * MLA_V),
                  ((0, 256 - MLA_KV_RANK), (0, 0)))
    w_kv = jnp.concatenate([w_k.reshape(256, MLA_HEADS * LANES), w_v], axis=1)
    glat = jnp.stack([q_lat_norm, jnp.pad(kv_lat_norm, (0, 256 - MLA_KV_RANK))])
    pad_qk = lambda g: jnp.pad(g, (0, LANES - MLA_QK))
    g128 = jnp.stack([pad_qk(mla_q_gain) * MLA_SCALE, pad_qk(mla_k_gain),
                      jnp.tile(gqa_q_gain, 2) * GQA_SCALE, jnp.tile(gqa_k_gain, 2)])
    g128 = jnp.pad(g128, ((0, 4), (0, 0)))
    return w_in_p, glat, w_uq_p.astype(BF16), w_kv.astype(BF16), g128


def _mixer(xall, mod3, tab, norm_attn, w_in, q_lat_norm, w_uq, kv_lat_norm, w_ukv, mla_q_gain,
           mla_k_gain, gqa_q_gain, gqa_k_gain, with_ctx_queries):
    w_in_p, glat, w_uq_p, w_kv, g128 = _layer_weights(
        w_in, q_lat_norm, w_uq, kv_lat_norm, w_ukv, mla_q_gain, mla_k_gain, gqa_q_gain, gqa_k_gain)
    qa, ka, va, qb, kb, vb = _projection(xall, mod3, norm_attn.reshape(1, D_MODEL), w_in_p, glat,
                                         w_uq_p, w_kv, g128, tab)
    out_rows = N_TOK if with_ctx_queries else N_LAT
    oa = _attention(qa, ka, va, _A_HEADS, _A_KV, out_rows)
    ob = _attention(qb, kb, vb, _B_HEADS, _B_KV, out_rows)
    if with_ctx_queries:
        oa = _attention_ctx(qa, ka, va, oa, _A_HEADS, _A_KV)
        ob = _attention_ctx(qb, kb, vb, ob, _B_HEADS, _B_KV)
    return oa, ob


def kernel(x, c, ctx, c_ctx, l0_w_mod, l0_b_mod, l0_norm_attn, l0_w_in, l0_q_lat_norm, l0_w_uq, l0_kv_lat_norm, l0_w_ukv, l0_mla_q_gain, l0_mla_k_gain, l0_gqa_q_gain, l0_gqa_k_gain, l0_w_out, l0_norm_ffn, l0_ffn_w_gate, l0_ffn_w_up, l0_ffn_w_down, l1_w_mod, l1_b_mod, l1_norm_attn, l1_w_in, l1_q_lat_norm, l1_w_uq, l1_kv_lat_norm, l1_w_ukv, l1_mla_q_gain, l1_mla_k_gain, l1_gqa_q_gain, l1_gqa_k_gain, l1_w_out, l1_norm_ffn, l1_router, l1_exp_w_gate, l1_exp_w_up, l1_exp_w_down):
    xall = jnp.concatenate([x.reshape(N_LAT, D_MODEL), ctx.reshape(N_CTX, D_MODEL)], axis=0)
    cc = jnp.concatenate([c, c_ctx[None, :], jnp.zeros((16 - BATCH - 1, D_MODEL), F32)], axis=0)
    tab = _rope_tables()

    mod3 = _modulation(cc, l0_w_mod, l0_b_mod)
    oa, ob = _mixer(xall, mod3, tab, l0_norm_attn, l0_w_in, l0_q_lat_norm, l0_w_uq,
                    l0_kv_lat_norm, l0_w_ukv, l0_mla_q_gain, l0_mla_k_gain, l0_gqa_q_gain,
                    l0_gqa_k_gain, with_ctx_queries=True)
    x1, h2 = _post_attention(xall, oa, ob, mod3, l0_norm_ffn.reshape(1, D_MODEL),
                             l0_w_out.astype(BF16), None, TOK_TILES)
    xall = _dense_ffn(h2, x1, mod3, l0_ffn_w_gate.astype(BF16), l0_ffn_w_up.astype(BF16),
                      l0_ffn_w_down.astype(BF16))

    mod3 = _modulation(cc, l1_w_mod, l1_b_mod)
    oa, ob = _mixer(xall, mod3, tab, l1_norm_attn, l1_w_in, l1_q_lat_norm, l1_w_uq,
                    l1_kv_lat_norm, l1_w_ukv, l1_mla_q_gain, l1_mla_k_gain, l1_gqa_q_gain,
                    l1_gqa_k_gain, with_ctx_queries=False)
    w_router = jnp.pad(l1_router, ((0, 0), (0, LANES - N_EXPERTS)))
    x1, h2, gates = _post_attention(xall, oa, ob, mod3, l1_norm_ffn.reshape(1, D_MODEL),
                                    l1_w_out.astype(BF16), w_router, LAT_TILES)
    out = _moe_ffn(h2, x1, gates, mod3, l1_exp_w_gate.astype(BF16), l1_exp_w_up.astype(BF16),
                   l1_exp_w_down.astype(BF16))
    return out.reshape(BATCH, SEQ, D_MODEL)
```

```python
import functools

import jax
import jax.numpy as jnp
from jax import lax
from jax.experimental import pallas as pl
from jax.experimental.pallas import tpu as pltpu

D_MODEL = 1024
BATCH = 8
SEQ = 2048
CTX_LEN = 256
GRID_W = 64
MLA_HEADS = 8
MLA_NOPE = 64
MLA_ROPE = 32
MLA_V = 64
MLA_QK = MLA_NOPE + MLA_ROPE
MLA_Q_RANK = 256
MLA_KV_RANK = 128
GQA_HEADS = 8
GQA_KV_HEADS = 2
GQA_HEAD_DIM = 64
DENSE_FF = 2816
N_EXPERTS = 8
EXPERT_FF = 2816
ROPE_THETA = 10000.0
NORM_EPS = 1e-6
MLA_SCALE = MLA_QK ** -0.5
GQA_SCALE = GQA_HEAD_DIM ** -0.5

LANES = 128
N_LAT = BATCH * SEQ
N_CTX = BATCH * CTX_LEN
N_TOK = N_LAT + N_CTX
TM = 512
LAT_TILES = N_LAT // TM
TOK_TILES = N_TOK // TM
SEQ_TILES = SEQ // TM
TQ = 512
IN_PAD = 1280
FF_CHUNK = 1408
ROW_SUB = D_MODEL // LANES
TMX = 512
SORT_TILES = 2 * N_LAT // TMX + N_EXPERTS
SORT_ROWS = SORT_TILES * TMX
VMEM_LIMIT = 56 * 1024 * 1024

F32 = jnp.float32
BF16 = jnp.bfloat16


def _silu(x):
    return x / (1.0 + jnp.exp(-x))


def _params(*sem):
    return pltpu.CompilerParams(dimension_semantics=sem, vmem_limit_bytes=VMEM_LIMIT)


def _mod_kernel(c_ref, w_ref, b_ref, o_ref):
    s = _silu(c_ref[...])
    o_ref[...] = jnp.dot(s, w_ref[...], precision=lax.Precision.HIGHEST,
                         preferred_element_type=F32) + b_ref[...]


def _modulation(cc, w_mod, b_mod):
    n = w_mod.shape[1]
    bn = 1024
    out = pl.pallas_call(
        _mod_kernel,
        grid=(n // bn,),
        in_specs=[pl.BlockSpec((16, D_MODEL), lambda j: (0, 0)),
                  pl.BlockSpec((D_MODEL, bn), lambda j: (0, j)),
                  pl.BlockSpec((1, bn), lambda j: (0, j))],
        out_specs=pl.BlockSpec((16, bn), lambda j: (0, j)),
        out_shape=jax.ShapeDtypeStruct((16, n), F32),
        compiler_params=_params("parallel"),
        name="modulation",
    )(cc, w_mod, b_mod.reshape(1, n))
    return out.reshape(16, 6, D_MODEL)


def _rope(y, tab_ref, base, shift):
    return (y * tab_ref[base] + pltpu.roll(y, LANES - shift, 1) * tab_ref[base + 1]
            + pltpu.roll(y, shift, 1) * tab_ref[base + 2])


def _proj_kernel(x_ref, mod_ref, gn_ref, win_ref, glat_ref, wuq_ref, wkv_ref, g128_ref, tab_ref,
                 qa_ref, ka_ref, va_ref, qb_ref, kb_ref, vb_ref):
    x = x_ref[...]
    r = lax.rsqrt(jnp.mean(x * x, axis=-1, keepdims=True) + NORM_EPS)
    h = (x * r * gn_ref[...]) * (1.0 + mod_ref[0, 1:2, :]) + mod_ref[0, 0:1, :]
    p = jnp.dot(h.astype(BF16), win_ref[...], preferred_element_type=F32)

    lane = lax.broadcasted_iota(jnp.int32, (TM, LANES), 1)
    lo = lane < 64

    cq = p[:, 0:256]
    rq = lax.rsqrt(jnp.mean(cq * cq, axis=-1, keepdims=True) + NORM_EPS)
    qa = jnp.dot((cq * rq * glat_ref[0:1, :]).astype(BF16), wuq_ref[...],
                 preferred_element_type=F32)
    ckv = p[:, 256:384]
    rkv = lax.rsqrt(jnp.mean(ckv * ckv, axis=-1, keepdims=True) + NORM_EPS)
    slab = jnp.concatenate([ckv * rkv * glat_ref[1:2, 0:128], p[:, 384:512]], axis=-1)
    kv = jnp.dot(slab.astype(BF16), wkv_ref[...], preferred_element_type=F32)

    gqa_a = g128_ref[0:1, :]
    gka_a = g128_ref[1:2, :]
    for hd in range(MLA_HEADS):
        blk = qa[:, hd * LANES:(hd + 1) * LANES]
        rr = lax.rsqrt(jnp.sum(blk * blk, axis=-1, keepdims=True) * (1.0 / MLA_QK) + NORM_EPS)
        qa_ref[hd] = _rope(blk * rr * gqa_a, tab_ref, 0, 8).astype(BF16)
        blk = kv[:, hd * LANES:(hd + 1) * LANES]
        rr = lax.rsqrt(jnp.sum(blk * blk, axis=-1, keepdims=True) * (1.0 / MLA_QK) + NORM_EPS)
        ka_ref[hd] = _rope(blk * rr * gka_a, tab_ref, 0, 8).astype(BF16)
    for pr in range(MLA_HEADS // 2):
        blk = kv[:, 1024 + pr * LANES:1024 + (pr + 1) * LANES]
        va_ref[2 * pr] = jnp.where(lo, blk, 0.0).astype(BF16)
        va_ref[2 * pr + 1] = jnp.where(lo, 0.0, blk).astype(BF16)

    def pair_norm(blk, gain):
        sq = blk * blk
        s_lo = jnp.sum(jnp.where(lo, sq, 0.0), axis=-1, keepdims=True)
        s_hi = jnp.sum(jnp.where(lo, 0.0, sq), axis=-1, keepdims=True)
        rr = jnp.where(lo, lax.rsqrt(s_lo * (1.0 / GQA_HEAD_DIM) + NORM_EPS),
                       lax.rsqrt(s_hi * (1.0 / GQA_HEAD_DIM) + NORM_EPS))
        return _rope(blk * rr * gain, tab_ref, 3, 16)

    gq_b = g128_ref[2:3, :]
    gk_b = g128_ref[3:4, :]
    for pr in range(GQA_HEADS // 2):
        blk = p[:, 512 + pr * LANES:512 + (pr + 1) * LANES]
        qb_ref[pr] = pair_norm(blk, gq_b).astype(BF16)
    kb = pair_norm(p[:, 1024:1152], gk_b)
    kb_sw = pltpu.roll(kb, 64, 1)
    kb_ref[0] = jnp.where(lo, kb, 0.0).astype(BF16)
    kb_ref[1] = jnp.where(lo, 0.0, kb_sw).astype(BF16)
    kb_ref[2] = jnp.where(lo, kb_sw, 0.0).astype(BF16)
    kb_ref[3] = jnp.where(lo, 0.0, kb).astype(BF16)
    vb = p[:, 1152:1280]
    vb_sw = pltpu.roll(vb, 64, 1)
    vb_ref[0] = jnp.where(lo, vb, 0.0).astype(BF16)
    vb_ref[1] = jnp.where(lo, 0.0, vb_sw).astype(BF16)
    vb_ref[2] = jnp.where(lo, vb_sw, 0.0).astype(BF16)
    vb_ref[3] = jnp.where(lo, 0.0, vb).astype(BF16)


def _mod_index(i):
    return jnp.where(i < LAT_TILES, i // SEQ_TILES, BATCH)


def _projection(xall, mod3, gn, w_in, glat, w_uq, w_kv, g128, tab):
    const2 = lambda i: (0, 0)
    head_out = lambda n: pl.BlockSpec((n, TM, LANES), lambda i: (0, i, 0))
    head_shape = lambda n: jax.ShapeDtypeStruct((n, N_TOK, LANES), BF16)
    return pl.pallas_call(
        _proj_kernel,
        grid=(TOK_TILES,),
        in_specs=[pl.BlockSpec((TM, D_MODEL), lambda i: (i, 0)),
                  pl.BlockSpec((1, 6, D_MODEL), lambda i: (_mod_index(i), 0, 0)),
                  pl.BlockSpec((1, D_MODEL), const2),
                  pl.BlockSpec((D_MODEL, IN_PAD), const2),
                  pl.BlockSpec((2, 256), const2),
                  pl.BlockSpec((MLA_Q_RANK, MLA_HEADS * LANES), const2),
                  pl.BlockSpec((256, MLA_HEADS * LANES + MLA_HEADS * MLA_V), const2),
                  pl.BlockSpec((8, LANES), const2),
                  pl.BlockSpec((6, TM, LANES),
                               lambda i: (0, jnp.where(i < LAT_TILES, i % SEQ_TILES, SEQ_TILES), 0))],
        out_specs=[head_out(8), head_out(8), head_out(8), head_out(4), head_out(4), head_out(4)],
        out_shape=[head_shape(8), head_shape(8), head_shape(8), head_shape(4), head_shape(4),
                   head_shape(4)],
        compiler_params=_params("parallel"),
        name="projection",
    )(xall, mod3, gn, w_in, glat, w_uq, w_kv, g128, tab)


def _attn_kernel(*refs, with_lat):
    if with_lat:
        q0_ref, q1_ref, kl_ref, kc_ref, vl_ref, vc_ref, o_ref = refs
    else:
        q0_ref, q1_ref, kc_ref, vc_ref, o_ref = refs
    nt = (((1,), (1,)), ((), ()))
    acc = None
    for u, q_ref in enumerate((q0_ref, q1_ref)):
        q = q_ref[...]
        s_c = lax.dot_general(q, kc_ref[u], nt, preferred_element_type=F32)
        m = jnp.max(s_c, axis=-1, keepdims=True)
        if with_lat:
            s_l = lax.dot_general(q, kl_ref[u], nt, preferred_element_type=F32)
            m = jnp.maximum(m, jnp.max(s_l, axis=-1, keepdims=True))
            p_l = jnp.exp(s_l - m)
        p_c = jnp.exp(s_c - m)
        l = jnp.sum(p_c, axis=-1, keepdims=True)
        o = jnp.dot(p_c.astype(BF16), vc_ref[u], preferred_element_type=F32)
        if with_lat:
            l = l + jnp.sum(p_l, axis=-1, keepdims=True)
            o = o + jnp.dot(p_l.astype(BF16), vl_ref[u], preferred_element_type=F32)
        o = o / l
        acc = o if acc is None else acc + o
    o_ref[...] = acc.astype(o_ref.dtype)


def _attention(q, k, v, q_heads, kv_pair):
    n_units = 4
    qt = SEQ // TQ
    ctx_blk0 = N_LAT // CTX_LEN

    def qspec(which):
        return pl.BlockSpec((None, TQ, LANES), lambda b, p, i: (q_heads(p)[which], b * qt + i, 0))

    lat = pl.BlockSpec((2, SEQ, LANES), lambda b, p, i: (kv_pair(p), b, 0))
    ctx = pl.BlockSpec((2, CTX_LEN, LANES), lambda b, p, i: (kv_pair(p), ctx_blk0 + b, 0))
    return pl.pallas_call(
        functools.partial(_attn_kernel, with_lat=True),
        grid=(BATCH, n_units, qt),
        in_specs=[qspec(0), qspec(1), lat, ctx, lat, ctx],
        out_specs=pl.BlockSpec((TQ, LANES), lambda b, p, i: (b * qt + i, p)),
        out_shape=jax.ShapeDtypeStruct((N_LAT, n_units * LANES), BF16),
        compiler_params=_params("parallel", "parallel", "parallel"),
        name="attention_latent",
    )(q, q, k, k, v, v)


def _attention_ctx(q, k, v, q_heads, kv_pair):
    n_units = 4
    ctx_blk0 = N_LAT // CTX_LEN

    def qspec(which):
        return pl.BlockSpec((None, CTX_LEN, LANES), lambda b, p: (q_heads(p)[which], ctx_blk0 + b, 0))

    ctx = pl.BlockSpec((2, CTX_LEN, LANES), lambda b, p: (kv_pair(p), ctx_blk0 + b, 0))
    return pl.pallas_call(
        functools.partial(_attn_kernel, with_lat=False),
        grid=(BATCH, n_units),
        in_specs=[qspec(0), qspec(1), ctx, ctx],
        out_specs=pl.BlockSpec((CTX_LEN, LANES), lambda b, p: (b, p)),
        out_shape=jax.ShapeDtypeStruct((N_CTX, n_units * LANES), BF16),
        compiler_params=_params("parallel", "parallel"),
        name="attention_context",
    )(q, q, k, v)


_A_HEADS = lambda p: (2 * p, 2 * p + 1)
_A_KV = lambda p: p
_B_HEADS = lambda p: (p, p)
_B_KV = lambda p: p // 2


def _post_body(x_ref, oa, ob, mod_ref, gn_ref, wo_ref):
    y = (jnp.dot(oa, wo_ref[0:512, :], preferred_element_type=F32)
         + jnp.dot(ob, wo_ref[512:1024, :], preferred_element_type=F32))
    x1 = x_ref[...] + mod_ref[0, 2:3, :] * y
    r = lax.rsqrt(jnp.mean(x1 * x1, axis=-1, keepdims=True) + NORM_EPS)
    h2 = (x1 * r * gn_ref[...]) * (1.0 + mod_ref[0, 4:5, :]) + mod_ref[0, 3:4, :]
    return x1, h2


def _post_kernel(x_ref, oa_ref, ob_ref, oac_ref, obc_ref, mod_ref, gn_ref, wo_ref, x1_ref, h2_ref):
    latent = pl.program_id(0) < LAT_TILES
    oa = jnp.where(latent, oa_ref[...], oac_ref[...])
    ob = jnp.where(latent, ob_ref[...], obc_ref[...])
    x1, h2 = _post_body(x_ref, oa, ob, mod_ref, gn_ref, wo_ref)
    x1_ref[...] = x1
    h2_ref[...] = h2.astype(BF16)


def _post_router_kernel(x_ref, oa_ref, ob_ref, mod_ref, gn_ref, wo_ref, wr_ref,
                        x1_ref, hrow_ref, route_ref, count_ref, carry_ref):
    x1, h2 = _post_body(x_ref, oa_ref[...], ob_ref[...], mod_ref, gn_ref, wo_ref)
    x1_ref[...] = x1
    for s in range(ROW_SUB):
        hrow_ref[pl.ds(s, TM, stride=ROW_SUB), :] = h2[:, s * LANES:(s + 1) * LANES]

    logits = jnp.dot(h2, wr_ref[...], precision=lax.Precision.HIGHEST, preferred_element_type=F32)
    lane = lax.broadcasted_iota(jnp.int32, logits.shape, 1)
    valid = lane < N_EXPERTS
    lg = jnp.where(valid, logits, -jnp.inf)
    e = jnp.exp(lg - jnp.max(lg, axis=-1, keepdims=True))
    probs = e / jnp.sum(e, axis=-1, keepdims=True)
    v1 = jnp.max(probs, axis=-1, keepdims=True)
    i1 = jnp.min(jnp.where(probs == v1, lane, LANES), axis=-1, keepdims=True)
    rest = jnp.where(valid & (lane != i1), probs, -1.0)
    v2 = jnp.max(rest, axis=-1, keepdims=True)
    i2 = jnp.min(jnp.where(rest == v2, lane, LANES), axis=-1, keepdims=True)

    @pl.when(pl.program_id(0) == 0)
    def _():
        carry_ref[...] = jnp.zeros_like(carry_ref)

    pick1 = lane == i1
    pick2 = lane == i2
    onehot = jnp.where(pick1, 1.0, 0.0) + jnp.where(pick2, 1.0, 0.0)
    row = lax.broadcasted_iota(jnp.int32, (TM, TM), 0)
    col = lax.broadcasted_iota(jnp.int32, (TM, TM), 1)
    before = jnp.where(col < row, 1.0, 0.0).astype(BF16)
    seen = jnp.dot(before, onehot.astype(BF16), preferred_element_type=F32) + carry_ref[0:1, :]
    rank1 = jnp.sum(jnp.where(pick1, seen, 0.0), axis=-1, keepdims=True)
    rank2 = jnp.sum(jnp.where(pick2, seen, 0.0), axis=-1, keepdims=True)
    carry_ref[...] = carry_ref[...] + jnp.sum(onehot, axis=0, keepdims=True)
    count_ref[...] = carry_ref[...]

    fields = (v1 / (v1 + v2), v2 / (v1 + v2), i1.astype(F32), i2.astype(F32), rank1, rank2)
    route = jnp.zeros((TM, LANES), F32)
    for k, val in enumerate(fields):
        route = jnp.where(lane == k, val, route)
    route_ref[...] = route


def _post_attention(xall, o_lat, o_ctx, mod3, gn, w_out, w_router):
    const2 = lambda i: (0, 0)
    n_tiles = LAT_TILES if o_ctx is None else TOK_TILES
    rows = n_tiles * TM
    lat_spec = pl.BlockSpec((TM, 512), lambda i: (jnp.minimum(i, LAT_TILES - 1), 0))
    ctx_spec = pl.BlockSpec((TM, 512), lambda i: (jnp.maximum(i - LAT_TILES, 0), 0))
    tail_specs = [pl.BlockSpec((1, 6, D_MODEL), lambda i: (_mod_index(i), 0, 0)),
                  pl.BlockSpec((1, D_MODEL), const2),
                  pl.BlockSpec((D_MODEL, D_MODEL), const2)]
    x1_spec = pl.BlockSpec((TM, D_MODEL), lambda i: (i, 0))
    x1_shape = jax.ShapeDtypeStruct((rows, D_MODEL), F32)
    if o_ctx is not None:
        return pl.pallas_call(
            _post_kernel, grid=(n_tiles,),
            in_specs=[x1_spec, lat_spec, lat_spec, ctx_spec, ctx_spec] + tail_specs,
            out_specs=[x1_spec, x1_spec],
            out_shape=[x1_shape, jax.ShapeDtypeStruct((rows, D_MODEL), BF16)],
            compiler_params=_params("parallel"), name="post_attention",
        )(xall, *o_lat, *o_ctx, mod3, gn, w_out)
    in_specs = [x1_spec, lat_spec, lat_spec] + tail_specs
    args = [xall, *o_lat, mod3, gn, w_out]
    return pl.pallas_call(
        _post_router_kernel, grid=(n_tiles,),
        in_specs=in_specs + [pl.BlockSpec((D_MODEL, LANES), const2)],
        out_specs=[x1_spec,
                   pl.BlockSpec((TM * ROW_SUB, LANES), lambda i: (i, 0)),
                   pl.BlockSpec((TM, LANES), lambda i: (i, 0)),
                   pl.BlockSpec((8, LANES), const2)],
        out_shape=[x1_shape,
                   jax.ShapeDtypeStruct((rows * ROW_SUB, LANES), F32),
                   jax.ShapeDtypeStruct((rows, LANES), F32),
                   jax.ShapeDtypeStruct((8, LANES), F32)],
        scratch_shapes=[pltpu.VMEM((8, LANES), F32)],
        compiler_params=_params("arbitrary"), name="post_attention_router",
    )(*args, w_router)


def _swiglu_chunk(h, wg, wu, wd):
    g = jnp.dot(h, wg, preferred_element_type=F32)
    u = jnp.dot(h, wu, preferred_element_type=F32)
    return jnp.dot((_silu(g) * u).astype(BF16), wd, preferred_element_type=F32)


def _ffn_kernel(h_ref, x1_ref, mod_ref, wg_ref, wu_ref, wd_ref, o_ref):
    h = h_ref[...]
    y = None
    for c in range(DENSE_FF // FF_CHUNK):
        sl = slice(c * FF_CHUNK, (c + 1) * FF_CHUNK)
        yc = _swiglu_chunk(h, wg_ref[:, sl], wu_ref[:, sl], wd_ref[sl, :])
        y = yc if y is None else y + yc
    o_ref[...] = x1_ref[...] + mod_ref[0, 5:6, :] * y


def _dense_ffn(h2, x1, mod3, wg, wu, wd):
    resident = lambda shape: pl.BlockSpec(shape, lambda i: (0, 0), pipeline_mode=pl.Buffered(1))
    return pl.pallas_call(
        _ffn_kernel,
        grid=(TOK_TILES,),
        in_specs=[pl.BlockSpec((TM, D_MODEL), lambda i: (i, 0)),
                  pl.BlockSpec((TM, D_MODEL), lambda i: (i, 0)),
                  pl.BlockSpec((1, 6, D_MODEL), lambda i: (_mod_index(i), 0, 0)),
                  resident((D_MODEL, DENSE_FF)), resident((D_MODEL, DENSE_FF)),
                  resident((DENSE_FF, D_MODEL))],
        out_specs=pl.BlockSpec((TM, D_MODEL), lambda i: (i, 0)),
        out_shape=jax.ShapeDtypeStruct((N_TOK, D_MODEL), F32),
        compiler_params=_params("parallel"),
        name="dense_ffn",
    )(h2, x1, mod3, wg, wu, wd)


def _row(ref, r):
    return ref.at[pl.ds(pl.multiple_of(r * ROW_SUB, ROW_SUB), ROW_SUB)]


def _rows_to_matrix(ref, n):
    return jnp.concatenate([ref[pl.ds(s, n, stride=ROW_SUB), :] for s in range(ROW_SUB)], axis=-1)


def _dispatch_kernel(dest_ref, h_ref, xs_init_ref, xs_ref, sem):
    del xs_init_ref
    i = pl.program_id(0)

    def issue(j, carry):
        src = _row(h_ref, i * TM + j)
        for k in range(2):
            pltpu.make_async_copy(src, _row(xs_ref, dest_ref[k, j]), sem).start()
        return carry

    lax.fori_loop(0, TM, issue, 0, unroll=8)
    for k in range(2):
        pltpu.make_async_copy(h_ref.at[pl.ds(0, TM * ROW_SUB)], xs_ref.at[pl.ds(0, TM * ROW_SUB)],
                              sem).wait()


def _dispatch(dest3, hrow):
    xs_init = jnp.zeros((SORT_ROWS * ROW_SUB, LANES), F32)
    return pl.pallas_call(
        _dispatch_kernel,
        grid=(LAT_TILES,),
        in_specs=[pl.BlockSpec((None, 2, TM), lambda i: (i, 0, 0), memory_space=pltpu.SMEM),
                  pl.BlockSpec(memory_space=pl.ANY),
                  pl.BlockSpec(memory_space=pl.ANY)],
        out_specs=pl.BlockSpec(memory_space=pl.ANY),
        out_shape=jax.ShapeDtypeStruct((SORT_ROWS * ROW_SUB, LANES), F32),
        scratch_shapes=[pltpu.SemaphoreType.DMA(())],
        input_output_aliases={2: 0},
        compiler_params=_params("arbitrary"),
        name="moe_dispatch",
    )(dest3, hrow, xs_init)


def _expert_kernel(te_ref, nu_ref, xs_ref, wg_ref, wu_ref, wd_ref, y_ref):
    del te_ref
    used = pl.program_id(0) < nu_ref[0]

    @pl.when(jnp.logical_not(used))
    def _():
        y_ref[...] = jnp.zeros_like(y_ref)

    @pl.when(used)
    def _():
        x = _rows_to_matrix(xs_ref, TMX).astype(BF16)
        y = None
        for c in range(EXPERT_FF // FF_CHUNK):
            sl = slice(c * FF_CHUNK, (c + 1) * FF_CHUNK)
            yc = _swiglu_chunk(x, wg_ref[0, :, sl], wu_ref[0, :, sl], wd_ref[0, sl, :])
            y = yc if y is None else y + yc
        for s in range(ROW_SUB):
            y_ref[pl.ds(s, TMX, stride=ROW_SUB), :] = y[:, s * LANES:(s + 1) * LANES]


def _experts(tile_expert, n_used, xs, wg, wu, wd):
    one = pl.Buffered(1)
    grid_spec = pltpu.PrefetchScalarGridSpec(
        num_scalar_prefetch=2,
        grid=(SORT_TILES,),
        in_specs=[pl.BlockSpec((TMX * ROW_SUB, LANES),
                               lambda t, te, nu: (jnp.minimum(t, nu[0] - 1), 0)),
                  pl.BlockSpec((1, D_MODEL, EXPERT_FF), lambda t, te, nu: (te[t], 0, 0),
                               pipeline_mode=one),
                  pl.BlockSpec((1, D_MODEL, EXPERT_FF), lambda t, te, nu: (te[t], 0, 0),
                               pipeline_mode=one),
                  pl.BlockSpec((1, EXPERT_FF, D_MODEL), lambda t, te, nu: (te[t], 0, 0),
                               pipeline_mode=one)],
        out_specs=pl.BlockSpec((TMX * ROW_SUB, LANES), lambda t, te, nu: (t, 0)))
    return pl.pallas_call(
        _expert_kernel,
        grid_spec=grid_spec,
        out_shape=jax.ShapeDtypeStruct((SORT_ROWS * ROW_SUB, LANES), F32),
        compiler_params=_params("arbitrary"),
        name="moe_experts",
    )(tile_expert, n_used, xs, wg, wu, wd)


def _combine_kernel(dest_ref, y_ref, x1_ref, route_ref, mod_ref, o_ref, buf_ref, sem):
    def issue(j, carry):
        for k in range(2):
            pltpu.make_async_copy(_row(y_ref, dest_ref[k, j]), _row(buf_ref.at[k], j), sem).start()
        return carry

    lax.fori_loop(0, TM, issue, 0, unroll=8)
    for k in range(2):
        pltpu.make_async_copy(y_ref.at[pl.ds(0, TM * ROW_SUB)], buf_ref.at[k], sem).wait()
    route = route_ref[...]
    y = (route[:, 0:1] * _rows_to_matrix(buf_ref.at[0], TM)
         + route[:, 1:2] * _rows_to_matrix(buf_ref.at[1], TM))
    o_ref[...] = x1_ref[...] + mod_ref[0, 5:6, :] * y


def _combine(dest3, ys, x1, route, mod3):
    return pl.pallas_call(
        _combine_kernel,
        grid=(LAT_TILES,),
        in_specs=[pl.BlockSpec((None, 2, TM), lambda i: (i, 0, 0), memory_space=pltpu.SMEM),
                  pl.BlockSpec(memory_space=pl.ANY),
                  pl.BlockSpec((TM, D_MODEL), lambda i: (i, 0)),
                  pl.BlockSpec((TM, LANES), lambda i: (i, 0)),
                  pl.BlockSpec((1, 6, D_MODEL), lambda i: (i // SEQ_TILES, 0, 0))],
        out_specs=pl.BlockSpec((TM, D_MODEL), lambda i: (i, 0)),
        out_shape=jax.ShapeDtypeStruct((N_LAT, D_MODEL), F32),
        scratch_shapes=[pltpu.VMEM((2, TM * ROW_SUB, LANES), F32), pltpu.SemaphoreType.DMA(())],
        compiler_params=_params("arbitrary"),
        name="moe_combine",
    )(dest3, ys, x1, route, mod3)


def _moe_ffn(hrow, x1, route, counts, mod3, wg, wu, wd):
    expert = route[:, 2:4].astype(jnp.int32)
    rank = route[:, 4:6].astype(jnp.int32)
    tiles = (counts[0, :N_EXPERTS].astype(jnp.int32) + TMX - 1) // TMX
    tile_end = jnp.cumsum(tiles)
    dest = (tile_end - tiles)[expert] * TMX + rank
    dest3 = dest.reshape(LAT_TILES, TM, 2).transpose(0, 2, 1)
    n_used = tile_end[N_EXPERTS - 1:]
    tile_ids = jnp.minimum(jnp.arange(SORT_TILES), n_used[0] - 1)
    tile_expert = jnp.sum(tile_ids[:, None] >= tile_end[None, :], axis=1).astype(jnp.int32)
    xs = _dispatch(dest3, hrow)
    ys = _experts(tile_expert, n_used.astype(jnp.int32), xs, wg, wu, wd)
    return _combine(dest3, ys, x1, route, mod3)


def _rope_tables():
    t = jnp.arange(SEQ)
    row = (t // GRID_W).astype(F32)
    col = (t % GRID_W).astype(F32)

    def one_axis(pos, half):
        freqs = ROPE_THETA ** (-jnp.arange(half, dtype=F32) / half)
        ang = pos[:, None] * freqs[None, :]
        cos, sin = jnp.cos(ang), jnp.sin(ang)
        zero = jnp.zeros_like(sin)
        return (jnp.concatenate([cos, cos], -1), jnp.concatenate([-sin, zero], -1),
                jnp.concatenate([zero, sin], -1))

    def two_axes(half):
        r, c = one_axis(row, half), one_axis(col, half)
        return [jnp.concatenate([a, b], -1) for a, b in zip(r, c)]

    def pad_a(tbl, fill):
        return jnp.concatenate([jnp.full((SEQ, MLA_NOPE), fill, F32), tbl,
                                jnp.full((SEQ, LANES - MLA_QK), fill, F32)], -1)

    ta = two_axes(MLA_ROPE // 4)
    tb = [jnp.concatenate([x, x], -1) for x in two_axes(GQA_HEAD_DIM // 4)]
    tabs = [pad_a(ta[0], 1.0), pad_a(ta[1], 0.0), pad_a(ta[2], 0.0)] + tb
    ident = [jnp.ones((TM, LANES), F32), jnp.zeros((TM, LANES), F32), jnp.zeros((TM, LANES), F32)]
    return jnp.stack([jnp.concatenate([tbl, idn], 0) for tbl, idn in zip(tabs, ident + ident)])


def _layer_weights(w_in, q_lat_norm, w_uq, kv_lat_norm, w_ukv, mla_q_gain, mla_k_gain,
                   gqa_q_gain, gqa_k_gain):
    split = MLA_Q_RANK + MLA_KV_RANK + MLA_ROPE
    w_in_p = jnp.concatenate([w_in[:, :split], jnp.zeros((D_MODEL, 512 - split), F32),
                              w_in[:, split:]], axis=1).astype(BF16)
    w_uq_p = jnp.pad(w_uq.reshape(MLA_Q_RANK, MLA_HEADS, MLA_QK),
                     ((0, 0), (0, 0), (0, LANES - MLA_QK))).reshape(MLA_Q_RANK, MLA_HEADS * LANES)
    ukv = w_ukv.reshape(MLA_KV_RANK, MLA_HEADS, MLA_NOPE + MLA_V)
    w_k = jnp.pad(ukv[:, :, :MLA_NOPE], ((0, 0), (0, 0), (0, LANES - MLA_NOPE)))
    place = jnp.pad(jnp.eye(MLA_ROPE, dtype=F32), ((0, 0), (MLA_NOPE, LANES - MLA_QK)))
    place = jnp.broadcast_to(place[:, None, :], (MLA_ROPE, MLA_HEADS, LANES))
    w_k = jnp.concatenate([w_k, place, jnp.zeros((256 - MLA_KV_RANK - MLA_ROPE, MLA_HEADS, LANES), F32)], 0)
    w_v = jnp.pad(ukv[:, :, MLA_NOPE:].reshape(MLA_KV_RANK, MLA_HEADS * MLA_V),
                  ((0, 256 - MLA_KV_RANK), (0, 0)))
    w_kv = jnp.concatenate([w_k.reshape(256, MLA_HEADS * LANES), w_v], axis=1)
    glat = jnp.stack([q_lat_norm, jnp.pad(kv_lat_norm, (0, 256 - MLA_KV_RANK))])
    pad_qk = lambda g: jnp.pad(g, (0, LANES - MLA_QK))
    g128 = jnp.stack([pad_qk(mla_q_gain) * MLA_SCALE, pad_qk(mla_k_gain),
                      jnp.tile(gqa_q_gain, 2) * GQA_SCALE, jnp.tile(gqa_k_gain, 2)])
    g128 = jnp.pad(g128, ((0, 4), (0, 0)))
    return w_in_p, glat, w_uq_p.astype(BF16), w_kv.astype(BF16), g128


def _mixer(xall, mod3, tab, norm_attn, w_in, q_lat_norm, w_uq, kv_lat_norm, w_ukv, mla_q_gain,
           mla_k_gain, gqa_q_gain, gqa_k_gain, with_ctx_queries):
    w_in_p, glat, w_uq_p, w_kv, g128 = _layer_weights(
        w_in, q_lat_norm, w_uq, kv_lat_norm, w_ukv, mla_q_gain, mla_k_gain, gqa_q_gain, gqa_k_gain)
    qa, ka, va, qb, kb, vb = _projection(xall, mod3, norm_attn.reshape(1, D_MODEL), w_in_p, glat,
                                         w_uq_p, w_kv, g128, tab)
    o_lat = (_attention(qa, ka, va, _A_HEADS, _A_KV), _attention(qb, kb, vb, _B_HEADS, _B_KV))
    if not with_ctx_queries:
        return o_lat, None
    return o_lat, (_attention_ctx(qa, ka, va, _A_HEADS, _A_KV),
                   _attention_ctx(qb, kb, vb, _B_HEADS, _B_KV))


def kernel(x, c, ctx, c_ctx, l0_w_mod, l0_b_mod, l0_norm_attn, l0_w_in, l0_q_lat_norm, l0_w_uq, l0_kv_lat_norm, l0_w_ukv, l0_mla_q_gain, l0_mla_k_gain, l0_gqa_q_gain, l0_gqa_k_gain, l0_w_out, l0_norm_ffn, l0_ffn_w_gate, l0_ffn_w_up, l0_ffn_w_down, l1_w_mod, l1_b_mod, l1_norm_attn, l1_w_in, l1_q_lat_norm, l1_w_uq, l1_kv_lat_norm, l1_w_ukv, l1_mla_q_gain, l1_mla_k_gain, l1_gqa_q_gain, l1_gqa_k_gain, l1_w_out, l1_norm_ffn, l1_router, l1_exp_w_gate, l1_exp_w_up, l1_exp_w_down):
    xall = jnp.concatenate([x.reshape(N_LAT, D_MODEL), ctx.reshape(N_CTX, D_MODEL)], axis=0)
    cc = jnp.concatenate([c, c_ctx[None, :], jnp.zeros((16 - BATCH - 1, D_MODEL), F32)], axis=0)
    tab = _rope_tables()

    mod3 = _modulation(cc, l0_w_mod, l0_b_mod)
    o_lat, o_ctx = _mixer(xall, mod3, tab, l0_norm_attn, l0_w_in, l0_q_lat_norm, l0_w_uq,
                          l0_kv_lat_norm, l0_w_ukv, l0_mla_q_gain, l0_mla_k_gain, l0_gqa_q_gain,
                          l0_gqa_k_gain, with_ctx_queries=True)
    x1, h2 = _post_attention(xall, o_lat, o_ctx, mod3, l0_norm_ffn.reshape(1, D_MODEL),
                             l0_w_out.astype(BF16), None)
    xall = _dense_ffn(h2, x1, mod3, l0_ffn_w_gate.astype(BF16), l0_ffn_w_up.astype(BF16),
                      l0_ffn_w_down.astype(BF16))

    mod3 = _modulation(cc, l1_w_mod, l1_b_mod)
    o_lat, _ = _mixer(xall, mod3, tab, l1_norm_attn, l1_w_in, l1_q_lat_norm, l1_w_uq,
                      l1_kv_lat_norm, l1_w_ukv, l1_mla_q_gain, l1_mla_k_gain, l1_gqa_q_gain,
                      l1_gqa_k_gain, with_ctx_queries=False)
    w_router = jnp.pad(l1_router, ((0, 0), (0, LANES - N_EXPERTS)))
    x1, hrow, route, counts = _post_attention(xall, o_lat, None, mod3,
                                              l1_norm_ffn.reshape(1, D_MODEL),
                                              l1_w_out.astype(BF16), w_router)
    out = _moe_ffn(hrow, x1, route, counts, mod3, l1_exp_w_gate.astype(BF16),
                   l1_exp_w_up.astype(BF16), l1_exp_w_down.astype(BF16))
    return out.reshape(BATCH, SEQ, D_MODEL)
```

```python
import functools
import math

import jax
import jax.numpy as jnp
import numpy as np
from jax import lax
from jax.experimental import pallas as pl
from jax.experimental.pallas import tpu as pltpu

D_MODEL = 1024
BATCH = 8
SEQ = 2048
CTX_LEN = 256
GRID_W = 64
MLA_HEADS = 8
MLA_NOPE = 64
MLA_ROPE = 32
MLA_V = 64
MLA_QK = MLA_NOPE + MLA_ROPE
MLA_Q_RANK = 256
MLA_KV_RANK = 128
GQA_HEADS = 8
GQA_KV_HEADS = 2
GQA_HEAD_DIM = 64
DENSE_FF = 2816
N_EXPERTS = 8
EXPERT_FF = 2816
ROPE_THETA = 10000.0
NORM_EPS = 1e-6
LOG2E = math.log2(math.e)
MLA_SCALE = MLA_QK ** -0.5
GQA_SCALE = GQA_HEAD_DIM ** -0.5

LANES = 128
N_LAT = BATCH * SEQ
N_CTX = BATCH * CTX_LEN
N_TOK = N_LAT + N_CTX
TM = 512
LAT_TILES = N_LAT // TM
TOK_TILES = N_TOK // TM
SEQ_TILES = SEQ // TM
TQ = 512
V_WIDTH = 2 * LANES
IN_PAD = 1920
FF_CHUNK = 1408
ROW_SUB = D_MODEL // LANES
TMX = 512
SORT_TILES = 2 * N_LAT // TMX + N_EXPERTS
SORT_ROWS = SORT_TILES * TMX
VMEM_LIMIT = 56 * 1024 * 1024

F32 = jnp.float32
BF16 = jnp.bfloat16


def _silu(x):
    return x / (1.0 + jnp.exp(-x))


def _params(*sem):
    return pltpu.CompilerParams(dimension_semantics=sem, vmem_limit_bytes=VMEM_LIMIT)


def _mod_kernel(c_ref, w_ref, b_ref, o_ref):
    s = _silu(c_ref[...])
    o_ref[...] = jnp.dot(s, w_ref[...], precision=lax.Precision.HIGHEST,
                         preferred_element_type=F32) + b_ref[...]


def _modulation(cc, w_mod, b_mod):
    n = w_mod.shape[1]
    bn = 1024
    out = pl.pallas_call(
        _mod_kernel,
        grid=(n // bn,),
        in_specs=[pl.BlockSpec((16, D_MODEL), lambda j: (0, 0)),
                  pl.BlockSpec((D_MODEL, bn), lambda j: (0, j)),
                  pl.BlockSpec((1, bn), lambda j: (0, j))],
        out_specs=pl.BlockSpec((16, bn), lambda j: (0, j)),
        out_shape=jax.ShapeDtypeStruct((16, n), F32),
        compiler_params=_params("parallel"),
        name="modulation",
    )(cc, w_mod, b_mod.reshape(1, n))
    return out.reshape(16, 6, D_MODEL)


def _proj_kernel(x_ref, mod_ref, gn_ref, win_ref, glat_ref, wuq_ref, wkv_ref, g128_ref, tab_ref,
                 qa_ref, ka_ref, va_ref, qb_ref, kb_ref, vb_ref):
    x = x_ref[...]
    r = lax.rsqrt(jnp.mean(x * x, axis=-1, keepdims=True) + NORM_EPS)
    h = (x * r * gn_ref[...]) * (1.0 + mod_ref[0, 1:2, :]) + mod_ref[0, 0:1, :]
    p = jnp.dot(h.astype(BF16), win_ref[...], preferred_element_type=F32)

    lane = lax.broadcasted_iota(jnp.int32, (TM, LANES), 1)
    lo = lane < 64
    ones_col = jnp.where(lane == 0, 1.0, 0.0).astype(BF16)

    def tables(base, row):
        return tab_ref[base] * g128_ref[row:row + 1, :], tab_ref[base + 1] * g128_ref[row + 1:row + 2, :]

    cq = p[:, 0:256]
    rq = lax.rsqrt(jnp.mean(cq * cq, axis=-1, keepdims=True) + NORM_EPS)
    qa = jnp.dot((cq * rq * glat_ref[0:1, :]).astype(BF16), wuq_ref[...],
                 preferred_element_type=F32)
    ckv = p[:, 256:384]
    rkv = lax.rsqrt(jnp.mean(ckv * ckv, axis=-1, keepdims=True) + NORM_EPS)
    slab = jnp.concatenate([ckv * rkv * glat_ref[1:2, 0:128], p[:, 384:512]], axis=-1)
    kv = jnp.dot(slab.astype(BF16), wkv_ref[...], preferred_element_type=F32)

    def head_a(src, hd, cos_g, sin_g):
        blk = src[:, hd * LANES:(hd + 1) * LANES]
        partner = src[:, 1024 + hd * LANES:1024 + (hd + 1) * LANES]
        rr = lax.rsqrt(jnp.sum(blk * blk, axis=-1, keepdims=True) * (1.0 / MLA_QK) + NORM_EPS)
        return ((blk * cos_g + partner * sin_g) * rr).astype(BF16)

    cq_g, sq_g = tables(0, 0)
    ck_g, sk_g = tables(0, 2)
    for hd in range(MLA_HEADS):
        qa_ref[hd] = head_a(qa, hd, cq_g, sq_g)
        ka_ref[hd] = head_a(kv, hd, ck_g, sk_g)
    for pr in range(MLA_HEADS // 2):
        blk = kv[:, 2048 + pr * LANES:2048 + (pr + 1) * LANES]
        va_ref[2 * pr, :, 0:LANES] = jnp.where(lo, blk, 0.0).astype(BF16)
        va_ref[2 * pr + 1, :, 0:LANES] = jnp.where(lo, 0.0, blk).astype(BF16)
    for hd in range(MLA_HEADS):
        va_ref[hd, :, LANES:V_WIDTH] = ones_col

    def pair_b(blk, partner, cos_g, sin_g):
        sq = blk * blk
        s_lo = jnp.sum(jnp.where(lo, sq, 0.0), axis=-1, keepdims=True)
        s_hi = jnp.sum(jnp.where(lo, 0.0, sq), axis=-1, keepdims=True)
        rr = jnp.where(lo, lax.rsqrt(s_lo * (1.0 / GQA_HEAD_DIM) + NORM_EPS),
                       lax.rsqrt(s_hi * (1.0 / GQA_HEAD_DIM) + NORM_EPS))
        return (blk * cos_g + partner * sin_g) * rr

    cq_g, sq_g = tables(2, 4)
    ck_g, sk_g = tables(2, 6)
    for pr in range(GQA_HEADS // 2):
        blk = p[:, 512 + pr * LANES:512 + (pr + 1) * LANES]
        partner = p[:, 1280 + pr * LANES:1280 + (pr + 1) * LANES]
        qb_ref[pr] = pair_b(blk, partner, cq_g, sq_g).astype(BF16)
    kb = pair_b(p[:, 1024:1152], p[:, 1792:1920], ck_g, sk_g)
    kb_sw = pltpu.roll(kb, 64, 1)
    kb_ref[0] = jnp.where(lo, kb, 0.0).astype(BF16)
    kb_ref[1] = jnp.where(lo, 0.0, kb_sw).astype(BF16)
    kb_ref[2] = jnp.where(lo, kb_sw, 0.0).astype(BF16)
    kb_ref[3] = jnp.where(lo, 0.0, kb).astype(BF16)
    vb = p[:, 1152:1280]
    vb_sw = pltpu.roll(vb, 64, 1)
    vb_ref[0, :, 0:LANES] = jnp.where(lo, vb, 0.0).astype(BF16)
    vb_ref[1, :, 0:LANES] = jnp.where(lo, 0.0, vb_sw).astype(BF16)
    vb_ref[2, :, 0:LANES] = jnp.where(lo, vb_sw, 0.0).astype(BF16)
    vb_ref[3, :, 0:LANES] = jnp.where(lo, 0.0, vb).astype(BF16)
    for j in range(2 * GQA_KV_HEADS):
        vb_ref[j, :, LANES:V_WIDTH] = ones_col


def _mod_index(i):
    return jnp.where(i < LAT_TILES, i // SEQ_TILES, BATCH)


def _projection(xall, mod3, gn, w_in, glat, w_uq, w_kv, g128, tab):
    const2 = lambda i: (0, 0)
    head_out = lambda n, w: pl.BlockSpec((n, TM, w), lambda i: (0, i, 0))
    head_shape = lambda n, w: jax.ShapeDtypeStruct((n, N_TOK, w), BF16)
    return pl.pallas_call(
        _proj_kernel,
        grid=(TOK_TILES,),
        in_specs=[pl.BlockSpec((TM, D_MODEL), lambda i: (i, 0)),
                  pl.BlockSpec((1, 6, D_MODEL), lambda i: (_mod_index(i), 0, 0)),
                  pl.BlockSpec((1, D_MODEL), const2),
                  pl.BlockSpec((D_MODEL, IN_PAD), const2),
                  pl.BlockSpec((2, 256), const2),
                  pl.BlockSpec((MLA_Q_RANK, 2 * MLA_HEADS * LANES), const2),
                  pl.BlockSpec((256, 2 * MLA_HEADS * LANES + MLA_HEADS * MLA_V), const2),
                  pl.BlockSpec((8, LANES), const2),
                  pl.BlockSpec((4, TM, LANES),
                               lambda i: (0, jnp.where(i < LAT_TILES, i % SEQ_TILES, SEQ_TILES), 0))],
        out_specs=[head_out(8, LANES), head_out(8, LANES), head_out(8, V_WIDTH),
                   head_out(4, LANES), head_out(4, LANES), head_out(4, V_WIDTH)],
        out_shape=[head_shape(8, LANES), head_shape(8, LANES), head_shape(8, V_WIDTH),
                   head_shape(4, LANES), head_shape(4, LANES), head_shape(4, V_WIDTH)],
        compiler_params=_params("parallel"),
        name="projection",
    )(xall, mod3, gn, w_in, glat, w_uq, w_kv, g128, tab)


def _attn_kernel(*refs, with_lat):
    if with_lat:
        q0_ref, q1_ref, kl_ref, kc_ref, vl_ref, vc_ref, o_ref = refs
    else:
        q0_ref, q1_ref, kc_ref, vc_ref, o_ref = refs
    nt = (((1,), (1,)), ((), ()))
    out = None
    for u, q_ref in enumerate((q0_ref, q1_ref)):
        q = q_ref[...]
        s_c = lax.dot_general(q, kc_ref[u], nt, preferred_element_type=F32)
        m = jnp.max(s_c, axis=-1, keepdims=True)
        if with_lat:
            s_l = lax.dot_general(q, kl_ref[u], nt, preferred_element_type=F32)
            m = jnp.maximum(m, jnp.max(s_l, axis=-1, keepdims=True))
        acc = jnp.dot(jnp.exp2((s_c - m).astype(BF16)), vc_ref[u], preferred_element_type=F32)
        if with_lat:
            acc = acc + jnp.dot(jnp.exp2((s_l - m).astype(BF16)), vl_ref[u],
                                preferred_element_type=F32)
        o = acc[:, 0:LANES] / acc[:, LANES:LANES + 1]
        out = o if out is None else out + o
    o_ref[...] = out.astype(o_ref.dtype)


def _attention(q, k, v, q_heads, kv_pair):
    n_units = 4
    qt = SEQ // TQ
    ctx_blk0 = N_LAT // CTX_LEN

    def qspec(which):
        return pl.BlockSpec((None, TQ, LANES), lambda b, p, i: (q_heads(p)[which], b * qt + i, 0))

    def kvspec(rows, blk0, width):
        return pl.BlockSpec((2, rows, width), lambda b, p, i: (kv_pair(p), blk0 + b, 0))

    return pl.pallas_call(
        functools.partial(_attn_kernel, with_lat=True),
        grid=(BATCH, n_units, qt),
        in_specs=[qspec(0), qspec(1), kvspec(SEQ, 0, LANES), kvspec(CTX_LEN, ctx_blk0, LANES),
                  kvspec(SEQ, 0, V_WIDTH), kvspec(CTX_LEN, ctx_blk0, V_WIDTH)],
        out_specs=pl.BlockSpec((TQ, LANES), lambda b, p, i: (b * qt + i, p)),
        out_shape=jax.ShapeDtypeStruct((N_LAT, n_units * LANES), BF16),
        compiler_params=_params("parallel", "parallel", "parallel"),
        name="attention_latent",
    )(q, q, k, k, v, v)


def _attention_ctx(q, k, v, q_heads, kv_pair):
    n_units = 4
    ctx_blk0 = N_LAT // CTX_LEN

    def qspec(which):
        return pl.BlockSpec((None, CTX_LEN, LANES), lambda b, p: (q_heads(p)[which], ctx_blk0 + b, 0))

    def kvspec(width):
        return pl.BlockSpec((2, CTX_LEN, width), lambda b, p: (kv_pair(p), ctx_blk0 + b, 0))

    return pl.pallas_call(
        functools.partial(_attn_kernel, with_lat=False),
        grid=(BATCH, n_units),
        in_specs=[qspec(0), qspec(1), kvspec(LANES), kvspec(V_WIDTH)],
        out_specs=pl.BlockSpec((CTX_LEN, LANES), lambda b, p: (b, p)),
        out_shape=jax.ShapeDtypeStruct((N_CTX, n_units * LANES), BF16),
        compiler_params=_params("parallel", "parallel"),
        name="attention_context",
    )(q, q, k, v)


_A_HEADS = lambda p: (2 * p, 2 * p + 1)
_A_KV = lambda p: p
_B_HEADS = lambda p: (p, p)
_B_KV = lambda p: p // 2


def _post_body(x_ref, oa, ob, mod_ref, gn_ref, wo_ref):
    y = (jnp.dot(oa, wo_ref[0:512, :], preferred_element_type=F32)
         + jnp.dot(ob, wo_ref[512:1024, :], preferred_element_type=F32))
    x1 = x_ref[...] + mod_ref[0, 2:3, :] * y
    r = lax.rsqrt(jnp.mean(x1 * x1, axis=-1, keepdims=True) + NORM_EPS)
    h2 = (x1 * r * gn_ref[...]) * (1.0 + mod_ref[0, 4:5, :]) + mod_ref[0, 3:4, :]
    return x1, h2


def _post_kernel(x_ref, oa_ref, ob_ref, oac_ref, obc_ref, mod_ref, gn_ref, wo_ref, x1_ref, h2_ref):
    latent = pl.program_id(0) < LAT_TILES
    oa = jnp.where(latent, oa_ref[...], oac_ref[...])
    ob = jnp.where(latent, ob_ref[...], obc_ref[...])
    x1, h2 = _post_body(x_ref, oa, ob, mod_ref, gn_ref, wo_ref)
    x1_ref[...] = x1
    h2_ref[...] = h2.astype(BF16)


def _post_router_kernel(x_ref, oa_ref, ob_ref, mod_ref, gn_ref, wo_ref, wr_ref,
                        x1_ref, hrow_ref, route_ref, count_ref, carry_ref):
    x1, h2 = _post_body(x_ref, oa_ref[...], ob_ref[...], mod_ref, gn_ref, wo_ref)
    x1_ref[...] = x1
    for s in range(ROW_SUB):
        hrow_ref[pl.ds(s, TM, stride=ROW_SUB), :] = h2[:, s * LANES:(s + 1) * LANES]

    logits = jnp.dot(h2, wr_ref[...], precision=lax.Precision.HIGHEST, preferred_element_type=F32)
    lane = lax.broadcasted_iota(jnp.int32, logits.shape, 1)
    valid = lane < N_EXPERTS
    lg = jnp.where(valid, logits, -jnp.inf)
    e = jnp.exp(lg - jnp.max(lg, axis=-1, keepdims=True))
    probs = e / jnp.sum(e, axis=-1, keepdims=True)
    v1 = jnp.max(probs, axis=-1, keepdims=True)
    i1 = jnp.min(jnp.where(probs == v1, lane, LANES), axis=-1, keepdims=True)
    rest = jnp.where(valid & (lane != i1), probs, -1.0)
    v2 = jnp.max(rest, axis=-1, keepdims=True)
    i2 = jnp.min(jnp.where(rest == v2, lane, LANES), axis=-1, keepdims=True)

    @pl.when(pl.program_id(0) == 0)
    def _():
        carry_ref[...] = jnp.zeros_like(carry_ref)

    pick1 = lane == i1
    pick2 = lane == i2
    onehot = jnp.where(pick1, 1.0, 0.0) + jnp.where(pick2, 1.0, 0.0)
    row = lax.broadcasted_iota(jnp.int32, (TM, TM), 0)
    col = lax.broadcasted_iota(jnp.int32, (TM, TM), 1)
    before = jnp.where(col < row, 1.0, 0.0).astype(BF16)
    seen = jnp.dot(before, onehot.astype(BF16), preferred_element_type=F32) + carry_ref[0:1, :]
    rank1 = jnp.sum(jnp.where(pick1, seen, 0.0), axis=-1, keepdims=True)
    rank2 = jnp.sum(jnp.where(pick2, seen, 0.0), axis=-1, keepdims=True)
    carry_ref[...] = carry_ref[...] + jnp.sum(onehot, axis=0, keepdims=True)
    count_ref[...] = carry_ref[...]

    fields = (v1 / (v1 + v2), v2 / (v1 + v2), i1.astype(F32), i2.astype(F32), rank1, rank2)
    route = jnp.zeros((TM, LANES), F32)
    for k, val in enumerate(fields):
        route = jnp.where(lane == k, val, route)
    route_ref[...] = route


def _post_attention(xall, o_lat, o_ctx, mod3, gn, w_out, w_router):
    const2 = lambda i: (0, 0)
    n_tiles = LAT_TILES if o_ctx is None else TOK_TILES
    rows = n_tiles * TM
    lat_spec = pl.BlockSpec((TM, 512), lambda i: (jnp.minimum(i, LAT_TILES - 1), 0))
    ctx_spec = pl.BlockSpec((TM, 512), lambda i: (jnp.maximum(i - LAT_TILES, 0), 0))
    tail_specs = [pl.BlockSpec((1, 6, D_MODEL), lambda i: (_mod_index(i), 0, 0)),
                  pl.BlockSpec((1, D_MODEL), const2),
                  pl.BlockSpec((D_MODEL, D_MODEL), const2)]
    x1_spec = pl.BlockSpec((TM, D_MODEL), lambda i: (i, 0))
    x1_shape = jax.ShapeDtypeStruct((rows, D_MODEL), F32)
    if o_ctx is not None:
        return pl.pallas_call(
            _post_kernel, grid=(n_tiles,),
            in_specs=[x1_spec, lat_spec, lat_spec, ctx_spec, ctx_spec] + tail_specs,
            out_specs=[x1_spec, x1_spec],
            out_shape=[x1_shape, jax.ShapeDtypeStruct((rows, D_MODEL), BF16)],
            compiler_params=_params("parallel"), name="post_attention",
        )(xall, *o_lat, *o_ctx, mod3, gn, w_out)
    in_specs = [x1_spec, lat_spec, lat_spec] + tail_specs
    args = [xall, *o_lat, mod3, gn, w_out]
    return pl.pallas_call(
        _post_router_kernel, grid=(n_tiles,),
        in_specs=in_specs + [pl.BlockSpec((D_MODEL, LANES), const2)],
        out_specs=[x1_spec,
                   pl.BlockSpec((TM * ROW_SUB, LANES), lambda i: (i, 0)),
                   pl.BlockSpec((TM, LANES), lambda i: (i, 0)),
                   pl.BlockSpec((8, LANES), const2)],
        out_shape=[x1_shape,
                   jax.ShapeDtypeStruct((rows * ROW_SUB, LANES), F32),
                   jax.ShapeDtypeStruct((rows, LANES), F32),
                   jax.ShapeDtypeStruct((8, LANES), F32)],
        scratch_shapes=[pltpu.VMEM((8, LANES), F32)],
        compiler_params=_params("arbitrary"), name="post_attention_router",
    )(*args, w_router)


def _swiglu_chunk(h, wg, wu, wd):
    g = jnp.dot(h, wg, preferred_element_type=F32)
    u = jnp.dot(h, wu, preferred_element_type=F32)
    return jnp.dot((_silu(g) * u).astype(BF16), wd, preferred_element_type=F32)


def _ffn_kernel(h_ref, x1_ref, mod_ref, wg_ref, wu_ref, wd_ref, o_ref):
    h = h_ref[...]
    y = None
    for c in range(DENSE_FF // FF_CHUNK):
        sl = slice(c * FF_CHUNK, (c + 1) * FF_CHUNK)
        yc = _swiglu_chunk(h, wg_ref[:, sl], wu_ref[:, sl], wd_ref[sl, :])
        y = yc if y is None else y + yc
    o_ref[...] = x1_ref[...] + mod_ref[0, 5:6, :] * y


def _dense_ffn(h2, x1, mod3, wg, wu, wd):
    resident = lambda shape: pl.BlockSpec(shape, lambda i: (0, 0), pipeline_mode=pl.Buffered(1))
    return pl.pallas_call(
        _ffn_kernel,
        grid=(TOK_TILES,),
        in_specs=[pl.BlockSpec((TM, D_MODEL), lambda i: (i, 0)),
                  pl.BlockSpec((TM, D_MODEL), lambda i: (i, 0)),
                  pl.BlockSpec((1, 6, D_MODEL), lambda i: (_mod_index(i), 0, 0)),
                  resident((D_MODEL, DENSE_FF)), resident((D_MODEL, DENSE_FF)),
                  resident((DENSE_FF, D_MODEL))],
        out_specs=pl.BlockSpec((TM, D_MODEL), lambda i: (i, 0)),
        out_shape=jax.ShapeDtypeStruct((N_TOK, D_MODEL), F32),
        compiler_params=_params("parallel"),
        name="dense_ffn",
    )(h2, x1, mod3, wg, wu, wd)


def _row(ref, r):
    return ref.at[pl.ds(pl.multiple_of(r * ROW_SUB, ROW_SUB), ROW_SUB)]


def _rows_to_matrix(ref, n):
    return jnp.concatenate([ref[pl.ds(s, n, stride=ROW_SUB), :] for s in range(ROW_SUB)], axis=-1)


def _dispatch_kernel(dest_ref, h_ref, xs_init_ref, xs_ref, sem):
    del xs_init_ref

    def issue(j, carry):
        src = _row(h_ref, j)
        for k in range(2):
            pltpu.make_async_copy(src, _row(xs_ref, dest_ref[k, j]), sem).start()
        return carry

    lax.fori_loop(0, TM, issue, 0, unroll=8)
    for k in range(2):
        pltpu.make_async_copy(h_ref, xs_ref.at[pl.ds(0, TM * ROW_SUB)], sem).wait()


def _dispatch(dest3, hrow):
    xs_init = jnp.zeros((SORT_ROWS * ROW_SUB, LANES), F32)
    return pl.pallas_call(
        _dispatch_kernel,
        grid=(LAT_TILES,),
        in_specs=[pl.BlockSpec((None, 2, TM), lambda i: (i, 0, 0), memory_space=pltpu.SMEM),
                  pl.BlockSpec((TM * ROW_SUB, LANES), lambda i: (i, 0)),
                  pl.BlockSpec(memory_space=pl.ANY)],
        out_specs=pl.BlockSpec(memory_space=pl.ANY),
        out_shape=jax.ShapeDtypeStruct((SORT_ROWS * ROW_SUB, LANES), F32),
        scratch_shapes=[pltpu.SemaphoreType.DMA(())],
        input_output_aliases={2: 0},
        compiler_params=_params("arbitrary"),
        name="moe_dispatch",
    )(dest3, hrow, xs_init)


def _expert_kernel(te_ref, nu_ref, xs_ref, wg_ref, wu_ref, wd_ref, y_ref):
    del te_ref
    used = pl.program_id(0) < nu_ref[0]

    @pl.when(jnp.logical_not(used))
    def _():
        y_ref[...] = jnp.zeros_like(y_ref)

    @pl.when(used)
    def _():
        x = _rows_to_matrix(xs_ref, TMX).astype(BF16)
        y = None
        for c in range(EXPERT_FF // FF_CHUNK):
            sl = slice(c * FF_CHUNK, (c + 1) * FF_CHUNK)
            yc = _swiglu_chunk(x, wg_ref[0, :, sl], wu_ref[0, :, sl], wd_ref[0, sl, :])
            y = yc if y is None else y + yc
        for s in range(ROW_SUB):
            y_ref[pl.ds(s, TMX, stride=ROW_SUB), :] = y[:, s * LANES:(s + 1) * LANES]


def _experts(tile_expert, n_used, xs, wg, wu, wd):
    one = pl.Buffered(1)
    grid_spec = pltpu.PrefetchScalarGridSpec(
        num_scalar_prefetch=2,
        grid=(SORT_TILES,),
        in_specs=[pl.BlockSpec((TMX * ROW_SUB, LANES),
                               lambda t, te, nu: (jnp.minimum(t, nu[0] - 1), 0)),
                  pl.BlockSpec((1, D_MODEL, EXPERT_FF), lambda t, te, nu: (te[t], 0, 0),
                               pipeline_mode=one),
                  pl.BlockSpec((1, D_MODEL, EXPERT_FF), lambda t, te, nu: (te[t], 0, 0),
                               pipeline_mode=one),
                  pl.BlockSpec((1, EXPERT_FF, D_MODEL), lambda t, te, nu: (te[t], 0, 0),
                               pipeline_mode=one)],
        out_specs=pl.BlockSpec((TMX * ROW_SUB, LANES), lambda t, te, nu: (t, 0)))
    return pl.pallas_call(
        _expert_kernel,
        grid_spec=grid_spec,
        out_shape=jax.ShapeDtypeStruct((SORT_ROWS * ROW_SUB, LANES), F32),
        compiler_params=_params("arbitrary"),
        name="moe_experts",
    )(tile_expert, n_used, xs, wg, wu, wd)


def _combine_kernel(dest_ref, y_ref, x1_ref, route_ref, mod_ref, o_ref, buf_ref, sem):
    def issue(j, carry):
        for k in range(2):
            pltpu.make_async_copy(_row(y_ref, dest_ref[k, j]), _row(buf_ref.at[k], j), sem).start()
        return carry

    lax.fori_loop(0, TM, issue, 0, unroll=8)
    for k in range(2):
        pltpu.make_async_copy(y_ref.at[pl.ds(0, TM * ROW_SUB)], buf_ref.at[k], sem).wait()
    route = route_ref[...]
    y = (route[:, 0:1] * _rows_to_matrix(buf_ref.at[0], TM)
         + route[:, 1:2] * _rows_to_matrix(buf_ref.at[1], TM))
    o_ref[...] = x1_ref[...] + mod_ref[0, 5:6, :] * y


def _combine(dest3, ys, x1, route, mod3):
    return pl.pallas_call(
        _combine_kernel,
        grid=(LAT_TILES,),
        in_specs=[pl.BlockSpec((None, 2, TM), lambda i: (i, 0, 0), memory_space=pltpu.SMEM),
                  pl.BlockSpec(memory_space=pl.ANY),
                  pl.BlockSpec((TM, D_MODEL), lambda i: (i, 0)),
                  pl.BlockSpec((TM, LANES), lambda i: (i, 0)),
                  pl.BlockSpec((1, 6, D_MODEL), lambda i: (i // SEQ_TILES, 0, 0))],
        out_specs=pl.BlockSpec((TM, D_MODEL), lambda i: (i, 0)),
        out_shape=jax.ShapeDtypeStruct((N_LAT, D_MODEL), F32),
        scratch_shapes=[pltpu.VMEM((2, TM * ROW_SUB, LANES), F32), pltpu.SemaphoreType.DMA(())],
        compiler_params=_params("arbitrary"),
        name="moe_combine",
    )(dest3, ys, x1, route, mod3)


def _moe_ffn(hrow, x1, route, counts, mod3, wg, wu, wd):
    expert = route[:, 2:4].astype(jnp.int32)
    rank = route[:, 4:6].astype(jnp.int32)
    tiles = (counts[0, :N_EXPERTS].astype(jnp.int32) + TMX - 1) // TMX
    tile_end = jnp.cumsum(tiles)
    dest = (tile_end - tiles)[expert] * TMX + rank
    dest3 = dest.reshape(LAT_TILES, TM, 2).transpose(0, 2, 1)
    n_used = tile_end[N_EXPERTS - 1:]
    tile_ids = jnp.minimum(jnp.arange(SORT_TILES), n_used[0] - 1)
    tile_expert = jnp.sum(tile_ids[:, None] >= tile_end[None, :], axis=1).astype(jnp.int32)
    xs = _dispatch(dest3, hrow)
    ys = _experts(tile_expert, n_used.astype(jnp.int32), xs, wg, wu, wd)
    return _combine(dest3, ys, x1, route, mod3)


def _partner(half):
    idx = np.arange(4 * half)
    return np.where((idx // half) % 2 == 0, idx + half, idx - half)


_PARTNER_A = np.concatenate([np.arange(MLA_NOPE), MLA_NOPE + _partner(MLA_ROPE // 4),
                             np.arange(MLA_QK, LANES)])
_PARTNER_B = np.concatenate([_partner(GQA_HEAD_DIM // 4), GQA_HEAD_DIM + _partner(GQA_HEAD_DIM // 4)])


def _rope_tables():
    t = jnp.arange(SEQ)
    row = (t // GRID_W).astype(F32)
    col = (t % GRID_W).astype(F32)

    def one_axis(pos, half):
        freqs = ROPE_THETA ** (-jnp.arange(half, dtype=F32) / half)
        ang = pos[:, None] * freqs[None, :]
        cos, sin = jnp.cos(ang), jnp.sin(ang)
        return jnp.concatenate([cos, cos], -1), jnp.concatenate([-sin, sin], -1)

    def two_axes(half):
        r, c = one_axis(row, half), one_axis(col, half)
        return [jnp.concatenate([a, b], -1) for a, b in zip(r, c)]

    def pad_a(tbl, fill):
        return jnp.concatenate([jnp.full((SEQ, MLA_NOPE), fill, F32), tbl,
                                jnp.full((SEQ, LANES - MLA_QK), fill, F32)], -1)

    ta = two_axes(MLA_ROPE // 4)
    tb = [jnp.concatenate([x, x], -1) for x in two_axes(GQA_HEAD_DIM // 4)]
    tabs = [pad_a(ta[0], 1.0), pad_a(ta[1], 0.0)] + tb
    ident = [jnp.ones((TM, LANES), F32), jnp.zeros((TM, LANES), F32)]
    return jnp.stack([jnp.concatenate([tbl, idn], 0) for tbl, idn in zip(tabs, ident + ident)])


def _layer_weights(w_in, q_lat_norm, w_uq, kv_lat_norm, w_ukv, mla_q_gain, mla_k_gain,
                   gqa_q_gain, gqa_k_gain):
    split = MLA_Q_RANK + MLA_KV_RANK + MLA_ROPE
    perm_a = (np.arange(MLA_HEADS)[:, None] * LANES + _PARTNER_A[None, :]).reshape(-1)
    perm_b = (np.arange(GQA_HEADS // 2)[:, None] * LANES + _PARTNER_B[None, :]).reshape(-1)
    w_qb = w_in[:, split:split + GQA_HEADS * GQA_HEAD_DIM]
    w_kb = w_in[:, split + 512:split + 640]
    w_in_p = jnp.concatenate([w_in[:, :split], jnp.zeros((D_MODEL, 512 - split), F32),
                              w_in[:, split:], w_qb[:, perm_b], w_kb[:, _PARTNER_B]],
                             axis=1).astype(BF16)
    w_uq_p = jnp.pad(w_uq.reshape(MLA_Q_RANK, MLA_HEADS, MLA_QK),
                     ((0, 0), (0, 0), (0, LANES - MLA_QK))).reshape(MLA_Q_RANK, MLA_HEADS * LANES)
    w_uq_p = jnp.concatenate([w_uq_p, w_uq_p[:, perm_a]], axis=1)
    ukv = w_ukv.reshape(MLA_KV_RANK, MLA_HEADS, MLA_NOPE + MLA_V)
    w_k = jnp.pad(ukv[:, :, :MLA_NOPE], ((0, 0), (0, 0), (0, LANES - MLA_NOPE)))
    place = jnp.pad(jnp.eye(MLA_ROPE, dtype=F32), ((0, 0), (MLA_NOPE, LANES - MLA_QK)))
    place = jnp.broadcast_to(place[:, None, :], (MLA_ROPE, MLA_HEADS, LANES))
    w_k = jnp.concatenate([w_k, place,
                           jnp.zeros((256 - MLA_KV_RANK - MLA_ROPE, MLA_HEADS, LANES), F32)], 0)
    w_k = w_k.reshape(256, MLA_HEADS * LANES)
    w_v = jnp.pad(ukv[:, :, MLA_NOPE:].reshape(MLA_KV_RANK, MLA_HEADS * MLA_V),
                  ((0, 256 - MLA_KV_RANK), (0, 0)))
    w_kv = jnp.concatenate([w_k, w_k[:, perm_a], w_v], axis=1)
    glat = jnp.stack([q_lat_norm, jnp.pad(kv_lat_norm, (0, 256 - MLA_KV_RANK))])
    pad_qk = lambda g: jnp.pad(g, (0, LANES - MLA_QK))
    gains = [pad_qk(mla_q_gain) * (MLA_SCALE * LOG2E), pad_qk(mla_k_gain),
             jnp.tile(gqa_q_gain, 2) * (GQA_SCALE * LOG2E), jnp.tile(gqa_k_gain, 2)]
    partners = [_PARTNER_A, _PARTNER_A, _PARTNER_B, _PARTNER_B]
    g128 = jnp.stack([v for g, pm in zip(gains, partners) for v in (g, g[pm])])
    return w_in_p, glat, w_uq_p.astype(BF16), w_kv.astype(BF16), g128


def _mixer(xall, mod3, tab, norm_attn, w_in, q_lat_norm, w_uq, kv_lat_norm, w_ukv, mla_q_gain,
           mla_k_gain, gqa_q_gain, gqa_k_gain, with_ctx_queries):
    w_in_p, glat, w_uq_p, w_kv, g128 = _layer_weights(
        w_in, q_lat_norm, w_uq, kv_lat_norm, w_ukv, mla_q_gain, mla_k_gain, gqa_q_gain, gqa_k_gain)
    qa, ka, va, qb, kb, vb = _projection(xall, mod3, norm_attn.reshape(1, D_MODEL), w_in_p, glat,
                                         w_uq_p, w_kv, g128, tab)
    o_lat = (_attention(qa, ka, va, _A_HEADS, _A_KV), _attention(qb, kb, vb, _B_HEADS, _B_KV))
    if not with_ctx_queries:
        return o_lat, None
    return o_lat, (_attention_ctx(qa, ka, va, _A_HEADS, _A_KV),
                   _attention_ctx(qb, kb, vb, _B_HEADS, _B_KV))


def kernel(x, c, ctx, c_ctx, l0_w_mod, l0_b_mod, l0_norm_attn, l0_w_in, l0_q_lat_norm, l0_w_uq, l0_kv_lat_norm, l0_w_ukv, l0_mla_q_gain, l0_mla_k_gain, l0_gqa_q_gain, l0_gqa_k_gain, l0_w_out, l0_norm_ffn, l0_ffn_w_gate, l0_ffn_w_up, l0_ffn_w_down, l1_w_mod, l1_b_mod, l1_norm_attn, l1_w_in, l1_q_lat_norm, l1_w_uq, l1_kv_lat_norm, l1_w_ukv, l1_mla_q_gain, l1_mla_k_gain, l1_gqa_q_gain, l1_gqa_k_gain, l1_w_out, l1_norm_ffn, l1_router, l1_exp_w_gate, l1_exp_w_up, l1_exp_w_down):
    xall = jnp.concatenate([x.reshape(N_LAT, D_MODEL), ctx.reshape(N_CTX, D_MODEL)], axis=0)
    cc = jnp.concatenate([c, c_ctx[None, :], jnp.zeros((16 - BATCH - 1, D_MODEL), F32)], axis=0)
    tab = _rope_tables()

    mod3 = _modulation(cc, l0_w_mod, l0_b_mod)
    o_lat, o_ctx = _mixer(xall, mod3, tab, l0_norm_attn, l0_w_in, l0_q_lat_norm, l0_w_uq,
                          l0_kv_lat_norm, l0_w_ukv, l0_mla_q_gain, l0_mla_k_gain, l0_gqa_q_gain,
                          l0_gqa_k_gain, with_ctx_queries=True)
    x1, h2 = _post_attention(xall, o_lat, o_ctx, mod3, l0_norm_ffn.reshape(1, D_MODEL),
                             l0_w_out.astype(BF16), None)
    xall = _dense_ffn(h2, x1, mod3, l0_ffn_w_gate.astype(BF16), l0_ffn_w_up.astype(BF16),
                      l0_ffn_w_down.astype(BF16))

    mod3 = _modulation(cc, l1_w_mod, l1_b_mod)
    o_lat, _ = _mixer(xall, mod3, tab, l1_norm_attn, l1_w_in, l1_q_lat_norm, l1_w_uq,
                      l1_kv_lat_norm, l1_w_ukv, l1_mla_q_gain, l1_mla_k_gain, l1_gqa_q_gain,
                      l1_gqa_k_gain, with_ctx_queries=False)
    w_router = jnp.pad(l1_router, ((0, 0), (0, LANES - N_EXPERTS)))
    x1, hrow, route, counts = _post_attention(xall, o_lat, None, mod3,
                                              l1_norm_ffn.reshape(1, D_MODEL),
                                              l1_w_out.astype(BF16), w_router)
    out = _moe_ffn(hrow, x1, route, counts, mod3, l1_exp_w_gate.astype(BF16),
                   l1_exp_w_up.astype(BF16), l1_exp_w_down.astype(BF16))
    return out.reshape(BATCH, SEQ, D_MODEL)
```

```python
import functools
import math

import jax
import jax.numpy as jnp
import numpy as np
from jax import lax
from jax.experimental import pallas as pl
from jax.experimental.pallas import tpu as pltpu

D_MODEL = 1024
BATCH = 8
SEQ = 2048
CTX_LEN = 256
GRID_W = 64
MLA_HEADS = 8
MLA_NOPE = 64
MLA_ROPE = 32
MLA_V = 64
MLA_QK = MLA_NOPE + MLA_ROPE
MLA_Q_RANK = 256
MLA_KV_RANK = 128
GQA_HEADS = 8
GQA_KV_HEADS = 2
GQA_HEAD_DIM = 64
DENSE_FF = 2816
N_EXPERTS = 8
EXPERT_FF = 2816
ROPE_THETA = 10000.0
NORM_EPS = 1e-6
LOG2E = math.log2(math.e)
MLA_SCALE = MLA_QK ** -0.5
GQA_SCALE = GQA_HEAD_DIM ** -0.5

LANES = 128
N_LAT = BATCH * SEQ
N_CTX = BATCH * CTX_LEN
N_TOK = N_LAT + N_CTX
TM = 512
LAT_TILES = N_LAT // TM
TOK_TILES = N_TOK // TM
SEQ_TILES = SEQ // TM
TQ = 2048
Q_CHUNK = 256
V_WIDTH = 2 * LANES
IN_PAD = 1920
FF_CHUNK = 1408
ROW_SUB = D_MODEL // LANES
TMX = 512
SORT_TILES = 2 * N_LAT // TMX + N_EXPERTS
SORT_ROWS = SORT_TILES * TMX
VMEM_LIMIT = 56 * 1024 * 1024

F32 = jnp.float32
BF16 = jnp.bfloat16


def _silu(x):
    return x / (1.0 + jnp.exp(-x))


def _params(*sem):
    return pltpu.CompilerParams(dimension_semantics=sem, vmem_limit_bytes=VMEM_LIMIT)


def _mod_kernel(c_ref, w_ref, b_ref, o_ref):
    s = _silu(c_ref[...])
    o_ref[...] = jnp.dot(s, w_ref[...], precision=lax.Precision.HIGHEST,
                         preferred_element_type=F32) + b_ref[...]


def _modulation(cc, w_mod, b_mod):
    n = w_mod.shape[1]
    bn = 1024
    out = pl.pallas_call(
        _mod_kernel,
        grid=(n // bn,),
        in_specs=[pl.BlockSpec((16, D_MODEL), lambda j: (0, 0)),
                  pl.BlockSpec((D_MODEL, bn), lambda j: (0, j)),
                  pl.BlockSpec((1, bn), lambda j: (0, j))],
        out_specs=pl.BlockSpec((16, bn), lambda j: (0, j)),
        out_shape=jax.ShapeDtypeStruct((16, n), F32),
        compiler_params=_params("parallel"),
        name="modulation",
    )(cc, w_mod, b_mod.reshape(1, n))
    return out.reshape(16, 6, D_MODEL)


def _tile_rows(lat_ref, ctx_ref):
    return jnp.where(pl.program_id(0) < LAT_TILES, lat_ref[...], ctx_ref[...])


def _proj_kernel(xl_ref, xc_ref, mod_ref, gn_ref, win_ref, glat_ref, wuq_ref, wkv_ref, g128_ref,
                 tab_ref, qa_ref, ka_ref, va_ref, qb_ref, kb_ref, vb_ref):
    x = _tile_rows(xl_ref, xc_ref)
    r = lax.rsqrt(jnp.mean(x * x, axis=-1, keepdims=True) + NORM_EPS)
    h = (x * r * gn_ref[...]) * (1.0 + mod_ref[0, 1:2, :]) + mod_ref[0, 0:1, :]
    p = jnp.dot(h.astype(BF16), win_ref[...], preferred_element_type=F32)

    lane = lax.broadcasted_iota(jnp.int32, (TM, LANES), 1)
    lo = lane < 64
    ones_col = jnp.where(lane == 0, 1.0, 0.0).astype(BF16)

    def tables(base, row):
        return tab_ref[base] * g128_ref[row:row + 1, :], tab_ref[base + 1] * g128_ref[row + 1:row + 2, :]

    cq = p[:, 0:256]
    rq = lax.rsqrt(jnp.mean(cq * cq, axis=-1, keepdims=True) + NORM_EPS)
    qa = jnp.dot((cq * rq * glat_ref[0:1, :]).astype(BF16), wuq_ref[...],
                 preferred_element_type=F32)
    ckv = p[:, 256:384]
    rkv = lax.rsqrt(jnp.mean(ckv * ckv, axis=-1, keepdims=True) + NORM_EPS)
    slab = jnp.concatenate([ckv * rkv * glat_ref[1:2, 0:128], p[:, 384:512]], axis=-1)
    kv = jnp.dot(slab.astype(BF16), wkv_ref[...], preferred_element_type=F32)

    def head_a(src, hd, cos_g, sin_g):
        blk = src[:, hd * LANES:(hd + 1) * LANES]
        partner = src[:, 1024 + hd * LANES:1024 + (hd + 1) * LANES]
        rr = lax.rsqrt(jnp.sum(blk * blk, axis=-1, keepdims=True) * (1.0 / MLA_QK) + NORM_EPS)
        return ((blk * cos_g + partner * sin_g) * rr).astype(BF16)

    cq_g, sq_g = tables(0, 0)
    ck_g, sk_g = tables(0, 2)
    for hd in range(MLA_HEADS):
        qa_ref[hd] = head_a(qa, hd, cq_g, sq_g)
        ka_ref[hd] = head_a(kv, hd, ck_g, sk_g)
    for pr in range(MLA_HEADS // 2):
        blk = kv[:, 2048 + pr * LANES:2048 + (pr + 1) * LANES]
        va_ref[2 * pr, :, 0:LANES] = jnp.where(lo, blk, 0.0).astype(BF16)
        va_ref[2 * pr + 1, :, 0:LANES] = jnp.where(lo, 0.0, blk).astype(BF16)
    for hd in range(MLA_HEADS):
        va_ref[hd, :, LANES:V_WIDTH] = ones_col

    def pair_b(blk, partner, cos_g, sin_g):
        sq = blk * blk
        s_lo = jnp.sum(jnp.where(lo, sq, 0.0), axis=-1, keepdims=True)
        s_hi = jnp.sum(jnp.where(lo, 0.0, sq), axis=-1, keepdims=True)
        rr = jnp.where(lo, lax.rsqrt(s_lo * (1.0 / GQA_HEAD_DIM) + NORM_EPS),
                       lax.rsqrt(s_hi * (1.0 / GQA_HEAD_DIM) + NORM_EPS))
        return (blk * cos_g + partner * sin_g) * rr

    cq_g, sq_g = tables(2, 4)
    ck_g, sk_g = tables(2, 6)
    for pr in range(GQA_HEADS // 2):
        blk = p[:, 512 + pr * LANES:512 + (pr + 1) * LANES]
        partner = p[:, 1280 + pr * LANES:1280 + (pr + 1) * LANES]
        qb_ref[pr] = pair_b(blk, partner, cq_g, sq_g).astype(BF16)
    kb = pair_b(p[:, 1024:1152], p[:, 1792:1920], ck_g, sk_g)
    kb_sw = pltpu.roll(kb, 64, 1)
    kb_ref[0] = jnp.where(lo, kb, 0.0).astype(BF16)
    kb_ref[1] = jnp.where(lo, 0.0, kb_sw).astype(BF16)
    kb_ref[2] = jnp.where(lo, kb_sw, 0.0).astype(BF16)
    kb_ref[3] = jnp.where(lo, 0.0, kb).astype(BF16)
    vb = p[:, 1152:1280]
    vb_sw = pltpu.roll(vb, 64, 1)
    vb_ref[0, :, 0:LANES] = jnp.where(lo, vb, 0.0).astype(BF16)
    vb_ref[1, :, 0:LANES] = jnp.where(lo, 0.0, vb_sw).astype(BF16)
    vb_ref[2, :, 0:LANES] = jnp.where(lo, vb_sw, 0.0).astype(BF16)
    vb_ref[3, :, 0:LANES] = jnp.where(lo, 0.0, vb).astype(BF16)
    for j in range(2 * GQA_KV_HEADS):
        vb_ref[j, :, LANES:V_WIDTH] = ones_col


def _mod_index(i):
    return jnp.where(i < LAT_TILES, i // SEQ_TILES, BATCH)


def _lat_ctx_specs(width, ctx_tile0):
    return [pl.BlockSpec((TM, width), lambda i: (jnp.minimum(i, LAT_TILES - 1), 0)),
            pl.BlockSpec((TM, width), lambda i: (ctx_tile0 + jnp.maximum(i - LAT_TILES, 0), 0))]


def _projection(x_lat, x_ctx, ctx_tile0, mod3, gn, w_in, glat, w_uq, w_kv, g128, tab):
    const2 = lambda i: (0, 0)
    head_out = lambda n, w: pl.BlockSpec((n, TM, w), lambda i: (0, i, 0))
    head_shape = lambda n, w: jax.ShapeDtypeStruct((n, N_TOK, w), BF16)
    return pl.pallas_call(
        _proj_kernel,
        grid=(TOK_TILES,),
        in_specs=_lat_ctx_specs(D_MODEL, ctx_tile0) + [
                  pl.BlockSpec((1, 6, D_MODEL), lambda i: (_mod_index(i), 0, 0)),
                  pl.BlockSpec((1, D_MODEL), const2),
                  pl.BlockSpec((D_MODEL, IN_PAD), const2),
                  pl.BlockSpec((2, 256), const2),
                  pl.BlockSpec((MLA_Q_RANK, 2 * MLA_HEADS * LANES), const2),
                  pl.BlockSpec((256, 2 * MLA_HEADS * LANES + MLA_HEADS * MLA_V), const2),
                  pl.BlockSpec((8, LANES), const2),
                  pl.BlockSpec((4, TM, LANES),
                               lambda i: (0, jnp.where(i < LAT_TILES, i % SEQ_TILES, SEQ_TILES), 0))],
        out_specs=[head_out(8, LANES), head_out(8, LANES), head_out(8, V_WIDTH),
                   head_out(4, LANES), head_out(4, LANES), head_out(4, V_WIDTH)],
        out_shape=[head_shape(8, LANES), head_shape(8, LANES), head_shape(8, V_WIDTH),
                   head_shape(4, LANES), head_shape(4, LANES), head_shape(4, V_WIDTH)],
        compiler_params=_params("parallel"),
        name="projection",
    )(x_lat, x_ctx, mod3, gn, w_in, glat, w_uq, w_kv, g128, tab)


def _attn_kernel(*refs, with_lat):
    if with_lat:
        q0_ref, q1_ref, kl_ref, kc_ref, vl_ref, vc_ref, o_ref = refs
    else:
        q0_ref, q1_ref, kc_ref, vc_ref, o_ref = refs
    nt = (((1,), (1,)), ((), ()))
    rows = q0_ref.shape[0]
    for r0 in range(0, rows, Q_CHUNK):
        out = None
        for u, q_ref in enumerate((q0_ref, q1_ref)):
            q = q_ref[r0:r0 + Q_CHUNK, :]
            s_c = lax.dot_general(q, kc_ref[u], nt, preferred_element_type=F32)
            m = jnp.max(s_c, axis=-1, keepdims=True)
            if with_lat:
                s_l = lax.dot_general(q, kl_ref[u], nt, preferred_element_type=F32)
                m = jnp.maximum(m, jnp.max(s_l, axis=-1, keepdims=True))
            acc = jnp.dot(jnp.exp2((s_c - m).astype(BF16)), vc_ref[u], preferred_element_type=F32)
            if with_lat:
                acc = acc + jnp.dot(jnp.exp2((s_l - m).astype(BF16)), vl_ref[u],
                                    preferred_element_type=F32)
            o = acc[:, 0:LANES] / acc[:, LANES:LANES + 1]
            out = o if out is None else out + o
        o_ref[r0:r0 + Q_CHUNK, :] = out.astype(o_ref.dtype)


def _attention(q, k, v, q_heads, kv_pair):
    n_units = 4
    qt = SEQ // TQ
    ctx_blk0 = N_LAT // CTX_LEN

    def qspec(which):
        return pl.BlockSpec((None, TQ, LANES), lambda b, p, i: (q_heads(p)[which], b * qt + i, 0))

    def kvspec(rows, blk0, width):
        return pl.BlockSpec((2, rows, width), lambda b, p, i: (kv_pair(p), blk0 + b, 0))

    return pl.pallas_call(
        functools.partial(_attn_kernel, with_lat=True),
        grid=(BATCH, n_units, qt),
        in_specs=[qspec(0), qspec(1), kvspec(SEQ, 0, LANES), kvspec(CTX_LEN, ctx_blk0, LANES),
                  kvspec(SEQ, 0, V_WIDTH), kvspec(CTX_LEN, ctx_blk0, V_WIDTH)],
        out_specs=pl.BlockSpec((TQ, LANES), lambda b, p, i: (b * qt + i, p)),
        out_shape=jax.ShapeDtypeStruct((N_LAT, n_units * LANES), BF16),
        compiler_params=_params("parallel", "parallel", "parallel"),
        name="attention_latent",
    )(q, q, k, k, v, v)


def _attention_ctx(q, k, v, q_heads, kv_pair):
    n_units = 4
    ctx_blk0 = N_LAT // CTX_LEN

    def qspec(which):
        return pl.BlockSpec((None, CTX_LEN, LANES), lambda b, p: (q_heads(p)[which], ctx_blk0 + b, 0))

    def kvspec(width):
        return pl.BlockSpec((2, CTX_LEN, width), lambda b, p: (kv_pair(p), ctx_blk0 + b, 0))

    return pl.pallas_call(
        functools.partial(_attn_kernel, with_lat=False),
        grid=(BATCH, n_units),
        in_specs=[qspec(0), qspec(1), kvspec(LANES), kvspec(V_WIDTH)],
        out_specs=pl.BlockSpec((CTX_LEN, LANES), lambda b, p: (b, p)),
        out_shape=jax.ShapeDtypeStruct((N_CTX, n_units * LANES), BF16),
        compiler_params=_params("parallel", "parallel"),
        name="attention_context",
    )(q, q, k, v)


_A_HEADS = lambda p: (2 * p, 2 * p + 1)
_A_KV = lambda p: p
_B_HEADS = lambda p: (p, p)
_B_KV = lambda p: p // 2


def _post_body(x, oa, ob, mod_ref, gn_ref, wo_ref):
    y = (jnp.dot(oa, wo_ref[0:512, :], preferred_element_type=F32)
         + jnp.dot(ob, wo_ref[512:1024, :], preferred_element_type=F32))
    x1 = x + mod_ref[0, 2:3, :] * y
    r = lax.rsqrt(jnp.mean(x1 * x1, axis=-1, keepdims=True) + NORM_EPS)
    h2 = (x1 * r * gn_ref[...]) * (1.0 + mod_ref[0, 4:5, :]) + mod_ref[0, 3:4, :]
    return x1, h2


def _post_kernel(xl_ref, xc_ref, oa_ref, oac_ref, ob_ref, obc_ref, mod_ref, gn_ref, wo_ref,
                 x1_ref, h2_ref):
    x1, h2 = _post_body(_tile_rows(xl_ref, xc_ref), _tile_rows(oa_ref, oac_ref),
                        _tile_rows(ob_ref, obc_ref), mod_ref, gn_ref, wo_ref)
    x1_ref[...] = x1
    h2_ref[...] = h2.astype(BF16)


def _post_router_kernel(x_ref, oa_ref, ob_ref, mod_ref, gn_ref, wo_ref, wr_ref,
                        x1_ref, hrow_ref, route_ref, count_ref, carry_ref):
    x1, h2 = _post_body(x_ref[...], oa_ref[...], ob_ref[...], mod_ref, gn_ref, wo_ref)
    x1_ref[...] = x1
    for s in range(ROW_SUB):
        hrow_ref[pl.ds(s, TM, stride=ROW_SUB), :] = h2[:, s * LANES:(s + 1) * LANES]

    h_hi = h2.astype(BF16)
    h_lo = (h2 - h_hi.astype(F32)).astype(BF16)
    part = jnp.dot(h_hi, wr_ref[...], preferred_element_type=F32)
    logits = (part[:, 0:LANES] + part[:, LANES:2 * LANES]
              + jnp.dot(h_lo, wr_ref[:, 0:LANES], preferred_element_type=F32))
    lane = lax.broadcasted_iota(jnp.int32, logits.shape, 1)
    valid = lane < N_EXPERTS
    lg = jnp.where(valid, logits, -jnp.inf)
    e = jnp.exp(lg - jnp.max(lg, axis=-1, keepdims=True))
    probs = e / jnp.sum(e, axis=-1, keepdims=True)
    v1 = jnp.max(probs, axis=-1, keepdims=True)
    i1 = jnp.min(jnp.where(probs == v1, lane, LANES), axis=-1, keepdims=True)
    rest = jnp.where(valid & (lane != i1), probs, -1.0)
    v2 = jnp.max(rest, axis=-1, keepdims=True)
    i2 = jnp.min(jnp.where(rest == v2, lane, LANES), axis=-1, keepdims=True)

    @pl.when(pl.program_id(0) == 0)
    def _():
        carry_ref[...] = jnp.zeros_like(carry_ref)

    pick1 = lane == i1
    pick2 = lane == i2
    onehot = jnp.where(pick1, 1.0, 0.0) + jnp.where(pick2, 1.0, 0.0)
    row = lax.broadcasted_iota(jnp.int32, (TM, TM), 0)
    col = lax.broadcasted_iota(jnp.int32, (TM, TM), 1)
    before = jnp.where(col < row, 1.0, 0.0).astype(BF16)
    seen = jnp.dot(before, onehot.astype(BF16), preferred_element_type=F32) + carry_ref[0:1, :]
    rank1 = jnp.sum(jnp.where(pick1, seen, 0.0), axis=-1, keepdims=True)
    rank2 = jnp.sum(jnp.where(pick2, seen, 0.0), axis=-1, keepdims=True)
    carry_ref[...] = carry_ref[...] + jnp.sum(onehot, axis=0, keepdims=True)
    count_ref[...] = carry_ref[...]

    fields = (v1 / (v1 + v2), v2 / (v1 + v2), i1.astype(F32), i2.astype(F32), rank1, rank2)
    route = jnp.zeros((TM, LANES), F32)
    for k, val in enumerate(fields):
        route = jnp.where(lane == k, val, route)
    route_ref[...] = route


def _post_attention(x_lat, x_ctx, o_lat, o_ctx, mod3, gn, w_out, w_router):
    const2 = lambda i: (0, 0)
    n_tiles = LAT_TILES if o_ctx is None else TOK_TILES
    rows = n_tiles * TM
    tail_specs = [pl.BlockSpec((1, 6, D_MODEL), lambda i: (_mod_index(i), 0, 0)),
                  pl.BlockSpec((1, D_MODEL), const2),
                  pl.BlockSpec((D_MODEL, D_MODEL), const2)]
    x1_spec = pl.BlockSpec((TM, D_MODEL), lambda i: (i, 0))
    x1_shape = jax.ShapeDtypeStruct((rows, D_MODEL), F32)
    if o_ctx is not None:
        return pl.pallas_call(
            _post_kernel, grid=(n_tiles,),
            in_specs=(_lat_ctx_specs(D_MODEL, 0) + _lat_ctx_specs(512, 0) + _lat_ctx_specs(512, 0)
                      + tail_specs),
            out_specs=[x1_spec, x1_spec],
            out_shape=[x1_shape, jax.ShapeDtypeStruct((rows, D_MODEL), BF16)],
            compiler_params=_params("parallel"), name="post_attention",
        )(x_lat, x_ctx, o_lat[0], o_ctx[0], o_lat[1], o_ctx[1], mod3, gn, w_out)
    half_spec = pl.BlockSpec((TM, 512), lambda i: (i, 0))
    in_specs = [x1_spec, half_spec, half_spec] + tail_specs
    args = [x_lat, *o_lat, mod3, gn, w_out]
    return pl.pallas_call(
        _post_router_kernel, grid=(n_tiles,),
        in_specs=in_specs + [pl.BlockSpec((D_MODEL, 2 * LANES), const2)],
        out_specs=[x1_spec,
                   pl.BlockSpec((TM * ROW_SUB, LANES), lambda i: (i, 0)),
                   pl.BlockSpec((TM, LANES), lambda i: (i, 0)),
                   pl.BlockSpec((8, LANES), const2)],
        out_shape=[x1_shape,
                   jax.ShapeDtypeStruct((rows * ROW_SUB, LANES), F32),
                   jax.ShapeDtypeStruct((rows, LANES), F32),
                   jax.ShapeDtypeStruct((8, LANES), F32)],
        scratch_shapes=[pltpu.VMEM((8, LANES), F32)],
        compiler_params=_params("arbitrary"), name="post_attention_router",
    )(*args, w_router)


def _swiglu_chunk(h, wg, wu, wd):
    g = jnp.dot(h, wg, preferred_element_type=F32)
    u = jnp.dot(h, wu, preferred_element_type=F32)
    return jnp.dot((_silu(g) * u).astype(BF16), wd, preferred_element_type=F32)


def _ffn_kernel(h_ref, x1_ref, mod_ref, wg_ref, wu_ref, wd_ref, o_ref):
    h = h_ref[...]
    y = None
    for c in range(DENSE_FF // FF_CHUNK):
        sl = slice(c * FF_CHUNK, (c + 1) * FF_CHUNK)
        yc = _swiglu_chunk(h, wg_ref[:, sl], wu_ref[:, sl], wd_ref[sl, :])
        y = yc if y is None else y + yc
    o_ref[...] = x1_ref[...] + mod_ref[0, 5:6, :] * y


def _dense_ffn(h2, x1, mod3, wg, wu, wd):
    resident = lambda shape: pl.BlockSpec(shape, lambda i: (0, 0), pipeline_mode=pl.Buffered(1))
    return pl.pallas_call(
        _ffn_kernel,
        grid=(TOK_TILES,),
        in_specs=[pl.BlockSpec((TM, D_MODEL), lambda i: (i, 0)),
                  pl.BlockSpec((TM, D_MODEL), lambda i: (i, 0)),
                  pl.BlockSpec((1, 6, D_MODEL), lambda i: (_mod_index(i), 0, 0)),
                  resident((D_MODEL, DENSE_FF)), resident((D_MODEL, DENSE_FF)),
                  resident((DENSE_FF, D_MODEL))],
        out_specs=pl.BlockSpec((TM, D_MODEL), lambda i: (i, 0)),
        out_shape=jax.ShapeDtypeStruct((N_TOK, D_MODEL), F32),
        compiler_params=_params("parallel"),
        name="dense_ffn",
    )(h2, x1, mod3, wg, wu, wd)


def _row(ref, r):
    return ref.at[pl.ds(pl.multiple_of(r * ROW_SUB, ROW_SUB), ROW_SUB)]


def _rows_to_matrix(ref, n):
    return jnp.concatenate([ref[pl.ds(s, n, stride=ROW_SUB), :] for s in range(ROW_SUB)], axis=-1)


def _dispatch_kernel(dest_ref, h_ref, xs_init_ref, xs_ref, sem):
    del xs_init_ref

    def issue(j, carry):
        src = _row(h_ref, j)
        for k in range(2):
            pltpu.make_async_copy(src, _row(xs_ref, dest_ref[k, j]), sem).start()
        return carry

    lax.fori_loop(0, TM, issue, 0, unroll=8)
    for k in range(2):
        pltpu.make_async_copy(h_ref, xs_ref.at[pl.ds(0, TM * ROW_SUB)], sem).wait()


def _dispatch(dest3, hrow):
    xs_init = jnp.zeros((SORT_ROWS * ROW_SUB, LANES), F32)
    return pl.pallas_call(
        _dispatch_kernel,
        grid=(LAT_TILES,),
        in_specs=[pl.BlockSpec((None, 2, TM), lambda i: (i, 0, 0), memory_space=pltpu.SMEM),
                  pl.BlockSpec((TM * ROW_SUB, LANES), lambda i: (i, 0)),
                  pl.BlockSpec(memory_space=pl.ANY)],
        out_specs=pl.BlockSpec(memory_space=pl.ANY),
        out_shape=jax.ShapeDtypeStruct((SORT_ROWS * ROW_SUB, LANES), F32),
        scratch_shapes=[pltpu.SemaphoreType.DMA(())],
        input_output_aliases={2: 0},
        compiler_params=_params("arbitrary"),
        name="moe_dispatch",
    )(dest3, hrow, xs_init)


def _expert_kernel(te_ref, nu_ref, xs_ref, wg_ref, wu_ref, wd_ref, y_ref):
    del te_ref
    used = pl.program_id(0) < nu_ref[0]

    @pl.when(jnp.logical_not(used))
    def _():
        y_ref[...] = jnp.zeros_like(y_ref)

    @pl.when(used)
    def _():
        x = _rows_to_matrix(xs_ref, TMX).astype(BF16)
        y = None
        for c in range(EXPERT_FF // FF_CHUNK):
            sl = slice(c * FF_CHUNK, (c + 1) * FF_CHUNK)
            yc = _swiglu_chunk(x, wg_ref[0, :, sl], wu_ref[0, :, sl], wd_ref[0, sl, :])
            y = yc if y is None else y + yc
        for s in range(ROW_SUB):
            y_ref[pl.ds(s, TMX, stride=ROW_SUB), :] = y[:, s * LANES:(s + 1) * LANES]


def _experts(tile_expert, n_used, xs, wg, wu, wd):
    one = pl.Buffered(1)
    grid_spec = pltpu.PrefetchScalarGridSpec(
        num_scalar_prefetch=2,
        grid=(SORT_TILES,),
        in_specs=[pl.BlockSpec((TMX * ROW_SUB, LANES),
                               lambda t, te, nu: (jnp.minimum(t, nu[0] - 1), 0)),
                  pl.BlockSpec((1, D_MODEL, EXPERT_FF), lambda t, te, nu: (te[t], 0, 0),
                               pipeline_mode=one),
                  pl.BlockSpec((1, D_MODEL, EXPERT_FF), lambda t, te, nu: (te[t], 0, 0),
                               pipeline_mode=one),
                  pl.BlockSpec((1, EXPERT_FF, D_MODEL), lambda t, te, nu: (te[t], 0, 0),
                               pipeline_mode=one)],
        out_specs=pl.BlockSpec((TMX * ROW_SUB, LANES), lambda t, te, nu: (t, 0)))
    return pl.pallas_call(
        _expert_kernel,
        grid_spec=grid_spec,
        out_shape=jax.ShapeDtypeStruct((SORT_ROWS * ROW_SUB, LANES), F32),
        compiler_params=_params("arbitrary"),
        name="moe_experts",
    )(tile_expert, n_used, xs, wg, wu, wd)


def _combine_kernel(dest_ref, y_ref, x1_ref, route_ref, mod_ref, o_ref, buf_ref, sem):
    def issue(j, carry):
        for k in range(2):
            pltpu.make_async_copy(_row(y_ref, dest_ref[k, j]), _row(buf_ref.at[k], j), sem).start()
        return carry

    lax.fori_loop(0, TM, issue, 0, unroll=8)
    for k in range(2):
        pltpu.make_async_copy(y_ref.at[pl.ds(0, TM * ROW_SUB)], buf_ref.at[k], sem).wait()
    route = route_ref[...]
    y = (route[:, 0:1] * _rows_to_matrix(buf_ref.at[0], TM)
         + route[:, 1:2] * _rows_to_matrix(buf_ref.at[1], TM))
    o_ref[...] = x1_ref[...] + mod_ref[0, 5:6, :] * y


def _combine(dest3, ys, x1, route, mod3):
    return pl.pallas_call(
        _combine_kernel,
        grid=(LAT_TILES,),
        in_specs=[pl.BlockSpec((None, 2, TM), lambda i: (i, 0, 0), memory_space=pltpu.SMEM),
                  pl.BlockSpec(memory_space=pl.ANY),
                  pl.BlockSpec((TM, D_MODEL), lambda i: (i, 0)),
                  pl.BlockSpec((TM, LANES), lambda i: (i, 0)),
                  pl.BlockSpec((1, 6, D_MODEL), lambda i: (i // SEQ_TILES, 0, 0))],
        out_specs=pl.BlockSpec((TM, D_MODEL), lambda i: (i, 0)),
        out_shape=jax.ShapeDtypeStruct((N_LAT, D_MODEL), F32),
        scratch_shapes=[pltpu.VMEM((2, TM * ROW_SUB, LANES), F32), pltpu.SemaphoreType.DMA(())],
        compiler_params=_params("arbitrary"),
        name="moe_combine",
    )(dest3, ys, x1, route, mod3)


def _moe_ffn(hrow, x1, route, counts, mod3, wg, wu, wd):
    expert = route[:, 2:4].astype(jnp.int32)
    rank = route[:, 4:6].astype(jnp.int32)
    tiles = (counts[0, :N_EXPERTS].astype(jnp.int32) + TMX - 1) // TMX
    tile_end = jnp.cumsum(tiles)
    dest = (tile_end - tiles)[expert] * TMX + rank
    dest3 = dest.reshape(LAT_TILES, TM, 2).transpose(0, 2, 1)
    n_used = tile_end[N_EXPERTS - 1:]
    tile_ids = jnp.minimum(jnp.arange(SORT_TILES), n_used[0] - 1)
    tile_expert = jnp.sum(tile_ids[:, None] >= tile_end[None, :], axis=1).astype(jnp.int32)
    xs = _dispatch(dest3, hrow)
    ys = _experts(tile_expert, n_used.astype(jnp.int32), xs, wg, wu, wd)
    return _combine(dest3, ys, x1, route, mod3)


def _partner(half):
    idx = np.arange(4 * half)
    return np.where((idx // half) % 2 == 0, idx + half, idx - half)


_PARTNER_A = np.concatenate([np.arange(MLA_NOPE), MLA_NOPE + _partner(MLA_ROPE // 4),
                             np.arange(MLA_QK, LANES)])
_PARTNER_B = np.concatenate([_partner(GQA_HEAD_DIM // 4), GQA_HEAD_DIM + _partner(GQA_HEAD_DIM // 4)])


def _rope_tables():
    t = jnp.arange(SEQ)
    row = (t // GRID_W).astype(F32)
    col = (t % GRID_W).astype(F32)

    def one_axis(pos, half):
        freqs = ROPE_THETA ** (-jnp.arange(half, dtype=F32) / half)
        ang = pos[:, None] * freqs[None, :]
        cos, sin = jnp.cos(ang), jnp.sin(ang)
        return jnp.concatenate([cos, cos], -1), jnp.concatenate([-sin, sin], -1)

    def two_axes(half):
        r, c = one_axis(row, half), one_axis(col, half)
        return [jnp.concatenate([a, b], -1) for a, b in zip(r, c)]

    def pad_a(tbl, fill):
        return jnp.concatenate([jnp.full((SEQ, MLA_NOPE), fill, F32), tbl,
                                jnp.full((SEQ, LANES - MLA_QK), fill, F32)], -1)

    ta = two_axes(MLA_ROPE // 4)
    tb = [jnp.concatenate([x, x], -1) for x in two_axes(GQA_HEAD_DIM // 4)]
    tabs = [pad_a(ta[0], 1.0), pad_a(ta[1], 0.0)] + tb
    ident = [jnp.ones((TM, LANES), F32), jnp.zeros((TM, LANES), F32)]
    return jnp.stack([jnp.concatenate([tbl, idn], 0) for tbl, idn in zip(tabs, ident + ident)])


def _layer_weights(w_in, q_lat_norm, w_uq, kv_lat_norm, w_ukv, mla_q_gain, mla_k_gain,
                   gqa_q_gain, gqa_k_gain):
    split = MLA_Q_RANK + MLA_KV_RANK + MLA_ROPE
    perm_a = (np.arange(MLA_HEADS)[:, None] * LANES + _PARTNER_A[None, :]).reshape(-1)
    perm_b = (np.arange(GQA_HEADS // 2)[:, None] * LANES + _PARTNER_B[None, :]).reshape(-1)
    w_qb = w_in[:, split:split + GQA_HEADS * GQA_HEAD_DIM]
    w_kb = w_in[:, split + 512:split + 640]
    w_in_p = jnp.concatenate([w_in[:, :split], jnp.zeros((D_MODEL, 512 - split), F32),
                              w_in[:, split:], w_qb[:, perm_b], w_kb[:, _PARTNER_B]],
                             axis=1).astype(BF16)
    w_uq_p = jnp.pad(w_uq.reshape(MLA_Q_RANK, MLA_HEADS, MLA_QK),
                     ((0, 0), (0, 0), (0, LANES - MLA_QK))).reshape(MLA_Q_RANK, MLA_HEADS * LANES)
    w_uq_p = jnp.concatenate([w_uq_p, w_uq_p[:, perm_a]], axis=1)
    ukv = w_ukv.reshape(MLA_KV_RANK, MLA_HEADS, MLA_NOPE + MLA_V)
    w_k = jnp.pad(ukv[:, :, :MLA_NOPE], ((0, 0), (0, 0), (0, LANES - MLA_NOPE)))
    place = jnp.pad(jnp.eye(MLA_ROPE, dtype=F32), ((0, 0), (MLA_NOPE, LANES - MLA_QK)))
    place = jnp.broadcast_to(place[:, None, :], (MLA_ROPE, MLA_HEADS, LANES))
    w_k = jnp.concatenate([w_k, place,
                           jnp.zeros((256 - MLA_KV_RANK - MLA_ROPE, MLA_HEADS, LANES), F32)], 0)
    w_k = w_k.reshape(256, MLA_HEADS * LANES)
    w_v = jnp.pad(ukv[:, :, MLA_NOPE:].reshape(MLA_KV_RANK, MLA_HEADS * MLA_V),
                  ((0, 256 - MLA_KV_RANK), (0, 0)))
    w_kv = jnp.concatenate([w_k, w_k[:, perm_a], w_v], axis=1)
    glat = jnp.stack([q_lat_norm, jnp.pad(kv_lat_norm, (0, 256 - MLA_KV_RANK))])
    pad_qk = lambda g: jnp.pad(g, (0, LANES - MLA_QK))
    gains = [pad_qk(mla_q_gain) * (MLA_SCALE * LOG2E), pad_qk(mla_k_gain),
             jnp.tile(gqa_q_gain, 2) * (GQA_SCALE * LOG2E), jnp.tile(gqa_k_gain, 2)]
    partners = [_PARTNER_A, _PARTNER_A, _PARTNER_B, _PARTNER_B]
    g128 = jnp.stack([v for g, pm in zip(gains, partners) for v in (g, g[pm])])
    return w_in_p, glat, w_uq_p.astype(BF16), w_kv.astype(BF16), g128


def _mixer(x_lat, x_ctx, ctx_tile0, mod3, tab, norm_attn, w_in, q_lat_norm, w_uq, kv_lat_norm,
           w_ukv, mla_q_gain, mla_k_gain, gqa_q_gain, gqa_k_gain, with_ctx_queries):
    w_in_p, glat, w_uq_p, w_kv, g128 = _layer_weights(
        w_in, q_lat_norm, w_uq, kv_lat_norm, w_ukv, mla_q_gain, mla_k_gain, gqa_q_gain, gqa_k_gain)
    qa, ka, va, qb, kb, vb = _projection(x_lat, x_ctx, ctx_tile0, mod3,
                                         norm_attn.reshape(1, D_MODEL), w_in_p, glat,
                                         w_uq_p, w_kv, g128, tab)
    o_lat = (_attention(qa, ka, va, _A_HEADS, _A_KV), _attention(qb, kb, vb, _B_HEADS, _B_KV))
    if not with_ctx_queries:
        return o_lat, None
    return o_lat, (_attention_ctx(qa, ka, va, _A_HEADS, _A_KV),
                   _attention_ctx(qb, kb, vb, _B_HEADS, _B_KV))


def kernel(x, c, ctx, c_ctx, l0_w_mod, l0_b_mod, l0_norm_attn, l0_w_in, l0_q_lat_norm, l0_w_uq, l0_kv_lat_norm, l0_w_ukv, l0_mla_q_gain, l0_mla_k_gain, l0_gqa_q_gain, l0_gqa_k_gain, l0_w_out, l0_norm_ffn, l0_ffn_w_gate, l0_ffn_w_up, l0_ffn_w_down, l1_w_mod, l1_b_mod, l1_norm_attn, l1_w_in, l1_q_lat_norm, l1_w_uq, l1_kv_lat_norm, l1_w_ukv, l1_mla_q_gain, l1_mla_k_gain, l1_gqa_q_gain, l1_gqa_k_gain, l1_w_out, l1_norm_ffn, l1_router, l1_exp_w_gate, l1_exp_w_up, l1_exp_w_down):
    x_lat = x.reshape(N_LAT, D_MODEL)
    x_ctx = ctx.reshape(N_CTX, D_MODEL)
    cc = jnp.concatenate([c, c_ctx[None, :], jnp.zeros((16 - BATCH - 1, D_MODEL), F32)], axis=0)
    tab = _rope_tables()

    mod3 = _modulation(cc, l0_w_mod, l0_b_mod)
    o_lat, o_ctx = _mixer(x_lat, x_ctx, 0, mod3, tab, l0_norm_attn, l0_w_in, l0_q_lat_norm,
                          l0_w_uq, l0_kv_lat_norm, l0_w_ukv, l0_mla_q_gain, l0_mla_k_gain,
                          l0_gqa_q_gain, l0_gqa_k_gain, with_ctx_queries=True)
    x1, h2 = _post_attention(x_lat, x_ctx, o_lat, o_ctx, mod3, l0_norm_ffn.reshape(1, D_MODEL),
                             l0_w_out.astype(BF16), None)
    xall = _dense_ffn(h2, x1, mod3, l0_ffn_w_gate.astype(BF16), l0_ffn_w_up.astype(BF16),
                      l0_ffn_w_down.astype(BF16))

    mod3 = _modulation(cc, l1_w_mod, l1_b_mod)
    o_lat, _ = _mixer(xall, xall, LAT_TILES, mod3, tab, l1_norm_attn, l1_w_in, l1_q_lat_norm,
                      l1_w_uq, l1_kv_lat_norm, l1_w_ukv, l1_mla_q_gain, l1_mla_k_gain,
                      l1_gqa_q_gain, l1_gqa_k_gain, with_ctx_queries=False)
    w_router = jnp.pad(l1_router, ((0, 0), (0, LANES - N_EXPERTS)))
    w_router_hi = w_router.astype(BF16)
    w_router = jnp.concatenate([w_router_hi, (w_router - w_router_hi.astype(F32)).astype(BF16)],
                               axis=1)
    x1, hrow, route, counts = _post_attention(xall, None, o_lat, None, mod3,
                                              l1_norm_ffn.reshape(1, D_MODEL),
                                              l1_w_out.astype(BF16), w_router)
    out = _moe_ffn(hrow, x1, route, counts, mod3, l1_exp_w_gate.astype(BF16),
                   l1_exp_w_up.astype(BF16), l1_exp_w_down.astype(BF16))
    return out.reshape(BATCH, SEQ, D_MODEL)
```

```python
import functools
import math

import jax
import jax.numpy as jnp
import numpy as np
from jax import lax
from jax.experimental import pallas as pl
from jax.experimental.pallas import tpu as pltpu

D_MODEL = 1024
BATCH = 8
SEQ = 2048
CTX_LEN = 256
GRID_W = 64
MLA_HEADS = 8
MLA_NOPE = 64
MLA_ROPE = 32
MLA_V = 64
MLA_QK = MLA_NOPE + MLA_ROPE
MLA_Q_RANK = 256
MLA_KV_RANK = 128
GQA_HEADS = 8
GQA_KV_HEADS = 2
GQA_HEAD_DIM = 64
DENSE_FF = 2816
N_EXPERTS = 8
EXPERT_FF = 2816
ROPE_THETA = 10000.0
NORM_EPS = 1e-6
LOG2E = math.log2(math.e)
MLA_SCALE = MLA_QK ** -0.5
GQA_SCALE = GQA_HEAD_DIM ** -0.5

LANES = 128
N_LAT = BATCH * SEQ
N_CTX = BATCH * CTX_LEN
N_TOK = N_LAT + N_CTX
TM = 512
LAT_TILES = N_LAT // TM
TOK_TILES = N_TOK // TM
SEQ_TILES = SEQ // TM
TQ = 2048
Q_CHUNK = 256
V_WIDTH = 2 * LANES
IN_PAD = 1920
FF_SPLITS = (0, 1536, 2816)
ROW_SUB = D_MODEL // LANES
TMX = 512
SORT_TILES = 2 * N_LAT // TMX + N_EXPERTS
SORT_ROWS = SORT_TILES * TMX
Y2_ROWS = 2 * N_LAT + 2 * TMX
VMEM_LIMIT = 56 * 1024 * 1024

F32 = jnp.float32
BF16 = jnp.bfloat16


def _silu(x):
    return x / (1.0 + jnp.exp(-x))


def _params(*sem):
    return pltpu.CompilerParams(dimension_semantics=sem, vmem_limit_bytes=VMEM_LIMIT)


def _mod_kernel(c_ref, w_ref, b_ref, o_ref):
    s = _silu(c_ref[...])
    o_ref[...] = jnp.dot(s, w_ref[...], precision=lax.Precision.HIGHEST,
                         preferred_element_type=F32) + b_ref[...]


def _modulation(cc, w_mod, b_mod):
    n = w_mod.shape[1]
    bn = 1024
    out = pl.pallas_call(
        _mod_kernel,
        grid=(n // bn,),
        in_specs=[pl.BlockSpec((16, D_MODEL), lambda j: (0, 0)),
                  pl.BlockSpec((D_MODEL, bn), lambda j: (0, j)),
                  pl.BlockSpec((1, bn), lambda j: (0, j))],
        out_specs=pl.BlockSpec((16, bn), lambda j: (0, j)),
        out_shape=jax.ShapeDtypeStruct((16, n), F32),
        compiler_params=_params("parallel"),
        name="modulation",
    )(cc, w_mod, b_mod.reshape(1, n))
    return out.reshape(16, 6, D_MODEL)


def _tile_rows(lat_ref, ctx_ref):
    return jnp.where(pl.program_id(0) < LAT_TILES, lat_ref[...], ctx_ref[...])


def _proj_kernel(xl_ref, xc_ref, mod_ref, gn_ref, win_ref, glat_ref, wuq_ref, wkv_ref, g128_ref,
                 tab_ref, qa_ref, ka_ref, va_ref, qb_ref, kb_ref, vb_ref):
    x = _tile_rows(xl_ref, xc_ref)
    r = lax.rsqrt(jnp.mean(x * x, axis=-1, keepdims=True) + NORM_EPS)
    h = (x * r * gn_ref[...]) * (1.0 + mod_ref[0, 1:2, :]) + mod_ref[0, 0:1, :]
    p = jnp.dot(h.astype(BF16), win_ref[...], preferred_element_type=F32)

    lane = lax.broadcasted_iota(jnp.int32, (TM, LANES), 1)
    lo = lane < 64
    ones_col = jnp.where(lane == 0, 1.0, 0.0).astype(BF16)

    def tables(base, row):
        return tab_ref[base] * g128_ref[row:row + 1, :], tab_ref[base + 1] * g128_ref[row + 1:row + 2, :]

    cq = p[:, 0:256]
    rq = lax.rsqrt(jnp.mean(cq * cq, axis=-1, keepdims=True) + NORM_EPS)
    qa = jnp.dot((cq * rq * glat_ref[0:1, :]).astype(BF16), wuq_ref[...],
                 preferred_element_type=F32)
    ckv = p[:, 256:384]
    rkv = lax.rsqrt(jnp.mean(ckv * ckv, axis=-1, keepdims=True) + NORM_EPS)
    slab = jnp.concatenate([ckv * rkv * glat_ref[1:2, 0:128], p[:, 384:512]], axis=-1)
    kv = jnp.dot(slab.astype(BF16), wkv_ref[...], preferred_element_type=F32)

    def head_a(src, hd, cos_g, sin_g):
        blk = src[:, hd * LANES:(hd + 1) * LANES]
        partner = src[:, 1024 + hd * LANES:1024 + (hd + 1) * LANES]
        rr = lax.rsqrt(jnp.sum(blk * blk, axis=-1, keepdims=True) * (1.0 / MLA_QK) + NORM_EPS)
        return ((blk * cos_g + partner * sin_g) * rr).astype(BF16)

    cq_g, sq_g = tables(0, 0)
    ck_g, sk_g = tables(0, 2)
    for hd in range(MLA_HEADS):
        qa_ref[hd] = head_a(qa, hd, cq_g, sq_g)
        ka_ref[hd] = head_a(kv, hd, ck_g, sk_g)
    for pr in range(MLA_HEADS // 2):
        blk = kv[:, 2048 + pr * LANES:2048 + (pr + 1) * LANES]
        va_ref[2 * pr, :, 0:LANES] = jnp.where(lo, blk, 0.0).astype(BF16)
        va_ref[2 * pr + 1, :, 0:LANES] = jnp.where(lo, 0.0, blk).astype(BF16)
    for hd in range(MLA_HEADS):
        va_ref[hd, :, LANES:V_WIDTH] = ones_col

    def pair_b(blk, partner, cos_g, sin_g):
        sq = blk * blk
        s_lo = jnp.sum(jnp.where(lo, sq, 0.0), axis=-1, keepdims=True)
        s_hi = jnp.sum(jnp.where(lo, 0.0, sq), axis=-1, keepdims=True)
        rr = jnp.where(lo, lax.rsqrt(s_lo * (1.0 / GQA_HEAD_DIM) + NORM_EPS),
                       lax.rsqrt(s_hi * (1.0 / GQA_HEAD_DIM) + NORM_EPS))
        return (blk * cos_g + partner * sin_g) * rr

    cq_g, sq_g = tables(2, 4)
    ck_g, sk_g = tables(2, 6)
    for pr in range(GQA_HEADS // 2):
        blk = p[:, 512 + pr * LANES:512 + (pr + 1) * LANES]
        partner = p[:, 1280 + pr * LANES:1280 + (pr + 1) * LANES]
        qb_ref[pr] = pair_b(blk, partner, cq_g, sq_g).astype(BF16)
    kb = pair_b(p[:, 1024:1152], p[:, 1792:1920], ck_g, sk_g)
    kb_sw = pltpu.roll(kb, 64, 1)
    kb_ref[0] = jnp.where(lo, kb, 0.0).astype(BF16)
    kb_ref[1] = jnp.where(lo, 0.0, kb_sw).astype(BF16)
    kb_ref[2] = jnp.where(lo, kb_sw, 0.0).astype(BF16)
    kb_ref[3] = jnp.where(lo, 0.0, kb).astype(BF16)
    vb = p[:, 1152:1280]
    vb_sw = pltpu.roll(vb, 64, 1)
    vb_ref[0, :, 0:LANES] = jnp.where(lo, vb, 0.0).astype(BF16)
    vb_ref[1, :, 0:LANES] = jnp.where(lo, 0.0, vb_sw).astype(BF16)
    vb_ref[2, :, 0:LANES] = jnp.where(lo, vb_sw, 0.0).astype(BF16)
    vb_ref[3, :, 0:LANES] = jnp.where(lo, 0.0, vb).astype(BF16)
    for j in range(2 * GQA_KV_HEADS):
        vb_ref[j, :, LANES:V_WIDTH] = ones_col


def _mod_index(i):
    return jnp.where(i < LAT_TILES, i // SEQ_TILES, BATCH)


def _lat_ctx_specs(width, ctx_tile0):
    return [pl.BlockSpec((TM, width), lambda i: (jnp.minimum(i, LAT_TILES - 1), 0)),
            pl.BlockSpec((TM, width), lambda i: (ctx_tile0 + jnp.maximum(i - LAT_TILES, 0), 0))]


def _projection(x_lat, x_ctx, ctx_tile0, mod3, gn, w_in, glat, w_uq, w_kv, g128, tab):
    const2 = lambda i: (0, 0)
    head_out = lambda n, w: pl.BlockSpec((n, TM, w), lambda i: (0, i, 0))
    head_shape = lambda n, w: jax.ShapeDtypeStruct((n, N_TOK, w), BF16)
    return pl.pallas_call(
        _proj_kernel,
        grid=(TOK_TILES,),
        in_specs=_lat_ctx_specs(D_MODEL, ctx_tile0) + [
                  pl.BlockSpec((1, 6, D_MODEL), lambda i: (_mod_index(i), 0, 0)),
                  pl.BlockSpec((1, D_MODEL), const2),
                  pl.BlockSpec((D_MODEL, IN_PAD), const2),
                  pl.BlockSpec((2, 256), const2),
                  pl.BlockSpec((MLA_Q_RANK, 2 * MLA_HEADS * LANES), const2),
                  pl.BlockSpec((256, 2 * MLA_HEADS * LANES + MLA_HEADS * MLA_V), const2),
                  pl.BlockSpec((8, LANES), const2),
                  pl.BlockSpec((4, TM, LANES),
                               lambda i: (0, jnp.where(i < LAT_TILES, i % SEQ_TILES, SEQ_TILES), 0))],
        out_specs=[head_out(8, LANES), head_out(8, LANES), head_out(8, V_WIDTH),
                   head_out(4, LANES), head_out(4, LANES), head_out(4, V_WIDTH)],
        out_shape=[head_shape(8, LANES), head_shape(8, LANES), head_shape(8, V_WIDTH),
                   head_shape(4, LANES), head_shape(4, LANES), head_shape(4, V_WIDTH)],
        compiler_params=_params("parallel"),
        name="projection",
    )(x_lat, x_ctx, mod3, gn, w_in, glat, w_uq, w_kv, g128, tab)


def _attn_kernel(*refs, with_lat):
    if with_lat:
        q0_ref, q1_ref, kl_ref, kc_ref, vl_ref, vc_ref, o_ref = refs
    else:
        q0_ref, q1_ref, kc_ref, vc_ref, o_ref = refs
    nt = (((1,), (1,)), ((), ()))
    rows = q0_ref.shape[0]
    for r0 in range(0, rows, Q_CHUNK):
        out = None
        for u, q_ref in enumerate((q0_ref, q1_ref)):
            q = q_ref[r0:r0 + Q_CHUNK, :]
            s_c = lax.dot_general(q, kc_ref[u], nt, preferred_element_type=F32)
            m = jnp.max(s_c, axis=-1, keepdims=True)
            if with_lat:
                s_l = lax.dot_general(q, kl_ref[u], nt, preferred_element_type=F32)
                m = jnp.maximum(m, jnp.max(s_l, axis=-1, keepdims=True))
            acc = jnp.dot(jnp.exp2((s_c - m).astype(BF16)), vc_ref[u], preferred_element_type=F32)
            if with_lat:
                acc = acc + jnp.dot(jnp.exp2((s_l - m).astype(BF16)), vl_ref[u],
                                    preferred_element_type=F32)
            o = acc[:, 0:LANES] / acc[:, LANES:LANES + 1]
            out = o if out is None else out + o
        o_ref[r0:r0 + Q_CHUNK, :] = out.astype(o_ref.dtype)


def _attention(q, k, v, q_heads, kv_pair):
    n_units = 4
    qt = SEQ // TQ
    ctx_blk0 = N_LAT // CTX_LEN

    def qspec(which):
        return pl.BlockSpec((None, TQ, LANES), lambda b, p, i: (q_heads(p)[which], b * qt + i, 0))

    def kvspec(rows, blk0, width):
        return pl.BlockSpec((2, rows, width), lambda b, p, i: (kv_pair(p), blk0 + b, 0))

    return pl.pallas_call(
        functools.partial(_attn_kernel, with_lat=True),
        grid=(BATCH, n_units, qt),
        in_specs=[qspec(0), qspec(1), kvspec(SEQ, 0, LANES), kvspec(CTX_LEN, ctx_blk0, LANES),
                  kvspec(SEQ, 0, V_WIDTH), kvspec(CTX_LEN, ctx_blk0, V_WIDTH)],
        out_specs=pl.BlockSpec((TQ, LANES), lambda b, p, i: (b * qt + i, p)),
        out_shape=jax.ShapeDtypeStruct((N_LAT, n_units * LANES), BF16),
        compiler_params=_params("parallel", "parallel", "parallel"),
        name="attention_latent",
    )(q, q, k, k, v, v)


def _attention_ctx(q, k, v, q_heads, kv_pair):
    n_units = 4
    ctx_blk0 = N_LAT // CTX_LEN

    def qspec(which):
        return pl.BlockSpec((None, CTX_LEN, LANES), lambda b, p: (q_heads(p)[which], ctx_blk0 + b, 0))

    def kvspec(width):
        return pl.BlockSpec((2, CTX_LEN, width), lambda b, p: (kv_pair(p), ctx_blk0 + b, 0))

    return pl.pallas_call(
        functools.partial(_attn_kernel, with_lat=False),
        grid=(BATCH, n_units),
        in_specs=[qspec(0), qspec(1), kvspec(LANES), kvspec(V_WIDTH)],
        out_specs=pl.BlockSpec((CTX_LEN, LANES), lambda b, p: (b, p)),
        out_shape=jax.ShapeDtypeStruct((N_CTX, n_units * LANES), BF16),
        compiler_params=_params("parallel", "parallel"),
        name="attention_context",
    )(q, q, k, v)


_A_HEADS = lambda p: (2 * p, 2 * p + 1)
_A_KV = lambda p: p
_B_HEADS = lambda p: (p, p)
_B_KV = lambda p: p // 2


def _post_body(x, oa, ob, mod_ref, gn_ref, wo_ref):
    y = (jnp.dot(oa, wo_ref[0:512, :], preferred_element_type=F32)
         + jnp.dot(ob, wo_ref[512:1024, :], preferred_element_type=F32))
    x1 = x + mod_ref[0, 2:3, :] * y
    r = lax.rsqrt(jnp.mean(x1 * x1, axis=-1, keepdims=True) + NORM_EPS)
    h2 = (x1 * r * gn_ref[...]) * (1.0 + mod_ref[0, 4:5, :]) + mod_ref[0, 3:4, :]
    return x1, h2


def _post_kernel(xl_ref, xc_ref, oa_ref, oac_ref, ob_ref, obc_ref, mod_ref, gn_ref, wo_ref,
                 x1_ref, h2_ref):
    x1, h2 = _post_body(_tile_rows(xl_ref, xc_ref), _tile_rows(oa_ref, oac_ref),
                        _tile_rows(ob_ref, obc_ref), mod_ref, gn_ref, wo_ref)
    x1_ref[...] = x1
    h2_ref[...] = h2.astype(BF16)


def _post_router_kernel(x_ref, oa_ref, ob_ref, mod_ref, gn_ref, wo_ref, wr_ref,
                        x1_ref, hrow_ref, route_ref, count_ref, carry_ref):
    x1, h2 = _post_body(x_ref[...], oa_ref[...], ob_ref[...], mod_ref, gn_ref, wo_ref)
    x1_ref[...] = x1
    for s in range(ROW_SUB):
        hrow_ref[pl.ds(s, TM, stride=ROW_SUB), :] = h2[:, s * LANES:(s + 1) * LANES]

    h_hi = h2.astype(BF16)
    h_lo = (h2 - h_hi.astype(F32)).astype(BF16)
    part = jnp.dot(h_hi, wr_ref[...], preferred_element_type=F32)
    logits = (part[:, 0:LANES] + part[:, LANES:2 * LANES]
              + jnp.dot(h_lo, wr_ref[:, 0:LANES], preferred_element_type=F32))
    lane = lax.broadcasted_iota(jnp.int32, logits.shape, 1)
    valid = lane < N_EXPERTS
    lg = jnp.where(valid, logits, -jnp.inf)
    e = jnp.exp(lg - jnp.max(lg, axis=-1, keepdims=True))
    probs = e / jnp.sum(e, axis=-1, keepdims=True)
    v1 = jnp.max(probs, axis=-1, keepdims=True)
    i1 = jnp.min(jnp.where(probs == v1, lane, LANES), axis=-1, keepdims=True)
    rest = jnp.where(valid & (lane != i1), probs, -1.0)
    v2 = jnp.max(rest, axis=-1, keepdims=True)
    i2 = jnp.min(jnp.where(rest == v2, lane, LANES), axis=-1, keepdims=True)

    @pl.when(pl.program_id(0) == 0)
    def _():
        carry_ref[...] = jnp.zeros_like(carry_ref)

    pick1 = lane == i1
    pick2 = lane == i2
    onehot = jnp.where(pick1, 1.0, 0.0) + jnp.where(pick2, 1.0, 0.0)
    row = lax.broadcasted_iota(jnp.int32, (TM, TM), 0)
    col = lax.broadcasted_iota(jnp.int32, (TM, TM), 1)
    before = jnp.where(col < row, 1.0, 0.0).astype(BF16)
    seen = jnp.dot(before, onehot.astype(BF16), preferred_element_type=F32) + carry_ref[0:1, :]
    rank1 = jnp.sum(jnp.where(pick1, seen, 0.0), axis=-1, keepdims=True)
    rank2 = jnp.sum(jnp.where(pick2, seen, 0.0), axis=-1, keepdims=True)
    carry_ref[...] = carry_ref[...] + jnp.sum(onehot, axis=0, keepdims=True)
    count_ref[...] = carry_ref[...]

    fields = (v1 / (v1 + v2), v2 / (v1 + v2), i1.astype(F32), i2.astype(F32), rank1, rank2)
    route = jnp.zeros((TM, LANES), F32)
    for k, val in enumerate(fields):
        route = jnp.where(lane == k, val, route)
    route_ref[...] = route


def _post_attention(x_lat, x_ctx, o_lat, o_ctx, mod3, gn, w_out, w_router):
    const2 = lambda i: (0, 0)
    n_tiles = LAT_TILES if o_ctx is None else TOK_TILES
    rows = n_tiles * TM
    tail_specs = [pl.BlockSpec((1, 6, D_MODEL), lambda i: (_mod_index(i), 0, 0)),
                  pl.BlockSpec((1, D_MODEL), const2),
                  pl.BlockSpec((D_MODEL, D_MODEL), const2)]
    x1_spec = pl.BlockSpec((TM, D_MODEL), lambda i: (i, 0))
    x1_shape = jax.ShapeDtypeStruct((rows, D_MODEL), F32)
    if o_ctx is not None:
        return pl.pallas_call(
            _post_kernel, grid=(n_tiles,),
            in_specs=(_lat_ctx_specs(D_MODEL, 0) + _lat_ctx_specs(512, 0) + _lat_ctx_specs(512, 0)
                      + tail_specs),
            out_specs=[x1_spec, x1_spec],
            out_shape=[x1_shape, jax.ShapeDtypeStruct((rows, D_MODEL), BF16)],
            compiler_params=_params("parallel"), name="post_attention",
        )(x_lat, x_ctx, o_lat[0], o_ctx[0], o_lat[1], o_ctx[1], mod3, gn, w_out)
    half_spec = pl.BlockSpec((TM, 512), lambda i: (i, 0))
    in_specs = [x1_spec, half_spec, half_spec] + tail_specs
    args = [x_lat, *o_lat, mod3, gn, w_out]
    return pl.pallas_call(
        _post_router_kernel, grid=(n_tiles,),
        in_specs=in_specs + [pl.BlockSpec((D_MODEL, 2 * LANES), const2)],
        out_specs=[x1_spec,
                   pl.BlockSpec((TM * ROW_SUB, LANES), lambda i: (i, 0)),
                   pl.BlockSpec((TM, LANES), lambda i: (i, 0)),
                   pl.BlockSpec((8, LANES), const2)],
        out_shape=[x1_shape,
                   jax.ShapeDtypeStruct((rows * ROW_SUB, LANES), F32),
                   jax.ShapeDtypeStruct((rows, LANES), F32),
                   jax.ShapeDtypeStruct((8, LANES), F32)],
        scratch_shapes=[pltpu.VMEM((8, LANES), F32)],
        compiler_params=_params("arbitrary"), name="post_attention_router",
    )(*args, w_router)


def _swiglu(h, wg_ref, wu_ref, wd_ref):
    y = None
    for lo, hi in zip(FF_SPLITS[:-1], FF_SPLITS[1:]):
        g = jnp.dot(h, wg_ref[:, lo:hi], preferred_element_type=F32)
        u = jnp.dot(h, wu_ref[:, lo:hi], preferred_element_type=F32)
        yc = jnp.dot((_silu(g) * u).astype(BF16), wd_ref[lo:hi, :], preferred_element_type=F32)
        y = yc if y is None else y + yc
    return y


def _ffn_kernel(h_ref, x1_ref, mod_ref, wg_ref, wu_ref, wd_ref, o_ref):
    y = _swiglu(h_ref[...], wg_ref, wu_ref, wd_ref)
    o_ref[...] = x1_ref[...] + mod_ref[0, 5:6, :] * y


def _dense_ffn(h2, x1, mod3, wg, wu, wd):
    resident = lambda shape: pl.BlockSpec(shape, lambda i: (0, 0), pipeline_mode=pl.Buffered(1))
    return pl.pallas_call(
        _ffn_kernel,
        grid=(TOK_TILES,),
        in_specs=[pl.BlockSpec((TM, D_MODEL), lambda i: (i, 0)),
                  pl.BlockSpec((TM, D_MODEL), lambda i: (i, 0)),
                  pl.BlockSpec((1, 6, D_MODEL), lambda i: (_mod_index(i), 0, 0)),
                  resident((D_MODEL, DENSE_FF)), resident((D_MODEL, DENSE_FF)),
                  resident((DENSE_FF, D_MODEL))],
        out_specs=pl.BlockSpec((TM, D_MODEL), lambda i: (i, 0)),
        out_shape=jax.ShapeDtypeStruct((N_TOK, D_MODEL), F32),
        compiler_params=_params("parallel"),
        name="dense_ffn",
    )(h2, x1, mod3, wg, wu, wd)


def _row(ref, r):
    return ref.at[pl.ds(pl.multiple_of(r * ROW_SUB, ROW_SUB), ROW_SUB)]


def _rows_to_matrix(ref, n):
    return jnp.concatenate([ref[pl.ds(s, n, stride=ROW_SUB), :] for s in range(ROW_SUB)], axis=-1)


def _expert_kernel(te_ref, nu_ref, inv_ref, h_ref, wg_ref, wu_ref, wd_ref, y2_ref,
                   xbuf, ybuf, sem_g, sem_s):
    del te_ref
    t = pl.program_id(0)
    n_used = nu_ref[0]
    slot = lax.rem(t, 2)
    other = 1 - slot

    def gather_rows(tile, s):
        base = tile * TMX
        for j in range(TMX):
            token = lax.shift_right_logical(jnp.maximum(inv_ref[base + j], 0), 1)
            pltpu.make_async_copy(_row(h_ref, token), _row(xbuf.at[s], j), sem_g.at[s]).start()

    def scatter_rows(tile, s, real, scratch_block):
        base = tile * TMX
        for j in range(TMX):
            code = inv_ref[base + j]
            row = jnp.where(jnp.logical_and(code >= 0, real),
                            (code & 1) * N_LAT + lax.shift_right_logical(code, 1),
                            2 * N_LAT + scratch_block * TMX + j)
            pltpu.make_async_copy(_row(ybuf.at[s], j), _row(y2_ref, row), sem_s.at[s]).start()

    def wait_gather(s):
        pltpu.make_async_copy(h_ref.at[pl.ds(0, TMX * ROW_SUB)], xbuf.at[s], sem_g.at[s]).wait()

    def wait_scatter(s):
        pltpu.make_async_copy(ybuf.at[s], y2_ref.at[pl.ds(0, TMX * ROW_SUB)], sem_s.at[s]).wait()

    @pl.when(t == 0)
    def _():
        ybuf[...] = jnp.zeros_like(ybuf)
        gather_rows(0, 0)

    @pl.when(t < n_used)
    def _():
        wait_gather(slot)
        x = _rows_to_matrix(xbuf.at[slot], TMX).astype(BF16)
        gather_rows(jnp.minimum(t + 1, n_used - 1), other)
        scatter_rows(jnp.maximum(t - 1, 0), other, t > 0, other)
        y = _swiglu(x, wg_ref.at[0], wu_ref.at[0], wd_ref.at[0])

        @pl.when(t > 0)
        def _():
            wait_scatter(slot)

        for s in range(ROW_SUB):
            ybuf[slot, pl.ds(s, TMX, stride=ROW_SUB), :] = y[:, s * LANES:(s + 1) * LANES]

    @pl.when(t == n_used)
    def _():
        wait_gather(slot)
        scatter_rows(n_used - 1, other, True, other)
        wait_scatter(other)
        wait_scatter(slot)
        scatter_rows(n_used - 1, slot, False, 0)
        wait_scatter(slot)


def _experts(tile_expert, n_used, inv, hrow, wg, wu, wd):
    one = pl.Buffered(1)
    wspec = lambda shape: pl.BlockSpec(shape, lambda t, te, nu, iv: (te[t], 0, 0), pipeline_mode=one)
    grid_spec = pltpu.PrefetchScalarGridSpec(
        num_scalar_prefetch=3,
        grid=(SORT_TILES + 1,),
        in_specs=[pl.BlockSpec(memory_space=pl.ANY),
                  wspec((1, D_MODEL, EXPERT_FF)), wspec((1, D_MODEL, EXPERT_FF)),
                  wspec((1, EXPERT_FF, D_MODEL))],
        out_specs=pl.BlockSpec(memory_space=pl.ANY),
        scratch_shapes=[pltpu.VMEM((2, TMX * ROW_SUB, LANES), F32),
                        pltpu.VMEM((2, TMX * ROW_SUB, LANES), F32),
                        pltpu.SemaphoreType.DMA((2,)), pltpu.SemaphoreType.DMA((2,))])
    return pl.pallas_call(
        _expert_kernel,
        grid_spec=grid_spec,
        out_shape=jax.ShapeDtypeStruct((Y2_ROWS * ROW_SUB, LANES), F32),
        compiler_params=_params("arbitrary"),
        name="moe_experts",
    )(tile_expert, n_used, inv, hrow, wg, wu, wd)


def _combine_kernel(ya_ref, yb_ref, x1_ref, route_ref, mod_ref, o_ref):
    route = route_ref[...]
    y = (route[:, 0:1] * _rows_to_matrix(ya_ref, TM) + route[:, 1:2] * _rows_to_matrix(yb_ref, TM))
    o_ref[...] = x1_ref[...] + mod_ref[0, 5:6, :] * y


def _combine(y2, x1, route, mod3):
    return pl.pallas_call(
        _combine_kernel,
        grid=(LAT_TILES,),
        in_specs=[pl.BlockSpec((TM * ROW_SUB, LANES), lambda i: (i, 0)),
                  pl.BlockSpec((TM * ROW_SUB, LANES), lambda i: (LAT_TILES + i, 0)),
                  pl.BlockSpec((TM, D_MODEL), lambda i: (i, 0)),
                  pl.BlockSpec((TM, LANES), lambda i: (i, 0)),
                  pl.BlockSpec((1, 6, D_MODEL), lambda i: (i // SEQ_TILES, 0, 0))],
        out_specs=pl.BlockSpec((TM, D_MODEL), lambda i: (i, 0)),
        out_shape=jax.ShapeDtypeStruct((N_LAT, D_MODEL), F32),
        compiler_params=_params("parallel"),
        name="moe_combine",
    )(y2, y2, x1, route, mod3)


def _moe_ffn(hrow, x1, route, counts, mod3, wg, wu, wd):
    expert = route[:, 2:4].astype(jnp.int32)
    rank = route[:, 4:6].astype(jnp.int32)
    tiles = (counts[0, :N_EXPERTS].astype(jnp.int32) + TMX - 1) // TMX
    tile_end = jnp.cumsum(tiles)
    dest = (tile_end - tiles)[expert] * TMX + rank
    code = 2 * jnp.arange(N_LAT, dtype=jnp.int32)[:, None] + jnp.arange(2, dtype=jnp.int32)[None, :]
    inv = jnp.full((SORT_ROWS,), -1, jnp.int32).at[dest.reshape(-1)].set(
        code.reshape(-1), unique_indices=True)
    n_used = tile_end[N_EXPERTS - 1:]
    tile_ids = jnp.minimum(jnp.arange(SORT_TILES + 1), n_used[0] - 1)
    tile_expert = jnp.sum(tile_ids[:, None] >= tile_end[None, :], axis=1).astype(jnp.int32)
    y2 = _experts(tile_expert, n_used.astype(jnp.int32), inv, hrow, wg, wu, wd)
    return _combine(y2, x1, route, mod3)


def _partner(half):
    idx = np.arange(4 * half)
    return np.where((idx // half) % 2 == 0, idx + half, idx - half)


_PARTNER_A = np.concatenate([np.arange(MLA_NOPE), MLA_NOPE + _partner(MLA_ROPE // 4),
                             np.arange(MLA_QK, LANES)])
_PARTNER_B = np.concatenate([_partner(GQA_HEAD_DIM // 4), GQA_HEAD_DIM + _partner(GQA_HEAD_DIM // 4)])


def _rope_tables():
    t = jnp.arange(SEQ)
    row = (t // GRID_W).astype(F32)
    col = (t % GRID_W).astype(F32)

    def one_axis(pos, half):
        freqs = ROPE_THETA ** (-jnp.arange(half, dtype=F32) / half)
        ang = pos[:, None] * freqs[None, :]
        cos, sin = jnp.cos(ang), jnp.sin(ang)
        return jnp.concatenate([cos, cos], -1), jnp.concatenate([-sin, sin], -1)

    def two_axes(half):
        r, c = one_axis(row, half), one_axis(col, half)
        return [jnp.concatenate([a, b], -1) for a, b in zip(r, c)]

    def pad_a(tbl, fill):
        return jnp.concatenate([jnp.full((SEQ, MLA_NOPE), fill, F32), tbl,
                                jnp.full((SEQ, LANES - MLA_QK), fill, F32)], -1)

    ta = two_axes(MLA_ROPE // 4)
    tb = [jnp.concatenate([x, x], -1) for x in two_axes(GQA_HEAD_DIM // 4)]
    tabs = [pad_a(ta[0], 1.0), pad_a(ta[1], 0.0)] + tb
    ident = [jnp.ones((TM, LANES), F32), jnp.zeros((TM, LANES), F32)]
    return jnp.stack([jnp.concatenate([tbl, idn], 0) for tbl, idn in zip(tabs, ident + ident)])


def _layer_weights(w_in, q_lat_norm, w_uq, kv_lat_norm, w_ukv, mla_q_gain, mla_k_gain,
                   gqa_q_gain, gqa_k_gain):
    split = MLA_Q_RANK + MLA_KV_RANK + MLA_ROPE
    perm_a = (np.arange(MLA_HEADS)[:, None] * LANES + _PARTNER_A[None, :]).reshape(-1)
    perm_b = (np.arange(GQA_HEADS // 2)[:, None] * LANES + _PARTNER_B[None, :]).reshape(-1)
    w_qb = w_in[:, split:split + GQA_HEADS * GQA_HEAD_DIM]
    w_kb = w_in[:, split + 512:split + 640]
    w_in_p = jnp.concatenate([w_in[:, :split], jnp.zeros((D_MODEL, 512 - split), F32),
                              w_in[:, split:], w_qb[:, perm_b], w_kb[:, _PARTNER_B]],
                             axis=1).astype(BF16)
    w_uq_p = jnp.pad(w_uq.reshape(MLA_Q_RANK, MLA_HEADS, MLA_QK),
                     ((0, 0), (0, 0), (0, LANES - MLA_QK))).reshape(MLA_Q_RANK, MLA_HEADS * LANES)
    w_uq_p = jnp.concatenate([w_uq_p, w_uq_p[:, perm_a]], axis=1)
    ukv = w_ukv.reshape(MLA_KV_RANK, MLA_HEADS, MLA_NOPE + MLA_V)
    w_k = jnp.pad(ukv[:, :, :MLA_NOPE], ((0, 0), (0, 0), (0, LANES - MLA_NOPE)))
    place = jnp.pad(jnp.eye(MLA_ROPE, dtype=F32), ((0, 0), (MLA_NOPE, LANES - MLA_QK)))
    place = jnp.broadcast_to(place[:, None, :], (MLA_ROPE, MLA_HEADS, LANES))
    w_k = jnp.concatenate([w_k, place,
                           jnp.zeros((256 - MLA_KV_RANK - MLA_ROPE, MLA_HEADS, LANES), F32)], 0)
    w_k = w_k.reshape(256, MLA_HEADS * LANES)
    w_v = jnp.pad(ukv[:, :, MLA_NOPE:].reshape(MLA_KV_RANK, MLA_HEADS * MLA_V),
                  ((0, 256 - MLA_KV_RANK), (0, 0)))
    w_kv = jnp.concatenate([w_k, w_k[:, perm_a], w_v], axis=1)
    glat = jnp.stack([q_lat_norm, jnp.pad(kv_lat_norm, (0, 256 - MLA_KV_RANK))])
    pad_qk = lambda g: jnp.pad(g, (0, LANES - MLA_QK))
    gains = [pad_qk(mla_q_gain) * (MLA_SCALE * LOG2E), pad_qk(mla_k_gain),
             jnp.tile(gqa_q_gain, 2) * (GQA_SCALE * LOG2E), jnp.tile(gqa_k_gain, 2)]
    partners = [_PARTNER_A, _PARTNER_A, _PARTNER_B, _PARTNER_B]
    g128 = jnp.stack([v for g, pm in zip(gains, partners) for v in (g, g[pm])])
    return w_in_p, glat, w_uq_p.astype(BF16), w_kv.astype(BF16), g128


def _mixer(x_lat, x_ctx, ctx_tile0, mod3, tab, norm_attn, w_in, q_lat_norm, w_uq, kv_lat_norm,
           w_ukv, mla_q_gain, mla_k_gain, gqa_q_gain, gqa_k_gain, with_ctx_queries):
    w_in_p, glat, w_uq_p, w_kv, g128 = _layer_weights(
        w_in, q_lat_norm, w_uq, kv_lat_norm, w_ukv, mla_q_gain, mla_k_gain, gqa_q_gain, gqa_k_gain)
    qa, ka, va, qb, kb, vb = _projection(x_lat, x_ctx, ctx_tile0, mod3,
                                         norm_attn.reshape(1, D_MODEL), w_in_p, glat,
                                         w_uq_p, w_kv, g128, tab)
    o_lat = (_attention(qa, ka, va, _A_HEADS, _A_KV), _attention(qb, kb, vb, _B_HEADS, _B_KV))
    if not with_ctx_queries:
        return o_lat, None
    return o_lat, (_attention_ctx(qa, ka, va, _A_HEADS, _A_KV),
                   _attention_ctx(qb, kb, vb, _B_HEADS, _B_KV))


def kernel(x, c, ctx, c_ctx, l0_w_mod, l0_b_mod, l0_norm_attn, l0_w_in, l0_q_lat_norm, l0_w_uq, l0_kv_lat_norm, l0_w_ukv, l0_mla_q_gain, l0_mla_k_gain, l0_gqa_q_gain, l0_gqa_k_gain, l0_w_out, l0_norm_ffn, l0_ffn_w_gate, l0_ffn_w_up, l0_ffn_w_down, l1_w_mod, l1_b_mod, l1_norm_attn, l1_w_in, l1_q_lat_norm, l1_w_uq, l1_kv_lat_norm, l1_w_ukv, l1_mla_q_gain, l1_mla_k_gain, l1_gqa_q_gain, l1_gqa_k_gain, l1_w_out, l1_norm_ffn, l1_router, l1_exp_w_gate, l1_exp_w_up, l1_exp_w_down):
    x_lat = x.reshape(N_LAT, D_MODEL)
    x_ctx = ctx.reshape(N_CTX, D_MODEL)
    cc = jnp.concatenate([c, c_ctx[None, :], jnp.zeros((16 - BATCH - 1, D_MODEL), F32)], axis=0)
    tab = _rope_tables()

    mod3 = _modulation(cc, l0_w_mod, l0_b_mod)
    o_lat, o_ctx = _mixer(x_lat, x_ctx, 0, mod3, tab, l0_norm_attn, l0_w_in, l0_q_lat_norm,
                          l0_w_uq, l0_kv_lat_norm, l0_w_ukv, l0_mla_q_gain, l0_mla_k_gain,
                          l0_gqa_q_gain, l0_gqa_k_gain, with_ctx_queries=True)
    x1, h2 = _post_attention(x_lat, x_ctx, o_lat, o_ctx, mod3, l0_norm_ffn.reshape(1, D_MODEL),
                             l0_w_out.astype(BF16), None)
    xall = _dense_ffn(h2, x1, mod3, l0_ffn_w_gate.astype(BF16), l0_ffn_w_up.astype(BF16),
                      l0_ffn_w_down.astype(BF16))

    mod3 = _modulation(cc, l1_w_mod, l1_b_mod)
    o_lat, _ = _mixer(xall, xall, LAT_TILES, mod3, tab, l1_norm_attn, l1_w_in, l1_q_lat_norm,
                      l1_w_uq, l1_kv_lat_norm, l1_w_ukv, l1_mla_q_gain, l1_mla_k_gain,
                      l1_gqa_q_gain, l1_gqa_k_gain, with_ctx_queries=False)
    w_router = jnp.pad(l1_router, ((0, 0), (0, LANES - N_EXPERTS)))
    w_router_hi = w_router.astype(BF16)
    w_router = jnp.concatenate([w_router_hi, (w_router - w_router_hi.astype(F32)).astype(BF16)],
                               axis=1)
    x1, hrow, route, counts = _post_attention(xall, None, o_lat, None, mod3,
                                              l1_norm_ffn.reshape(1, D_MODEL),
                                              l1_w_out.astype(BF16), w_router)
    out = _moe_ffn(hrow, x1, route, counts, mod3, l1_exp_w_gate.astype(BF16),
                   l1_exp_w_up.astype(BF16), l1_exp_w_down.astype(BF16))
    return out.reshape(BATCH, SEQ, D_MODEL)
```

```python
import functools
import math

import jax
import jax.numpy as jnp
import numpy as np
from jax import lax
from jax.experimental import pallas as pl
from jax.experimental.pallas import tpu as pltpu

D_MODEL = 1024
BATCH = 8
SEQ = 2048
CTX_LEN = 256
GRID_W = 64
MLA_HEADS = 8
MLA_NOPE = 64
MLA_ROPE = 32
MLA_V = 64
MLA_QK = MLA_NOPE + MLA_ROPE
MLA_Q_RANK = 256
MLA_KV_RANK = 128
GQA_HEADS = 8
GQA_KV_HEADS = 2
GQA_HEAD_DIM = 64
DENSE_FF = 2816
N_EXPERTS = 8
EXPERT_FF = 2816
ROPE_THETA = 10000.0
NORM_EPS = 1e-6
LOG2E = math.log2(math.e)
MLA_SCALE = MLA_QK ** -0.5
GQA_SCALE = GQA_HEAD_DIM ** -0.5

LANES = 128
N_LAT = BATCH * SEQ
N_CTX = BATCH * CTX_LEN
N_TOK = N_LAT + N_CTX
TM = 512
LAT_TILES = N_LAT // TM
TOK_TILES = N_TOK // TM
SEQ_TILES = SEQ // TM
Q_CHUNK = 256
V_WIDTH = 2 * LANES
IN_PAD = 1920
FF_SPLITS = (0, 1536, 2816)
ROW_SUB = D_MODEL // LANES
TMX = 512
SORT_TILES = 2 * N_LAT // TMX + N_EXPERTS
SORT_ROWS = SORT_TILES * TMX
Y2_ROWS = 2 * N_LAT + 2 * TMX
VMEM_LIMIT = 56 * 1024 * 1024

F32 = jnp.float32
BF16 = jnp.bfloat16


def _silu(x):
    return x / (1.0 + jnp.exp(-x))


def _params(*sem):
    return pltpu.CompilerParams(dimension_semantics=sem, vmem_limit_bytes=VMEM_LIMIT)


def _mod_kernel(c_ref, w_ref, b_ref, o_ref):
    s = _silu(c_ref[...])
    o_ref[...] = jnp.dot(s, w_ref[...], precision=lax.Precision.HIGHEST,
                         preferred_element_type=F32) + b_ref[...]


def _modulation(cc, w_mod, b_mod):
    n = w_mod.shape[1]
    bn = 1024
    out = pl.pallas_call(
        _mod_kernel,
        grid=(n // bn,),
        in_specs=[pl.BlockSpec((16, D_MODEL), lambda j: (0, 0)),
                  pl.BlockSpec((D_MODEL, bn), lambda j: (0, j)),
                  pl.BlockSpec((1, bn), lambda j: (0, j))],
        out_specs=pl.BlockSpec((16, bn), lambda j: (0, j)),
        out_shape=jax.ShapeDtypeStruct((16, n), F32),
        compiler_params=_params("parallel"),
        name="modulation",
    )(cc, w_mod, b_mod.reshape(1, n))
    return out.reshape(16, 6, D_MODEL)


def _tile_rows(lat_ref, ctx_ref):
    return jnp.where(pl.program_id(0) < LAT_TILES, lat_ref[...], ctx_ref[...])


def _proj_kernel(xl_ref, xc_ref, mod_ref, gn_ref, win_ref, glat_ref, wuq_ref, wkv_ref, g128_ref,
                 tab_ref, qa_ref, ka_ref, va_ref, qb_ref, kb_ref, vb_ref):
    x = _tile_rows(xl_ref, xc_ref)
    r = lax.rsqrt(jnp.mean(x * x, axis=-1, keepdims=True) + NORM_EPS)
    h = (x * r * gn_ref[...]) * (1.0 + mod_ref[0, 1:2, :]) + mod_ref[0, 0:1, :]
    p = jnp.dot(h.astype(BF16), win_ref[...], preferred_element_type=F32)

    lane = lax.broadcasted_iota(jnp.int32, (TM, LANES), 1)
    lo = lane < 64
    ones_col = jnp.where(lane == 0, 1.0, 0.0).astype(BF16)

    def tables(base, row):
        return tab_ref[base] * g128_ref[row:row + 1, :], tab_ref[base + 1] * g128_ref[row + 1:row + 2, :]

    cq = p[:, 0:256]
    rq = lax.rsqrt(jnp.mean(cq * cq, axis=-1, keepdims=True) + NORM_EPS)
    qa = jnp.dot((cq * rq * glat_ref[0:1, :]).astype(BF16), wuq_ref[...],
                 preferred_element_type=F32)
    ckv = p[:, 256:384]
    rkv = lax.rsqrt(jnp.mean(ckv * ckv, axis=-1, keepdims=True) + NORM_EPS)
    slab = jnp.concatenate([ckv * rkv * glat_ref[1:2, 0:128], p[:, 384:512]], axis=-1)
    kv = jnp.dot(slab.astype(BF16), wkv_ref[...], preferred_element_type=F32)

    def head_a(src, hd, cos_g, sin_g):
        blk = src[:, hd * LANES:(hd + 1) * LANES]
        partner = src[:, 1024 + hd * LANES:1024 + (hd + 1) * LANES]
        rr = lax.rsqrt(jnp.sum(blk * blk, axis=-1, keepdims=True) * (1.0 / MLA_QK) + NORM_EPS)
        return ((blk * cos_g + partner * sin_g) * rr).astype(BF16)

    cq_g, sq_g = tables(0, 0)
    ck_g, sk_g = tables(0, 2)
    for hd in range(MLA_HEADS):
        qa_ref[hd] = head_a(qa, hd, cq_g, sq_g)
        ka_ref[hd] = head_a(kv, hd, ck_g, sk_g)
    for pr in range(MLA_HEADS // 2):
        blk = kv[:, 2048 + pr * LANES:2048 + (pr + 1) * LANES]
        va_ref[2 * pr, :, 0:LANES] = jnp.where(lo, blk, 0.0).astype(BF16)
        va_ref[2 * pr + 1, :, 0:LANES] = jnp.where(lo, 0.0, blk).astype(BF16)
    for hd in range(MLA_HEADS):
        va_ref[hd, :, LANES:V_WIDTH] = ones_col

    def pair_b(blk, partner, cos_g, sin_g):
        sq = blk * blk
        s_lo = jnp.sum(jnp.where(lo, sq, 0.0), axis=-1, keepdims=True)
        s_hi = jnp.sum(jnp.where(lo, 0.0, sq), axis=-1, keepdims=True)
        rr = jnp.where(lo, lax.rsqrt(s_lo * (1.0 / GQA_HEAD_DIM) + NORM_EPS),
                       lax.rsqrt(s_hi * (1.0 / GQA_HEAD_DIM) + NORM_EPS))
        return (blk * cos_g + partner * sin_g) * rr

    cq_g, sq_g = tables(2, 4)
    ck_g, sk_g = tables(2, 6)
    for pr in range(GQA_HEADS // 2):
        blk = p[:, 512 + pr * LANES:512 + (pr + 1) * LANES]
        partner = p[:, 1280 + pr * LANES:1280 + (pr + 1) * LANES]
        qb_ref[pr] = pair_b(blk, partner, cq_g, sq_g).astype(BF16)
    kb = pair_b(p[:, 1024:1152], p[:, 1792:1920], ck_g, sk_g)
    kb_sw = pltpu.roll(kb, 64, 1)
    kb_ref[0] = jnp.where(lo, kb, 0.0).astype(BF16)
    kb_ref[1] = jnp.where(lo, 0.0, kb_sw).astype(BF16)
    kb_ref[2] = jnp.where(lo, kb_sw, 0.0).astype(BF16)
    kb_ref[3] = jnp.where(lo, 0.0, kb).astype(BF16)
    vb = p[:, 1152:1280]
    vb_sw = pltpu.roll(vb, 64, 1)
    vb_ref[0, :, 0:LANES] = jnp.where(lo, vb, 0.0).astype(BF16)
    vb_ref[1, :, 0:LANES] = jnp.where(lo, 0.0, vb_sw).astype(BF16)
    vb_ref[2, :, 0:LANES] = jnp.where(lo, vb_sw, 0.0).astype(BF16)
    vb_ref[3, :, 0:LANES] = jnp.where(lo, 0.0, vb).astype(BF16)
    for j in range(2 * GQA_KV_HEADS):
        vb_ref[j, :, LANES:V_WIDTH] = ones_col


def _mod_index(i):
    return jnp.where(i < LAT_TILES, i // SEQ_TILES, BATCH)


def _lat_ctx_specs(width, ctx_tile0):
    return [pl.BlockSpec((TM, width), lambda i: (jnp.minimum(i, LAT_TILES - 1), 0)),
            pl.BlockSpec((TM, width), lambda i: (ctx_tile0 + jnp.maximum(i - LAT_TILES, 0), 0))]


def _projection(x_lat, x_ctx, ctx_tile0, mod3, gn, w_in, glat, w_uq, w_kv, g128, tab):
    const2 = lambda i: (0, 0)
    head_out = lambda n, w: pl.BlockSpec((n, TM, w), lambda i: (0, i, 0))
    head_shape = lambda n, w: jax.ShapeDtypeStruct((n, N_TOK, w), BF16)
    return pl.pallas_call(
        _proj_kernel,
        grid=(TOK_TILES,),
        in_specs=_lat_ctx_specs(D_MODEL, ctx_tile0) + [
                  pl.BlockSpec((1, 6, D_MODEL), lambda i: (_mod_index(i), 0, 0)),
                  pl.BlockSpec((1, D_MODEL), const2),
                  pl.BlockSpec((D_MODEL, IN_PAD), const2),
                  pl.BlockSpec((2, 256), const2),
                  pl.BlockSpec((MLA_Q_RANK, 2 * MLA_HEADS * LANES), const2),
                  pl.BlockSpec((256, 2 * MLA_HEADS * LANES + MLA_HEADS * MLA_V), const2),
                  pl.BlockSpec((8, LANES), const2),
                  pl.BlockSpec((4, TM, LANES),
                               lambda i: (0, jnp.where(i < LAT_TILES, i % SEQ_TILES, SEQ_TILES), 0))],
        out_specs=[head_out(8, LANES), head_out(8, LANES), head_out(8, V_WIDTH),
                   head_out(4, LANES), head_out(4, LANES), head_out(4, V_WIDTH)],
        out_shape=[head_shape(8, LANES), head_shape(8, LANES), head_shape(8, V_WIDTH),
                   head_shape(4, LANES), head_shape(4, LANES), head_shape(4, V_WIDTH)],
        compiler_params=_params("parallel"),
        name="projection",
    )(x_lat, x_ctx, mod3, gn, w_in, glat, w_uq, w_kv, g128, tab)


def _pair_rows(q_refs, r0, n, kv_refs):
    nt = (((1,), (1,)), ((), ()))
    out = None
    for u, q_ref in enumerate(q_refs):
        q = q_ref[r0:r0 + n, :]
        scores = [lax.dot_general(q, k_ref[u], nt, preferred_element_type=F32)
                  for k_ref, _ in kv_refs]
        m = functools.reduce(jnp.maximum, [jnp.max(sc, axis=-1, keepdims=True) for sc in scores])
        acc = None
        for sc, (_, v_ref) in zip(scores, kv_refs):
            part = jnp.dot(jnp.exp2((sc - m).astype(BF16)), v_ref[u], preferred_element_type=F32)
            acc = part if acc is None else acc + part
        o = acc[:, 0:LANES] / acc[:, LANES:LANES + 1]
        out = o if out is None else out + o
    return out


def _attn_kernel(*refs, ctx_queries, cast):
    q0_ref, q1_ref, kl_ref, kc_ref, vl_ref, vc_ref = refs[:6]
    rest = list(refs[6:])
    qc_refs = (rest.pop(0), rest.pop(0)) if ctx_queries else None
    src_ref = rest.pop(0) if cast else None
    o_ref = rest.pop(0)
    oc_ref = rest.pop(0) if ctx_queries else None
    if cast:
        rest.pop(0)[...] = src_ref[...].astype(BF16)
    for r0 in range(0, SEQ, Q_CHUNK):
        out = _pair_rows((q0_ref, q1_ref), r0, Q_CHUNK, ((kc_ref, vc_ref), (kl_ref, vl_ref)))
        o_ref[r0:r0 + Q_CHUNK, :] = out.astype(o_ref.dtype)
    if ctx_queries:
        oc_ref[...] = _pair_rows(qc_refs, 0, CTX_LEN, ((kc_ref, vc_ref),)).astype(oc_ref.dtype)


def _attention(q, k, v, q_heads, kv_pair, ctx_queries, cast_src):
    n_units = 4
    ctx_blk0 = N_LAT // CTX_LEN

    def qspec(which, rows, blk0):
        return pl.BlockSpec((None, rows, LANES), lambda b, p: (q_heads(p)[which], blk0 + b, 0))

    def kvspec(rows, blk0, width):
        return pl.BlockSpec((2, rows, width), lambda b, p: (kv_pair(p), blk0 + b, 0))

    in_specs = [qspec(0, SEQ, 0), qspec(1, SEQ, 0), kvspec(SEQ, 0, LANES),
                kvspec(CTX_LEN, ctx_blk0, LANES), kvspec(SEQ, 0, V_WIDTH),
                kvspec(CTX_LEN, ctx_blk0, V_WIDTH)]
    args = [q, q, k, k, v, v]
    out_specs = [pl.BlockSpec((SEQ, LANES), lambda b, p: (b, p))]
    out_shape = [jax.ShapeDtypeStruct((N_LAT, n_units * LANES), BF16)]
    if ctx_queries:
        in_specs += [qspec(0, CTX_LEN, ctx_blk0), qspec(1, CTX_LEN, ctx_blk0)]
        args += [q, q]
        out_specs.append(pl.BlockSpec((CTX_LEN, LANES), lambda b, p: (b, p)))
        out_shape.append(jax.ShapeDtypeStruct((N_CTX, n_units * LANES), BF16))
    if cast_src is not None:
        rows, cols = cast_src.shape
        slab = pl.BlockSpec((rows // (BATCH * n_units), cols), lambda b, p: (b * n_units + p, 0))
        in_specs.append(slab)
        args.append(cast_src)
        out_specs.append(slab)
        out_shape.append(jax.ShapeDtypeStruct((rows, cols), BF16))
    outs = list(pl.pallas_call(
        functools.partial(_attn_kernel, ctx_queries=ctx_queries, cast=cast_src is not None),
        grid=(BATCH, n_units),
        in_specs=in_specs, out_specs=out_specs, out_shape=out_shape,
        compiler_params=_params("parallel", "parallel"),
        name="attention",
    )(*args))
    o_lat = outs.pop(0)
    o_ctx = outs.pop(0) if ctx_queries else None
    return o_lat, o_ctx, (outs.pop(0) if cast_src is not None else None)


_A_HEADS = lambda p: (2 * p, 2 * p + 1)
_A_KV = lambda p: p
_B_HEADS = lambda p: (p, p)
_B_KV = lambda p: p // 2


def _post_body(x, oa, ob, mod_ref, gn_ref, wo_ref):
    y = (jnp.dot(oa, wo_ref[0:512, :], preferred_element_type=F32)
         + jnp.dot(ob, wo_ref[512:1024, :], preferred_element_type=F32))
    x1 = x + mod_ref[0, 2:3, :] * y
    r = lax.rsqrt(jnp.mean(x1 * x1, axis=-1, keepdims=True) + NORM_EPS)
    h2 = (x1 * r * gn_ref[...]) * (1.0 + mod_ref[0, 4:5, :]) + mod_ref[0, 3:4, :]
    return x1, h2


def _post_kernel(xl_ref, xc_ref, oa_ref, oac_ref, ob_ref, obc_ref, mod_ref, gn_ref, wo_ref,
                 x1_ref, h2_ref):
    x1, h2 = _post_body(_tile_rows(xl_ref, xc_ref), _tile_rows(oa_ref, oac_ref),
                        _tile_rows(ob_ref, obc_ref), mod_ref, gn_ref, wo_ref)
    x1_ref[...] = x1
    h2_ref[...] = h2.astype(BF16)


def _post_router_kernel(x_ref, oa_ref, ob_ref, mod_ref, gn_ref, wo_ref, wr_ref,
                        x1_ref, hrow_ref, route_ref, count_ref, carry_ref):
    x1, h2 = _post_body(x_ref[...], oa_ref[...], ob_ref[...], mod_ref, gn_ref, wo_ref)
    x1_ref[...] = x1
    for s in range(ROW_SUB):
        hrow_ref[pl.ds(s, TM, stride=ROW_SUB), :] = h2[:, s * LANES:(s + 1) * LANES]

    h_hi = h2.astype(BF16)
    h_lo = (h2 - h_hi.astype(F32)).astype(BF16)
    part = jnp.dot(h_hi, wr_ref[...], preferred_element_type=F32)
    logits = (part[:, 0:LANES] + part[:, LANES:2 * LANES]
              + jnp.dot(h_lo, wr_ref[:, 0:LANES], preferred_element_type=F32))
    lane = lax.broadcasted_iota(jnp.int32, logits.shape, 1)
    valid = lane < N_EXPERTS
    lg = jnp.where(valid, logits, -jnp.inf)
    e = jnp.exp(lg - jnp.max(lg, axis=-1, keepdims=True))
    probs = e / jnp.sum(e, axis=-1, keepdims=True)
    v1 = jnp.max(probs, axis=-1, keepdims=True)
    i1 = jnp.min(jnp.where(probs == v1, lane, LANES), axis=-1, keepdims=True)
    rest = jnp.where(valid & (lane != i1), probs, -1.0)
    v2 = jnp.max(rest, axis=-1, keepdims=True)
    i2 = jnp.min(jnp.where(rest == v2, lane, LANES), axis=-1, keepdims=True)

    @pl.when(pl.program_id(0) == 0)
    def _():
        carry_ref[...] = jnp.zeros_like(carry_ref)

    pick1 = lane == i1
    pick2 = lane == i2
    onehot = jnp.where(pick1, 1.0, 0.0) + jnp.where(pick2, 1.0, 0.0)
    row = lax.broadcasted_iota(jnp.int32, (TM, TM), 0)
    col = lax.broadcasted_iota(jnp.int32, (TM, TM), 1)
    before = jnp.where(col < row, 1.0, 0.0).astype(BF16)
    seen = jnp.dot(before, onehot.astype(BF16), preferred_element_type=F32) + carry_ref[0:1, :]
    rank1 = jnp.sum(jnp.where(pick1, seen, 0.0), axis=-1, keepdims=True)
    rank2 = jnp.sum(jnp.where(pick2, seen, 0.0), axis=-1, keepdims=True)
    carry_ref[...] = carry_ref[...] + jnp.sum(onehot, axis=0, keepdims=True)
    count_ref[...] = carry_ref[...]

    fields = (v1 / (v1 + v2), v2 / (v1 + v2), i1.astype(F32), i2.astype(F32), rank1, rank2)
    route = jnp.zeros((TM, LANES), F32)
    for k, val in enumerate(fields):
        route = jnp.where(lane == k, val, route)
    route_ref[...] = route


def _post_attention(x_lat, x_ctx, o_lat, o_ctx, mod3, gn, w_out, w_router):
    const2 = lambda i: (0, 0)
    n_tiles = LAT_TILES if o_ctx is None else TOK_TILES
    rows = n_tiles * TM
    tail_specs = [pl.BlockSpec((1, 6, D_MODEL), lambda i: (_mod_index(i), 0, 0)),
                  pl.BlockSpec((1, D_MODEL), const2),
                  pl.BlockSpec((D_MODEL, D_MODEL), const2)]
    x1_spec = pl.BlockSpec((TM, D_MODEL), lambda i: (i, 0))
    x1_shape = jax.ShapeDtypeStruct((rows, D_MODEL), F32)
    if o_ctx is not None:
        return pl.pallas_call(
            _post_kernel, grid=(n_tiles,),
            in_specs=(_lat_ctx_specs(D_MODEL, 0) + _lat_ctx_specs(512, 0) + _lat_ctx_specs(512, 0)
                      + tail_specs),
            out_specs=[x1_spec, x1_spec],
            out_shape=[x1_shape, jax.ShapeDtypeStruct((rows, D_MODEL), BF16)],
            compiler_params=_params("parallel"), name="post_attention",
        )(x_lat, x_ctx, o_lat[0], o_ctx[0], o_lat[1], o_ctx[1], mod3, gn, w_out)
    half_spec = pl.BlockSpec((TM, 512), lambda i: (i, 0))
    in_specs = [x1_spec, half_spec, half_spec] + tail_specs
    args = [x_lat, *o_lat, mod3, gn, w_out]
    return pl.pallas_call(
        _post_router_kernel, grid=(n_tiles,),
        in_specs=in_specs + [pl.BlockSpec((D_MODEL, 2 * LANES), const2)],
        out_specs=[x1_spec,
                   pl.BlockSpec((TM * ROW_SUB, LANES), lambda i: (i, 0)),
                   pl.BlockSpec((TM, LANES), lambda i: (i, 0)),
                   pl.BlockSpec((8, LANES), const2)],
        out_shape=[x1_shape,
                   jax.ShapeDtypeStruct((rows * ROW_SUB, LANES), F32),
                   jax.ShapeDtypeStruct((rows, LANES), F32),
                   jax.ShapeDtypeStruct((8, LANES), F32)],
        scratch_shapes=[pltpu.VMEM((8, LANES), F32)],
        compiler_params=_params("arbitrary"), name="post_attention_router",
    )(*args, w_router)


def _swiglu(h, wg_ref, wu_ref, wd_ref):
    y = None
    for lo, hi in zip(FF_SPLITS[:-1], FF_SPLITS[1:]):
        g = jnp.dot(h, wg_ref[:, lo:hi], preferred_element_type=F32)
        u = jnp.dot(h, wu_ref[:, lo:hi], preferred_element_type=F32)
        yc = jnp.dot((_silu(g) * u).astype(BF16), wd_ref[lo:hi, :], preferred_element_type=F32)
        y = yc if y is None else y + yc
    return y


def _ffn_kernel(h_ref, x1_ref, mod_ref, wg_ref, wu_ref, wd_ref, o_ref):
    y = _swiglu(h_ref[...], wg_ref, wu_ref, wd_ref)
    o_ref[...] = x1_ref[...] + mod_ref[0, 5:6, :] * y


def _dense_ffn(h2, x1, mod3, wg, wu, wd):
    resident = lambda shape: pl.BlockSpec(shape, lambda i: (0, 0), pipeline_mode=pl.Buffered(1))
    return pl.pallas_call(
        _ffn_kernel,
        grid=(TOK_TILES,),
        in_specs=[pl.BlockSpec((TM, D_MODEL), lambda i: (i, 0)),
                  pl.BlockSpec((TM, D_MODEL), lambda i: (i, 0)),
                  pl.BlockSpec((1, 6, D_MODEL), lambda i: (_mod_index(i), 0, 0)),
                  resident((D_MODEL, DENSE_FF)), resident((D_MODEL, DENSE_FF)),
                  resident((DENSE_FF, D_MODEL))],
        out_specs=pl.BlockSpec((TM, D_MODEL), lambda i: (i, 0)),
        out_shape=jax.ShapeDtypeStruct((N_TOK, D_MODEL), F32),
        compiler_params=_params("parallel"),
        name="dense_ffn",
    )(h2, x1, mod3, wg, wu, wd)


def _row(ref, r):
    return ref.at[pl.ds(pl.multiple_of(r * ROW_SUB, ROW_SUB), ROW_SUB)]


def _rows_to_matrix(ref, n):
    return jnp.concatenate([ref[pl.ds(s, n, stride=ROW_SUB), :] for s in range(ROW_SUB)], axis=-1)


def _invert_kernel(dest_ref, inv_ref):
    def clear(r, carry):
        inv_ref[r] = -1
        return carry

    def place(a, carry):
        inv_ref[dest_ref[a]] = a
        return carry

    lax.fori_loop(0, SORT_ROWS, clear, 0, unroll=16)
    lax.fori_loop(0, 2 * N_LAT, place, 0, unroll=16)


def _invert(dest_flat):
    return pl.pallas_call(
        _invert_kernel,
        in_specs=[pl.BlockSpec(memory_space=pltpu.SMEM)],
        out_specs=pl.BlockSpec(memory_space=pltpu.SMEM),
        out_shape=jax.ShapeDtypeStruct((SORT_ROWS,), jnp.int32),
        name="moe_invert",
    )(dest_flat)


def _expert_kernel(te_ref, nu_ref, inv_ref, h_ref, wg_ref, wu_ref, wd_ref, y2_ref,
                   xbuf, ybuf, sem_g, sem_s):
    del te_ref
    t = pl.program_id(0)
    n_used = nu_ref[0]
    slot = lax.rem(t, 2)
    other = 1 - slot

    def gather_rows(tile, s):
        base = tile * TMX
        for j in range(TMX):
            token = lax.shift_right_logical(jnp.maximum(inv_ref[base + j], 0), 1)
            pltpu.make_async_copy(_row(h_ref, token), _row(xbuf.at[s], j), sem_g.at[s]).start()

    def scatter_rows(tile, s, real, scratch_block):
        base = tile * TMX
        for j in range(TMX):
            code = inv_ref[base + j]
            row = jnp.where(jnp.logical_and(code >= 0, real),
                            (code & 1) * N_LAT + lax.shift_right_logical(code, 1),
                            2 * N_LAT + scratch_block * TMX + j)
            pltpu.make_async_copy(_row(ybuf.at[s], j), _row(y2_ref, row), sem_s.at[s]).start()

    def wait_gather(s):
        pltpu.make_async_copy(h_ref.at[pl.ds(0, TMX * ROW_SUB)], xbuf.at[s], sem_g.at[s]).wait()

    def wait_scatter(s):
        pltpu.make_async_copy(ybuf.at[s], y2_ref.at[pl.ds(0, TMX * ROW_SUB)], sem_s.at[s]).wait()

    @pl.when(t == 0)
    def _():
        ybuf[...] = jnp.zeros_like(ybuf)
        gather_rows(0, 0)

    @pl.when(t < n_used)
    def _():
        wait_gather(slot)
        x = _rows_to_matrix(xbuf.at[slot], TMX).astype(BF16)
        gather_rows(jnp.minimum(t + 1, n_used - 1), other)
        scatter_rows(jnp.maximum(t - 1, 0), other, t > 0, other)
        y = _swiglu(x, wg_ref.at[0], wu_ref.at[0], wd_ref.at[0])

        @pl.when(t > 0)
        def _():
            wait_scatter(slot)

        for s in range(ROW_SUB):
            ybuf[slot, pl.ds(s, TMX, stride=ROW_SUB), :] = y[:, s * LANES:(s + 1) * LANES]

    @pl.when(t == n_used)
    def _():
        wait_gather(slot)
        scatter_rows(n_used - 1, other, True, other)
        wait_scatter(other)
        wait_scatter(slot)
        scatter_rows(n_used - 1, slot, False, 0)
        wait_scatter(slot)


def _experts(tile_expert, n_used, inv, hrow, wg, wu, wd):
    one = pl.Buffered(1)
    wspec = lambda shape: pl.BlockSpec(shape, lambda t, te, nu, iv: (te[t], 0, 0), pipeline_mode=one)
    grid_spec = pltpu.PrefetchScalarGridSpec(
        num_scalar_prefetch=3,
        grid=(SORT_TILES + 1,),
        in_specs=[pl.BlockSpec(memory_space=pl.ANY),
                  wspec((1, D_MODEL, EXPERT_FF)), wspec((1, D_MODEL, EXPERT_FF)),
                  wspec((1, EXPERT_FF, D_MODEL))],
        out_specs=pl.BlockSpec(memory_space=pl.ANY),
        scratch_shapes=[pltpu.VMEM((2, TMX * ROW_SUB, LANES), F32),
                        pltpu.VMEM((2, TMX * ROW_SUB, LANES), F32),
                        pltpu.SemaphoreType.DMA((2,)), pltpu.SemaphoreType.DMA((2,))])
    return pl.pallas_call(
        _expert_kernel,
        grid_spec=grid_spec,
        out_shape=jax.ShapeDtypeStruct((Y2_ROWS * ROW_SUB, LANES), F32),
        compiler_params=_params("arbitrary"),
        name="moe_experts",
    )(tile_expert, n_used, inv, hrow, wg, wu, wd)


def _combine_kernel(ya_ref, yb_ref, x1_ref, route_ref, mod_ref, o_ref):
    route = route_ref[...]
    y = (route[:, 0:1] * _rows_to_matrix(ya_ref, TM) + route[:, 1:2] * _rows_to_matrix(yb_ref, TM))
    o_ref[...] = x1_ref[...] + mod_ref[0, 5:6, :] * y


def _combine(y2, x1, route, mod3):
    return pl.pallas_call(
        _combine_kernel,
        grid=(LAT_TILES,),
        in_specs=[pl.BlockSpec((TM * ROW_SUB, LANES), lambda i: (i, 0)),
                  pl.BlockSpec((TM * ROW_SUB, LANES), lambda i: (LAT_TILES + i, 0)),
                  pl.BlockSpec((TM, D_MODEL), lambda i: (i, 0)),
                  pl.BlockSpec((TM, LANES), lambda i: (i, 0)),
                  pl.BlockSpec((1, 6, D_MODEL), lambda i: (i // SEQ_TILES, 0, 0))],
        out_specs=pl.BlockSpec((TM, D_MODEL), lambda i: (i, 0)),
        out_shape=jax.ShapeDtypeStruct((N_LAT, D_MODEL), F32),
        compiler_params=_params("parallel"),
        name="moe_combine",
    )(y2, y2, x1, route, mod3)


def _moe_ffn(hrow, x1, route, counts, mod3, wg, wu, wd):
    expert = route[:, 2:4].astype(jnp.int32)
    rank = route[:, 4:6].astype(jnp.int32)
    tiles = (counts[0, :N_EXPERTS].astype(jnp.int32) + TMX - 1) // TMX
    tile_end = jnp.cumsum(tiles)
    dest = (tile_end - tiles)[expert] * TMX + rank
    inv = _invert(dest.reshape(-1))
    n_used = tile_end[N_EXPERTS - 1:]
    tile_ids = jnp.minimum(jnp.arange(SORT_TILES + 1), n_used[0] - 1)
    tile_expert = jnp.sum(tile_ids[:, None] >= tile_end[None, :], axis=1).astype(jnp.int32)
    y2 = _experts(tile_expert, n_used.astype(jnp.int32), inv, hrow, wg, wu, wd)
    return _combine(y2, x1, route, mod3)


def _partner(half):
    idx = np.arange(4 * half)
    return np.where((idx // half) % 2 == 0, idx + half, idx - half)


_PARTNER_A = np.concatenate([np.arange(MLA_NOPE), MLA_NOPE + _partner(MLA_ROPE // 4),
                             np.arange(MLA_QK, LANES)])
_PARTNER_B = np.concatenate([_partner(GQA_HEAD_DIM // 4), GQA_HEAD_DIM + _partner(GQA_HEAD_DIM // 4)])


def _rope_tables():
    t = jnp.arange(SEQ)
    row = (t // GRID_W).astype(F32)
    col = (t % GRID_W).astype(F32)

    def one_axis(pos, half):
        freqs = ROPE_THETA ** (-jnp.arange(half, dtype=F32) / half)
        ang = pos[:, None] * freqs[None, :]
        cos, sin = jnp.cos(ang), jnp.sin(ang)
        return jnp.concatenate([cos, cos], -1), jnp.concatenate([-sin, sin], -1)

    def two_axes(half):
        r, c = one_axis(row, half), one_axis(col, half)
        return [jnp.concatenate([a, b], -1) for a, b in zip(r, c)]

    def pad_a(tbl, fill):
        return jnp.concatenate([jnp.full((SEQ, MLA_NOPE), fill, F32), tbl,
                                jnp.full((SEQ, LANES - MLA_QK), fill, F32)], -1)

    ta = two_axes(MLA_ROPE // 4)
    tb = [jnp.concatenate([x, x], -1) for x in two_axes(GQA_HEAD_DIM // 4)]
    tabs = [pad_a(ta[0], 1.0), pad_a(ta[1], 0.0)] + tb
    ident = [jnp.ones((TM, LANES), F32), jnp.zeros((TM, LANES), F32)]
    return jnp.stack([jnp.concatenate([tbl, idn], 0) for tbl, idn in zip(tabs, ident + ident)])


def _layer_weights(w_in, q_lat_norm, w_uq, kv_lat_norm, w_ukv, mla_q_gain, mla_k_gain,
                   gqa_q_gain, gqa_k_gain):
    split = MLA_Q_RANK + MLA_KV_RANK + MLA_ROPE
    perm_a = (np.arange(MLA_HEADS)[:, None] * LANES + _PARTNER_A[None, :]).reshape(-1)
    perm_b = (np.arange(GQA_HEADS // 2)[:, None] * LANES + _PARTNER_B[None, :]).reshape(-1)
    w_qb = w_in[:, split:split + GQA_HEADS * GQA_HEAD_DIM]
    w_kb = w_in[:, split + 512:split + 640]
    w_in_p = jnp.concatenate([w_in[:, :split], jnp.zeros((D_MODEL, 512 - split), F32),
                              w_in[:, split:], w_qb[:, perm_b], w_kb[:, _PARTNER_B]],
                             axis=1).astype(BF16)
    w_uq_p = jnp.pad(w_uq.reshape(MLA_Q_RANK, MLA_HEADS, MLA_QK),
                     ((0, 0), (0, 0), (0, LANES - MLA_QK))).reshape(MLA_Q_RANK, MLA_HEADS * LANES)
    w_uq_p = jnp.concatenate([w_uq_p, w_uq_p[:, perm_a]], axis=1)
    ukv = w_ukv.reshape(MLA_KV_RANK, MLA_HEADS, MLA_NOPE + MLA_V)
    w_k = jnp.pad(ukv[:, :, :MLA_NOPE], ((0, 0), (0, 0), (0, LANES - MLA_NOPE)))
    place = jnp.pad(jnp.eye(MLA_ROPE, dtype=F32), ((0, 0), (MLA_NOPE, LANES - MLA_QK)))
    place = jnp.broadcast_to(place[:, None, :], (MLA_ROPE, MLA_HEADS, LANES))
    w_k = jnp.concatenate([w_k, place,
                           jnp.zeros((256 - MLA_KV_RANK - MLA_ROPE, MLA_HEADS, LANES), F32)], 0)
    w_k = w_k.reshape(256, MLA_HEADS * LANES)
    w_v = jnp.pad(ukv[:, :, MLA_NOPE:].reshape(MLA_KV_RANK, MLA_HEADS * MLA_V),
                  ((0, 256 - MLA_KV_RANK), (0, 0)))
    w_kv = jnp.concatenate([w_k, w_k[:, perm_a], w_v], axis=1)
    glat = jnp.stack([q_lat_norm, jnp.pad(kv_lat_norm, (0, 256 - MLA_KV_RANK))])
    pad_qk = lambda g: jnp.pad(g, (0, LANES - MLA_QK))
    gains = [pad_qk(mla_q_gain) * (MLA_SCALE * LOG2E), pad_qk(mla_k_gain),
             jnp.tile(gqa_q_gain, 2) * (GQA_SCALE * LOG2E), jnp.tile(gqa_k_gain, 2)]
    partners = [_PARTNER_A, _PARTNER_A, _PARTNER_B, _PARTNER_B]
    g128 = jnp.stack([v for g, pm in zip(gains, partners) for v in (g, g[pm])])
    return w_in_p, glat, w_uq_p.astype(BF16), w_kv.astype(BF16), g128


def _mixer(x_lat, x_ctx, ctx_tile0, mod3, tab, norm_attn, w_in, q_lat_norm, w_uq, kv_lat_norm,
           w_ukv, mla_q_gain, mla_k_gain, gqa_q_gain, gqa_k_gain, with_ctx_queries, cast_a, cast_b):
    w_in_p, glat, w_uq_p, w_kv, g128 = _layer_weights(
        w_in, q_lat_norm, w_uq, kv_lat_norm, w_ukv, mla_q_gain, mla_k_gain, gqa_q_gain, gqa_k_gain)
    qa, ka, va, qb, kb, vb = _projection(x_lat, x_ctx, ctx_tile0, mod3,
                                         norm_attn.reshape(1, D_MODEL), w_in_p, glat,
                                         w_uq_p, w_kv, g128, tab)
    oa, oa_ctx, cast_a = _attention(qa, ka, va, _A_HEADS, _A_KV, with_ctx_queries, cast_a)
    ob, ob_ctx, cast_b = _attention(qb, kb, vb, _B_HEADS, _B_KV, with_ctx_queries, cast_b)
    return (oa, ob), ((oa_ctx, ob_ctx) if with_ctx_queries else None), (cast_a, cast_b)


def kernel(x, c, ctx, c_ctx, l0_w_mod, l0_b_mod, l0_norm_attn, l0_w_in, l0_q_lat_norm, l0_w_uq, l0_kv_lat_norm, l0_w_ukv, l0_mla_q_gain, l0_mla_k_gain, l0_gqa_q_gain, l0_gqa_k_gain, l0_w_out, l0_norm_ffn, l0_ffn_w_gate, l0_ffn_w_up, l0_ffn_w_down, l1_w_mod, l1_b_mod, l1_norm_attn, l1_w_in, l1_q_lat_norm, l1_w_uq, l1_kv_lat_norm, l1_w_ukv, l1_mla_q_gain, l1_mla_k_gain, l1_gqa_q_gain, l1_gqa_k_gain, l1_w_out, l1_norm_ffn, l1_router, l1_exp_w_gate, l1_exp_w_up, l1_exp_w_down):
    x_lat = x.reshape(N_LAT, D_MODEL)
    x_ctx = ctx.reshape(N_CTX, D_MODEL)
    cc = jnp.concatenate([c, c_ctx[None, :], jnp.zeros((16 - BATCH - 1, D_MODEL), F32)], axis=0)
    tab = _rope_tables()

    mod3 = _modulation(cc, l0_w_mod, l0_b_mod)
    o_lat, o_ctx, (wg_x, wu_x) = _mixer(
        x_lat, x_ctx, 0, mod3, tab, l0_norm_attn, l0_w_in, l0_q_lat_norm, l0_w_uq, l0_kv_lat_norm,
        l0_w_ukv, l0_mla_q_gain, l0_mla_k_gain, l0_gqa_q_gain, l0_gqa_k_gain, True,
        l1_exp_w_gate.reshape(N_EXPERTS * D_MODEL, EXPERT_FF),
        l1_exp_w_up.reshape(N_EXPERTS * D_MODEL, EXPERT_FF))
    x1, h2 = _post_attention(x_lat, x_ctx, o_lat, o_ctx, mod3, l0_norm_ffn.reshape(1, D_MODEL),
                             l0_w_out.astype(BF16), None)
    xall = _dense_ffn(h2, x1, mod3, l0_ffn_w_gate.astype(BF16), l0_ffn_w_up.astype(BF16),
                      l0_ffn_w_down.astype(BF16))

    mod3 = _modulation(cc, l1_w_mod, l1_b_mod)
    o_lat, _, (wd_x, _) = _mixer(
        xall, xall, LAT_TILES, mod3, tab, l1_norm_attn, l1_w_in, l1_q_lat_norm, l1_w_uq,
        l1_kv_lat_norm, l1_w_ukv, l1_mla_q_gain, l1_mla_k_gain, l1_gqa_q_gain, l1_gqa_k_gain,
        False, l1_exp_w_down.reshape(N_EXPERTS * EXPERT_FF, D_MODEL), None)
    w_router = jnp.pad(l1_router, ((0, 0), (0, LANES - N_EXPERTS)))
    w_router_hi = w_router.astype(BF16)
    w_router = jnp.concatenate([w_router_hi, (w_router - w_router_hi.astype(F32)).astype(BF16)],
                               axis=1)
    x1, hrow, route, counts = _post_attention(xall, None, o_lat, None, mod3,
                                              l1_norm_ffn.reshape(1, D_MODEL),
                                              l1_w_out.astype(BF16), w_router)
    out = _moe_ffn(hrow, x1, route, counts, mod3,
                   wg_x.reshape(N_EXPERTS, D_MODEL, EXPERT_FF),
                   wu_x.reshape(N_EXPERTS, D_MODEL, EXPERT_FF),
                   wd_x.reshape(N_EXPERTS, EXPERT_FF, D_MODEL))
    return out.reshape(BATCH, SEQ, D_MODEL)
```

```python
import functools
import math

import jax
import jax.numpy as jnp
import numpy as np
from jax import lax
from jax.experimental import pallas as pl
from jax.experimental.pallas import tpu as pltpu

D_MODEL = 1024
BATCH = 8
SEQ = 2048
CTX_LEN = 256
GRID_W = 64
MLA_HEADS = 8
MLA_NOPE = 64
MLA_ROPE = 32
MLA_V = 64
MLA_QK = MLA_NOPE + MLA_ROPE
MLA_Q_RANK = 256
MLA_KV_RANK = 128
GQA_HEADS = 8
GQA_KV_HEADS = 2
GQA_HEAD_DIM = 64
DENSE_FF = 2816
N_EXPERTS = 8
EXPERT_FF = 2816
ROPE_THETA = 10000.0
NORM_EPS = 1e-6
LOG2E = math.log2(math.e)
MLA_SCALE = MLA_QK ** -0.5
GQA_SCALE = GQA_HEAD_DIM ** -0.5

LANES = 128
N_LAT = BATCH * SEQ
N_CTX = BATCH * CTX_LEN
N_TOK = N_LAT + N_CTX
TM = 512
LAT_TILES = N_LAT // TM
TOK_TILES = N_TOK // TM
SEQ_TILES = SEQ // TM
Q_CHUNK = 256
V_WIDTH = 2 * LANES
IN_PAD = 1920
FF_SPLITS = (0, 1536, 2816)
ROW_SUB = D_MODEL // LANES
TMX = 512
SORT_TILES = 2 * N_LAT // TMX + N_EXPERTS
SORT_ROWS = SORT_TILES * TMX
Y2_ROWS = 2 * N_LAT + 2 * TMX
VMEM_LIMIT = 56 * 1024 * 1024

F32 = jnp.float32
BF16 = jnp.bfloat16


def _silu(x):
    return x / (1.0 + jnp.exp(-x))


def _params(*sem):
    return pltpu.CompilerParams(dimension_semantics=sem, vmem_limit_bytes=VMEM_LIMIT)


def _mod_kernel(c_ref, w_ref, b_ref, o_ref):
    s = _silu(c_ref[...])
    o_ref[...] = jnp.dot(s, w_ref[...], precision=lax.Precision.HIGHEST,
                         preferred_element_type=F32) + b_ref[...]


def _modulation(cc, w_mod, b_mod):
    n = w_mod.shape[1]
    bn = 1024
    out = pl.pallas_call(
        _mod_kernel,
        grid=(n // bn,),
        in_specs=[pl.BlockSpec((16, D_MODEL), lambda j: (0, 0)),
                  pl.BlockSpec((D_MODEL, bn), lambda j: (0, j)),
                  pl.BlockSpec((1, bn), lambda j: (0, j))],
        out_specs=pl.BlockSpec((16, bn), lambda j: (0, j)),
        out_shape=jax.ShapeDtypeStruct((16, n), F32),
        compiler_params=_params("parallel"),
        name="modulation",
    )(cc, w_mod, b_mod.reshape(1, n))
    return out.reshape(16, 6, D_MODEL)


def _tile_rows(lat_ref, ctx_ref):
    return jnp.where(pl.program_id(0) < LAT_TILES, lat_ref[...], ctx_ref[...])


def _proj_kernel(xl_ref, xc_ref, mod_ref, gn_ref, win_ref, glat_ref, wuq_ref, wkv_ref, g128_ref,
                 tab_ref, qa_ref, ka_ref, va_ref, qb_ref, kb_ref, vb_ref):
    x = _tile_rows(xl_ref, xc_ref)
    r = lax.rsqrt(jnp.mean(x * x, axis=-1, keepdims=True) + NORM_EPS)
    h = (x * r * gn_ref[...]) * (1.0 + mod_ref[0, 1:2, :]) + mod_ref[0, 0:1, :]
    p = jnp.dot(h.astype(BF16), win_ref[...], preferred_element_type=F32)

    lane = lax.broadcasted_iota(jnp.int32, (TM, LANES), 1)
    lo = lane < 64
    ones_col = jnp.where(lane == 0, 1.0, 0.0).astype(BF16)

    def tables(base, row):
        return tab_ref[base] * g128_ref[row:row + 1, :], tab_ref[base + 1] * g128_ref[row + 1:row + 2, :]

    cq = p[:, 0:256]
    rq = lax.rsqrt(jnp.mean(cq * cq, axis=-1, keepdims=True) + NORM_EPS)
    qa = jnp.dot((cq * rq * glat_ref[0:1, :]).astype(BF16), wuq_ref[...],
                 preferred_element_type=F32)
    ckv = p[:, 256:384]
    rkv = lax.rsqrt(jnp.mean(ckv * ckv, axis=-1, keepdims=True) + NORM_EPS)
    slab = jnp.concatenate([ckv * rkv * glat_ref[1:2, 0:128], p[:, 384:512]], axis=-1)
    kv = jnp.dot(slab.astype(BF16), wkv_ref[...], preferred_element_type=F32)

    def head_a(src, hd, cos_g, sin_g):
        blk = src[:, hd * LANES:(hd + 1) * LANES]
        partner = src[:, 1024 + hd * LANES:1024 + (hd + 1) * LANES]
        rr = lax.rsqrt(jnp.sum(blk * blk, axis=-1, keepdims=True) * (1.0 / MLA_QK) + NORM_EPS)
        return ((blk * cos_g + partner * sin_g) * rr).astype(BF16)

    cq_g, sq_g = tables(0, 0)
    ck_g, sk_g = tables(0, 2)
    for hd in range(MLA_HEADS):
        qa_ref[hd] = head_a(qa, hd, cq_g, sq_g)
        ka_ref[hd] = head_a(kv, hd, ck_g, sk_g)
    for pr in range(MLA_HEADS // 2):
        blk = kv[:, 2048 + pr * LANES:2048 + (pr + 1) * LANES]
        va_ref[2 * pr, :, 0:LANES] = jnp.where(lo, blk, 0.0).astype(BF16)
        va_ref[2 * pr + 1, :, 0:LANES] = jnp.where(lo, 0.0, blk).astype(BF16)
    for hd in range(MLA_HEADS):
        va_ref[hd, :, LANES:V_WIDTH] = ones_col

    def pair_b(blk, partner, cos_g, sin_g):
        sq = blk * blk
        s_lo = jnp.sum(jnp.where(lo, sq, 0.0), axis=-1, keepdims=True)
        s_hi = jnp.sum(jnp.where(lo, 0.0, sq), axis=-1, keepdims=True)
        rr = jnp.where(lo, lax.rsqrt(s_lo * (1.0 / GQA_HEAD_DIM) + NORM_EPS),
                       lax.rsqrt(s_hi * (1.0 / GQA_HEAD_DIM) + NORM_EPS))
        return (blk * cos_g + partner * sin_g) * rr

    cq_g, sq_g = tables(2, 4)
    ck_g, sk_g = tables(2, 6)
    for pr in range(GQA_HEADS // 2):
        blk = p[:, 512 + pr * LANES:512 + (pr + 1) * LANES]
        partner = p[:, 1280 + pr * LANES:1280 + (pr + 1) * LANES]
        qb_ref[pr] = pair_b(blk, partner, cq_g, sq_g).astype(BF16)
    kb = pair_b(p[:, 1024:1152], p[:, 1792:1920], ck_g, sk_g)
    kb_sw = pltpu.roll(kb, 64, 1)
    kb_ref[0] = jnp.where(lo, kb, 0.0).astype(BF16)
    kb_ref[1] = jnp.where(lo, 0.0, kb_sw).astype(BF16)
    kb_ref[2] = jnp.where(lo, kb_sw, 0.0).astype(BF16)
    kb_ref[3] = jnp.where(lo, 0.0, kb).astype(BF16)
    vb = p[:, 1152:1280]
    vb_sw = pltpu.roll(vb, 64, 1)
    vb_ref[0, :, 0:LANES] = jnp.where(lo, vb, 0.0).astype(BF16)
    vb_ref[1, :, 0:LANES] = jnp.where(lo, 0.0, vb_sw).astype(BF16)
    vb_ref[2, :, 0:LANES] = jnp.where(lo, vb_sw, 0.0).astype(BF16)
    vb_ref[3, :, 0:LANES] = jnp.where(lo, 0.0, vb).astype(BF16)
    for j in range(2 * GQA_KV_HEADS):
        vb_ref[j, :, LANES:V_WIDTH] = ones_col


def _mod_index(i):
    return jnp.where(i < LAT_TILES, i // SEQ_TILES, BATCH)


def _lat_ctx_specs(width, ctx_tile0):
    return [pl.BlockSpec((TM, width), lambda i: (jnp.minimum(i, LAT_TILES - 1), 0)),
            pl.BlockSpec((TM, width), lambda i: (ctx_tile0 + jnp.maximum(i - LAT_TILES, 0), 0))]


def _projection(x_lat, x_ctx, ctx_tile0, mod3, gn, w_in, glat, w_uq, w_kv, g128, tab):
    const2 = lambda i: (0, 0)
    head_out = lambda n, w: pl.BlockSpec((n, TM, w), lambda i: (0, i, 0))
    head_shape = lambda n, w: jax.ShapeDtypeStruct((n, N_TOK, w), BF16)
    return pl.pallas_call(
        _proj_kernel,
        grid=(TOK_TILES,),
        in_specs=_lat_ctx_specs(D_MODEL, ctx_tile0) + [
                  pl.BlockSpec((1, 6, D_MODEL), lambda i: (_mod_index(i), 0, 0)),
                  pl.BlockSpec((1, D_MODEL), const2),
                  pl.BlockSpec((D_MODEL, IN_PAD), const2),
                  pl.BlockSpec((2, 256), const2),
                  pl.BlockSpec((MLA_Q_RANK, 2 * MLA_HEADS * LANES), const2),
                  pl.BlockSpec((256, 2 * MLA_HEADS * LANES + MLA_HEADS * MLA_V), const2),
                  pl.BlockSpec((8, LANES), const2),
                  pl.BlockSpec((4, TM, LANES),
                               lambda i: (0, jnp.where(i < LAT_TILES, i % SEQ_TILES, SEQ_TILES), 0))],
        out_specs=[head_out(8, LANES), head_out(8, LANES), head_out(8, V_WIDTH),
                   head_out(4, LANES), head_out(4, LANES), head_out(4, V_WIDTH)],
        out_shape=[head_shape(8, LANES), head_shape(8, LANES), head_shape(8, V_WIDTH),
                   head_shape(4, LANES), head_shape(4, LANES), head_shape(4, V_WIDTH)],
        compiler_params=_params("parallel"),
        name="projection",
    )(x_lat, x_ctx, mod3, gn, w_in, glat, w_uq, w_kv, g128, tab)


def _pair_rows(q_refs, r0, n, kv_refs):
    nt = (((1,), (1,)), ((), ()))
    out = None
    for u, q_ref in enumerate(q_refs):
        q = q_ref[r0:r0 + n, :]
        scores = [lax.dot_general(q, k_ref[u], nt, preferred_element_type=F32)
                  for k_ref, _ in kv_refs]
        m = functools.reduce(jnp.maximum, [jnp.max(sc, axis=-1, keepdims=True) for sc in scores])
        acc = None
        for sc, (_, v_ref) in zip(scores, kv_refs):
            part = jnp.dot(jnp.exp2((sc - m).astype(BF16)), v_ref[u], preferred_element_type=F32)
            acc = part if acc is None else acc + part
        o = acc[:, 0:LANES] / acc[:, LANES:LANES + 1]
        out = o if out is None else out + o
    return out


def _attn_kernel(*refs, ctx_queries, cast):
    q0_ref, q1_ref, kl_ref, kc_ref, vl_ref, vc_ref = refs[:6]
    rest = list(refs[6:])
    qc_refs = (rest.pop(0), rest.pop(0)) if ctx_queries else None
    src_ref = rest.pop(0) if cast else None
    o_ref = rest.pop(0)
    oc_ref = rest.pop(0) if ctx_queries else None
    if cast:
        rest.pop(0)[...] = src_ref[...].astype(BF16)
    for r0 in range(0, SEQ, Q_CHUNK):
        out = _pair_rows((q0_ref, q1_ref), r0, Q_CHUNK, ((kc_ref, vc_ref), (kl_ref, vl_ref)))
        o_ref[r0:r0 + Q_CHUNK, :] = out.astype(o_ref.dtype)
    if ctx_queries:
        oc_ref[...] = _pair_rows(qc_refs, 0, CTX_LEN, ((kc_ref, vc_ref),)).astype(oc_ref.dtype)


def _attention(q, k, v, q_heads, kv_pair, ctx_queries, cast_src):
    n_units = 4
    ctx_blk0 = N_LAT // CTX_LEN

    def qspec(which, rows, blk0):
        return pl.BlockSpec((None, rows, LANES), lambda b, p: (q_heads(p)[which], blk0 + b, 0))

    def kvspec(rows, blk0, width):
        return pl.BlockSpec((2, rows, width), lambda b, p: (kv_pair(p), blk0 + b, 0))

    in_specs = [qspec(0, SEQ, 0), qspec(1, SEQ, 0), kvspec(SEQ, 0, LANES),
                kvspec(CTX_LEN, ctx_blk0, LANES), kvspec(SEQ, 0, V_WIDTH),
                kvspec(CTX_LEN, ctx_blk0, V_WIDTH)]
    args = [q, q, k, k, v, v]
    out_specs = [pl.BlockSpec((SEQ, LANES), lambda b, p: (b, p))]
    out_shape = [jax.ShapeDtypeStruct((N_LAT, n_units * LANES), BF16)]
    if ctx_queries:
        in_specs += [qspec(0, CTX_LEN, ctx_blk0), qspec(1, CTX_LEN, ctx_blk0)]
        args += [q, q]
        out_specs.append(pl.BlockSpec((CTX_LEN, LANES), lambda b, p: (b, p)))
        out_shape.append(jax.ShapeDtypeStruct((N_CTX, n_units * LANES), BF16))
    if cast_src is not None:
        rows, cols = cast_src.shape
        slab = pl.BlockSpec((rows // (BATCH * n_units), cols), lambda b, p: (b * n_units + p, 0))
        in_specs.append(slab)
        args.append(cast_src)
        out_specs.append(slab)
        out_shape.append(jax.ShapeDtypeStruct((rows, cols), BF16))
    outs = list(pl.pallas_call(
        functools.partial(_attn_kernel, ctx_queries=ctx_queries, cast=cast_src is not None),
        grid=(BATCH, n_units),
        in_specs=in_specs, out_specs=out_specs, out_shape=out_shape,
        compiler_params=_params("parallel", "parallel"),
        name="attention",
    )(*args))
    o_lat = outs.pop(0)
    o_ctx = outs.pop(0) if ctx_queries else None
    return o_lat, o_ctx, (outs.pop(0) if cast_src is not None else None)


_A_HEADS = lambda p: (2 * p, 2 * p + 1)
_A_KV = lambda p: p
_B_HEADS = lambda p: (p, p)
_B_KV = lambda p: p // 2


def _post_body(x, oa, ob, mod_ref, gn_ref, wo_ref):
    y = (jnp.dot(oa, wo_ref[0:512, :], preferred_element_type=F32)
         + jnp.dot(ob, wo_ref[512:1024, :], preferred_element_type=F32))
    x1 = x + mod_ref[0, 2:3, :] * y
    r = lax.rsqrt(jnp.mean(x1 * x1, axis=-1, keepdims=True) + NORM_EPS)
    h2 = (x1 * r * gn_ref[...]) * (1.0 + mod_ref[0, 4:5, :]) + mod_ref[0, 3:4, :]
    return x1, h2


def _swiglu(h, wg_ref, wu_ref, wd_ref):
    y = None
    for lo, hi in zip(FF_SPLITS[:-1], FF_SPLITS[1:]):
        g = jnp.dot(h, wg_ref[:, lo:hi], preferred_element_type=F32)
        u = jnp.dot(h, wu_ref[:, lo:hi], preferred_element_type=F32)
        yc = jnp.dot((_silu(g) * u).astype(BF16), wd_ref[lo:hi, :], preferred_element_type=F32)
        y = yc if y is None else y + yc
    return y


def _post_ffn_kernel(xl_ref, xc_ref, oa_ref, oac_ref, ob_ref, obc_ref, mod_ref, gn_ref, wo_ref,
                     wg_ref, wu_ref, wd_ref, o_ref):
    x1, h2 = _post_body(_tile_rows(xl_ref, xc_ref), _tile_rows(oa_ref, oac_ref),
                        _tile_rows(ob_ref, obc_ref), mod_ref, gn_ref, wo_ref)
    y = _swiglu(h2.astype(BF16), wg_ref, wu_ref, wd_ref)
    o_ref[...] = x1 + mod_ref[0, 5:6, :] * y


def _post_router_kernel(x_ref, oa_ref, ob_ref, mod_ref, gn_ref, wo_ref, wr_ref,
                        x1_ref, hrow_ref, route_ref, count_ref, carry_ref):
    x1, h2 = _post_body(x_ref[...], oa_ref[...], ob_ref[...], mod_ref, gn_ref, wo_ref)
    x1_ref[...] = x1
    for s in range(ROW_SUB):
        hrow_ref[pl.ds(s, TM, stride=ROW_SUB), :] = h2[:, s * LANES:(s + 1) * LANES]

    h_hi = h2.astype(BF16)
    h_lo = (h2 - h_hi.astype(F32)).astype(BF16)
    part = jnp.dot(h_hi, wr_ref[...], preferred_element_type=F32)
    logits = (part[:, 0:LANES] + part[:, LANES:2 * LANES]
              + jnp.dot(h_lo, wr_ref[:, 0:LANES], preferred_element_type=F32))
    lane = lax.broadcasted_iota(jnp.int32, logits.shape, 1)
    valid = lane < N_EXPERTS
    lg = jnp.where(valid, logits, -jnp.inf)
    e = jnp.exp(lg - jnp.max(lg, axis=-1, keepdims=True))
    probs = e / jnp.sum(e, axis=-1, keepdims=True)
    v1 = jnp.max(probs, axis=-1, keepdims=True)
    i1 = jnp.min(jnp.where(probs == v1, lane, LANES), axis=-1, keepdims=True)
    rest = jnp.where(valid & (lane != i1), probs, -1.0)
    v2 = jnp.max(rest, axis=-1, keepdims=True)
    i2 = jnp.min(jnp.where(rest == v2, lane, LANES), axis=-1, keepdims=True)

    @pl.when(pl.program_id(0) == 0)
    def _():
        carry_ref[...] = jnp.zeros_like(carry_ref)

    pick1 = lane == i1
    pick2 = lane == i2
    onehot = jnp.where(pick1, 1.0, 0.0) + jnp.where(pick2, 1.0, 0.0)
    row = lax.broadcasted_iota(jnp.int32, (TM, TM), 0)
    col = lax.broadcasted_iota(jnp.int32, (TM, TM), 1)
    before = jnp.where(col < row, 1.0, 0.0).astype(BF16)
    seen = jnp.dot(before, onehot.astype(BF16), preferred_element_type=F32) + carry_ref[0:1, :]
    rank1 = jnp.sum(jnp.where(pick1, seen, 0.0), axis=-1, keepdims=True)
    rank2 = jnp.sum(jnp.where(pick2, seen, 0.0), axis=-1, keepdims=True)
    carry_ref[...] = carry_ref[...] + jnp.sum(onehot, axis=0, keepdims=True)
    count_ref[...] = carry_ref[...]

    fields = (v1 / (v1 + v2), v2 / (v1 + v2), i1.astype(F32), i2.astype(F32), rank1, rank2)
    route = jnp.zeros((TM, LANES), F32)
    for k, val in enumerate(fields):
        route = jnp.where(lane == k, val, route)
    route_ref[...] = route


def _post_specs():
    const2 = lambda i: (0, 0)
    return [pl.BlockSpec((1, 6, D_MODEL), lambda i: (_mod_index(i), 0, 0)),
            pl.BlockSpec((1, D_MODEL), const2),
            pl.BlockSpec((D_MODEL, D_MODEL), const2, pipeline_mode=pl.Buffered(1))]


def _post_dense_ffn(x_lat, x_ctx, o_lat, o_ctx, mod3, gn, w_out, wg, wu, wd):
    resident = lambda shape: pl.BlockSpec(shape, lambda i: (0, 0), pipeline_mode=pl.Buffered(1))
    return pl.pallas_call(
        _post_ffn_kernel, grid=(TOK_TILES,),
        in_specs=(_lat_ctx_specs(D_MODEL, 0) + _lat_ctx_specs(512, 0) + _lat_ctx_specs(512, 0)
                  + _post_specs() + [resident((D_MODEL, DENSE_FF)), resident((D_MODEL, DENSE_FF)),
                                     resident((DENSE_FF, D_MODEL))]),
        out_specs=pl.BlockSpec((TM, D_MODEL), lambda i: (i, 0)),
        out_shape=jax.ShapeDtypeStruct((N_TOK, D_MODEL), F32),
        compiler_params=_params("parallel"), name="post_attention_ffn",
    )(x_lat, x_ctx, o_lat[0], o_ctx[0], o_lat[1], o_ctx[1], mod3, gn, w_out, wg, wu, wd)


def _post_attention_router(x_lat, o_lat, mod3, gn, w_out, w_router):
    const2 = lambda i: (0, 0)
    n_tiles = LAT_TILES
    rows = n_tiles * TM
    x1_spec = pl.BlockSpec((TM, D_MODEL), lambda i: (i, 0))
    x1_shape = jax.ShapeDtypeStruct((rows, D_MODEL), F32)
    half_spec = pl.BlockSpec((TM, 512), lambda i: (i, 0))
    in_specs = [x1_spec, half_spec, half_spec] + _post_specs()
    args = [x_lat, *o_lat, mod3, gn, w_out]
    return pl.pallas_call(
        _post_router_kernel, grid=(n_tiles,),
        in_specs=in_specs + [pl.BlockSpec((D_MODEL, 2 * LANES), const2)],
        out_specs=[x1_spec,
                   pl.BlockSpec((TM * ROW_SUB, LANES), lambda i: (i, 0)),
                   pl.BlockSpec((TM, LANES), lambda i: (i, 0)),
                   pl.BlockSpec((8, LANES), const2)],
        out_shape=[x1_shape,
                   jax.ShapeDtypeStruct((rows * ROW_SUB, LANES), F32),
                   jax.ShapeDtypeStruct((rows, LANES), F32),
                   jax.ShapeDtypeStruct((8, LANES), F32)],
        scratch_shapes=[pltpu.VMEM((8, LANES), F32)],
        compiler_params=_params("arbitrary"), name="post_attention_router",
    )(*args, w_router)


def _row(ref, r):
    return ref.at[pl.ds(pl.multiple_of(r * ROW_SUB, ROW_SUB), ROW_SUB)]


def _rows_to_matrix(ref, n):
    return jnp.concatenate([ref[pl.ds(s, n, stride=ROW_SUB), :] for s in range(ROW_SUB)], axis=-1)


def _invert_kernel(dest_ref, inv_ref):
    def clear(r, carry):
        inv_ref[r] = -1
        return carry

    def place(a, carry):
        inv_ref[dest_ref[a]] = a
        return carry

    lax.fori_loop(0, SORT_ROWS, clear, 0, unroll=16)
    lax.fori_loop(0, 2 * N_LAT, place, 0, unroll=16)


def _invert(dest_flat):
    return pl.pallas_call(
        _invert_kernel,
        in_specs=[pl.BlockSpec(memory_space=pltpu.SMEM)],
        out_specs=pl.BlockSpec(memory_space=pltpu.SMEM),
        out_shape=jax.ShapeDtypeStruct((SORT_ROWS,), jnp.int32),
        name="moe_invert",
    )(dest_flat)


def _expert_kernel(te_ref, nu_ref, inv_ref, h_ref, wg_ref, wu_ref, wd_ref, y2_ref,
                   xbuf, ybuf, sem_g, sem_s):
    del te_ref
    t = pl.program_id(0)
    n_used = nu_ref[0]
    slot = lax.rem(t, 2)
    other = 1 - slot

    def gather_rows(tile, s):
        base = tile * TMX
        for j in range(TMX):
            token = lax.shift_right_logical(jnp.maximum(inv_ref[base + j], 0), 1)
            pltpu.make_async_copy(_row(h_ref, token), _row(xbuf.at[s], j),
                                  sem_g.at[s]).start(priority=j % 2)

    def scatter_rows(tile, s, real, scratch_block):
        base = tile * TMX
        for j in range(TMX):
            code = inv_ref[base + j]
            row = jnp.where(jnp.logical_and(code >= 0, real),
                            (code & 1) * N_LAT + lax.shift_right_logical(code, 1),
                            2 * N_LAT + scratch_block * TMX + j)
            pltpu.make_async_copy(_row(ybuf.at[s], j), _row(y2_ref, row),
                                  sem_s.at[s]).start(priority=j % 2)

    def wait_gather(s):
        pltpu.make_async_copy(h_ref.at[pl.ds(0, TMX * ROW_SUB)], xbuf.at[s], sem_g.at[s]).wait()

    def wait_scatter(s):
        pltpu.make_async_copy(ybuf.at[s], y2_ref.at[pl.ds(0, TMX * ROW_SUB)], sem_s.at[s]).wait()

    @pl.when(t == 0)
    def _():
        ybuf[...] = jnp.zeros_like(ybuf)
        gather_rows(0, 0)

    @pl.when(t < n_used)
    def _():
        wait_gather(slot)
        x = _rows_to_matrix(xbuf.at[slot], TMX).astype(BF16)
        gather_rows(jnp.minimum(t + 1, n_used - 1), other)
        scatter_rows(jnp.maximum(t - 1, 0), other, t > 0, other)
        y = _swiglu(x, wg_ref.at[0], wu_ref.at[0], wd_ref.at[0])

        @pl.when(t > 0)
        def _():
            wait_scatter(slot)

        for s in range(ROW_SUB):
            ybuf[slot, pl.ds(s, TMX, stride=ROW_SUB), :] = y[:, s * LANES:(s + 1) * LANES]

    @pl.when(t == n_used)
    def _():
        wait_gather(slot)
        scatter_rows(n_used - 1, other, True, other)
        wait_scatter(other)
        wait_scatter(slot)
        scatter_rows(n_used - 1, slot, False, 0)
        wait_scatter(slot)


def _experts(tile_expert, n_used, inv, hrow, wg, wu, wd):
    one = pl.Buffered(1)
    wspec = lambda shape: pl.BlockSpec(shape, lambda t, te, nu, iv: (te[t], 0, 0), pipeline_mode=one)
    grid_spec = pltpu.PrefetchScalarGridSpec(
        num_scalar_prefetch=3,
        grid=(SORT_TILES + 1,),
        in_specs=[pl.BlockSpec(memory_space=pl.ANY),
                  wspec((1, D_MODEL, EXPERT_FF)), wspec((1, D_MODEL, EXPERT_FF)),
                  wspec((1, EXPERT_FF, D_MODEL))],
        out_specs=pl.BlockSpec(memory_space=pl.ANY),
        scratch_shapes=[pltpu.VMEM((2, TMX * ROW_SUB, LANES), F32),
                        pltpu.VMEM((2, TMX * ROW_SUB, LANES), F32),
                        pltpu.SemaphoreType.DMA((2,)), pltpu.SemaphoreType.DMA((2,))])
    return pl.pallas_call(
        _expert_kernel,
        grid_spec=grid_spec,
        out_shape=jax.ShapeDtypeStruct((Y2_ROWS * ROW_SUB, LANES), F32),
        compiler_params=_params("arbitrary"),
        name="moe_experts",
    )(tile_expert, n_used, inv, hrow, wg, wu, wd)


def _combine_kernel(ya_ref, yb_ref, x1_ref, route_ref, mod_ref, o_ref):
    route = route_ref[...]
    y = (route[:, 0:1] * _rows_to_matrix(ya_ref, TM) + route[:, 1:2] * _rows_to_matrix(yb_ref, TM))
    o_ref[...] = x1_ref[...] + mod_ref[0, 5:6, :] * y


def _combine(y2, x1, route, mod3):
    return pl.pallas_call(
        _combine_kernel,
        grid=(LAT_TILES,),
        in_specs=[pl.BlockSpec((TM * ROW_SUB, LANES), lambda i: (i, 0)),
                  pl.BlockSpec((TM * ROW_SUB, LANES), lambda i: (LAT_TILES + i, 0)),
                  pl.BlockSpec((TM, D_MODEL), lambda i: (i, 0)),
                  pl.BlockSpec((TM, LANES), lambda i: (i, 0)),
                  pl.BlockSpec((1, 6, D_MODEL), lambda i: (i // SEQ_TILES, 0, 0))],
        out_specs=pl.BlockSpec((TM, D_MODEL), lambda i: (i, 0)),
        out_shape=jax.ShapeDtypeStruct((N_LAT, D_MODEL), F32),
        compiler_params=_params("parallel"),
        name="moe_combine",
    )(y2, y2, x1, route, mod3)


def _moe_ffn(hrow, x1, route, counts, mod3, wg, wu, wd):
    expert = route[:, 2:4].astype(jnp.int32)
    rank = route[:, 4:6].astype(jnp.int32)
    tiles = (counts[0, :N_EXPERTS].astype(jnp.int32) + TMX - 1) // TMX
    tile_end = jnp.cumsum(tiles)
    dest = (tile_end - tiles)[expert] * TMX + rank
    inv = _invert(dest.reshape(-1))
    n_used = tile_end[N_EXPERTS - 1:]
    tile_ids = jnp.minimum(jnp.arange(SORT_TILES + 1), n_used[0] - 1)
    tile_expert = jnp.sum(tile_ids[:, None] >= tile_end[None, :], axis=1).astype(jnp.int32)
    y2 = _experts(tile_expert, n_used.astype(jnp.int32), inv, hrow, wg, wu, wd)
    return _combine(y2, x1, route, mod3)


def _partner(half):
    idx = np.arange(4 * half)
    return np.where((idx // half) % 2 == 0, idx + half, idx - half)


_PARTNER_A = np.concatenate([np.arange(MLA_NOPE), MLA_NOPE + _partner(MLA_ROPE // 4),
                             np.arange(MLA_QK, LANES)])
_PARTNER_B = np.concatenate([_partner(GQA_HEAD_DIM // 4), GQA_HEAD_DIM + _partner(GQA_HEAD_DIM // 4)])


def _rope_tables():
    t = jnp.arange(SEQ)
    row = (t // GRID_W).astype(F32)
    col = (t % GRID_W).astype(F32)

    def one_axis(pos, half):
        freqs = ROPE_THETA ** (-jnp.arange(half, dtype=F32) / half)
        ang = pos[:, None] * freqs[None, :]
        cos, sin = jnp.cos(ang), jnp.sin(ang)
        return jnp.concatenate([cos, cos], -1), jnp.concatenate([-sin, sin], -1)

    def two_axes(half):
        r, c = one_axis(row, half), one_axis(col, half)
        return [jnp.concatenate([a, b], -1) for a, b in zip(r, c)]

    def pad_a(tbl, fill):
        return jnp.concatenate([jnp.full((SEQ, MLA_NOPE), fill, F32), tbl,
                                jnp.full((SEQ, LANES - MLA_QK), fill, F32)], -1)

    ta = two_axes(MLA_ROPE // 4)
    tb = [jnp.concatenate([x, x], -1) for x in two_axes(GQA_HEAD_DIM // 4)]
    tabs = [pad_a(ta[0], 1.0), pad_a(ta[1], 0.0)] + tb
    ident = [jnp.ones((TM, LANES), F32), jnp.zeros((TM, LANES), F32)]
    return jnp.stack([jnp.concatenate([tbl, idn], 0) for tbl, idn in zip(tabs, ident + ident)])


def _layer_weights(w_in, q_lat_norm, w_uq, kv_lat_norm, w_ukv, mla_q_gain, mla_k_gain,
                   gqa_q_gain, gqa_k_gain):
    split = MLA_Q_RANK + MLA_KV_RANK + MLA_ROPE
    perm_a = (np.arange(MLA_HEADS)[:, None] * LANES + _PARTNER_A[None, :]).reshape(-1)
    perm_b = (np.arange(GQA_HEADS // 2)[:, None] * LANES + _PARTNER_B[None, :]).reshape(-1)
    w_qb = w_in[:, split:split + GQA_HEADS * GQA_HEAD_DIM]
    w_kb = w_in[:, split + 512:split + 640]
    w_in_p = jnp.concatenate([w_in[:, :split], jnp.zeros((D_MODEL, 512 - split), F32),
                              w_in[:, split:], w_qb[:, perm_b], w_kb[:, _PARTNER_B]],
                             axis=1).astype(BF16)
    w_uq_p = jnp.pad(w_uq.reshape(MLA_Q_RANK, MLA_HEADS, MLA_QK),
                     ((0, 0), (0, 0), (0, LANES - MLA_QK))).reshape(MLA_Q_RANK, MLA_HEADS * LANES)
    w_uq_p = jnp.concatenate([w_uq_p, w_uq_p[:, perm_a]], axis=1)
    ukv = w_ukv.reshape(MLA_KV_RANK, MLA_HEADS, MLA_NOPE + MLA_V)
    w_k = jnp.pad(ukv[:, :, :MLA_NOPE], ((0, 0), (0, 0), (0, LANES - MLA_NOPE)))
    place = jnp.pad(jnp.eye(MLA_ROPE, dtype=F32), ((0, 0), (MLA_NOPE, LANES - MLA_QK)))
    place = jnp.broadcast_to(place[:, None, :], (MLA_ROPE, MLA_HEADS, LANES))
    w_k = jnp.concatenate([w_k, place,
                           jnp.zeros((256 - MLA_KV_RANK - MLA_ROPE, MLA_HEADS, LANES), F32)], 0)
    w_k = w_k.reshape(256, MLA_HEADS * LANES)
    w_v = jnp.pad(ukv[:, :, MLA_NOPE:].reshape(MLA_KV_RANK, MLA_HEADS * MLA_V),
                  ((0, 256 - MLA_KV_RANK), (0, 0)))
    w_kv = jnp.concatenate([w_k, w_k[:, perm_a], w_v], axis=1)
    glat = jnp.stack([q_lat_norm, jnp.pad(kv_lat_norm, (0, 256 - MLA_KV_RANK))])
    pad_qk = lambda g: jnp.pad(g, (0, LANES - MLA_QK))
    gains = [pad_qk(mla_q_gain) * (MLA_SCALE * LOG2E), pad_qk(mla_k_gain),
             jnp.tile(gqa_q_gain, 2) * (GQA_SCALE * LOG2E), jnp.tile(gqa_k_gain, 2)]
    partners = [_PARTNER_A, _PARTNER_A, _PARTNER_B, _PARTNER_B]
    g128 = jnp.stack([v for g, pm in zip(gains, partners) for v in (g, g[pm])])
    return w_in_p, glat, w_uq_p.astype(BF16), w_kv.astype(BF16), g128


def _mixer(x_lat, x_ctx, ctx_tile0, mod3, tab, norm_attn, w_in, q_lat_norm, w_uq, kv_lat_norm,
           w_ukv, mla_q_gain, mla_k_gain, gqa_q_gain, gqa_k_gain, with_ctx_queries, cast_a, cast_b):
    w_in_p, glat, w_uq_p, w_kv, g128 = _layer_weights(
        w_in, q_lat_norm, w_uq, kv_lat_norm, w_ukv, mla_q_gain, mla_k_gain, gqa_q_gain, gqa_k_gain)
    qa, ka, va, qb, kb, vb = _projection(x_lat, x_ctx, ctx_tile0, mod3,
                                         norm_attn.reshape(1, D_MODEL), w_in_p, glat,
                                         w_uq_p, w_kv, g128, tab)
    oa, oa_ctx, cast_a = _attention(qa, ka, va, _A_HEADS, _A_KV, with_ctx_queries, cast_a)
    ob, ob_ctx, cast_b = _attention(qb, kb, vb, _B_HEADS, _B_KV, with_ctx_queries, cast_b)
    return (oa, ob), ((oa_ctx, ob_ctx) if with_ctx_queries else None), (cast_a, cast_b)


def kernel(x, c, ctx, c_ctx, l0_w_mod, l0_b_mod, l0_norm_attn, l0_w_in, l0_q_lat_norm, l0_w_uq, l0_kv_lat_norm, l0_w_ukv, l0_mla_q_gain, l0_mla_k_gain, l0_gqa_q_gain, l0_gqa_k_gain, l0_w_out, l0_norm_ffn, l0_ffn_w_gate, l0_ffn_w_up, l0_ffn_w_down, l1_w_mod, l1_b_mod, l1_norm_attn, l1_w_in, l1_q_lat_norm, l1_w_uq, l1_kv_lat_norm, l1_w_ukv, l1_mla_q_gain, l1_mla_k_gain, l1_gqa_q_gain, l1_gqa_k_gain, l1_w_out, l1_norm_ffn, l1_router, l1_exp_w_gate, l1_exp_w_up, l1_exp_w_down):
    x_lat = x.reshape(N_LAT, D_MODEL)
    x_ctx = ctx.reshape(N_CTX, D_MODEL)
    cc = jnp.concatenate([c, c_ctx[None, :], jnp.zeros((16 - BATCH - 1, D_MODEL), F32)], axis=0)
    tab = _rope_tables()

    mod3 = _modulation(cc, l0_w_mod, l0_b_mod)
    o_lat, o_ctx, (wg_x, wu_x) = _mixer(
        x_lat, x_ctx, 0, mod3, tab, l0_norm_attn, l0_w_in, l0_q_lat_norm, l0_w_uq, l0_kv_lat_norm,
        l0_w_ukv, l0_mla_q_gain, l0_mla_k_gain, l0_gqa_q_gain, l0_gqa_k_gain, True,
        l1_exp_w_gate.reshape(N_EXPERTS * D_MODEL, EXPERT_FF),
        l1_exp_w_up.reshape(N_EXPERTS * D_MODEL, EXPERT_FF))
    xall = _post_dense_ffn(x_lat, x_ctx, o_lat, o_ctx, mod3, l0_norm_ffn.reshape(1, D_MODEL),
                           l0_w_out.astype(BF16), l0_ffn_w_gate.astype(BF16),
                           l0_ffn_w_up.astype(BF16), l0_ffn_w_down.astype(BF16))

    mod3 = _modulation(cc, l1_w_mod, l1_b_mod)
    o_lat, _, (wd_x, _) = _mixer(
        xall, xall, LAT_TILES, mod3, tab, l1_norm_attn, l1_w_in, l1_q_lat_norm, l1_w_uq,
        l1_kv_lat_norm, l1_w_ukv, l1_mla_q_gain, l1_mla_k_gain, l1_gqa_q_gain, l1_gqa_k_gain,
        False, l1_exp_w_down.reshape(N_EXPERTS * EXPERT_FF, D_MODEL), None)
    w_router = jnp.pad(l1_router, ((0, 0), (0, LANES - N_EXPERTS)))
    w_router_hi = w_router.astype(BF16)
    w_router = jnp.concatenate([w_router_hi, (w_router - w_router_hi.astype(F32)).astype(BF16)],
                               axis=1)
    x1, hrow, route, counts = _post_attention_router(xall, o_lat, mod3,
                                                     l1_norm_ffn.reshape(1, D_MODEL),
                                                     l1_w_out.astype(BF16), w_router)
    out = _moe_ffn(hrow, x1, route, counts, mod3,
                   wg_x.reshape(N_EXPERTS, D_MODEL, EXPERT_FF),
                   wu_x.reshape(N_EXPERTS, D_MODEL, EXPERT_FF),
                   wd_x.reshape(N_EXPERTS, EXPERT_FF, D_MODEL))
    return out.reshape(BATCH, SEQ, D_MODEL)
```

```python
import functools
import math

import jax
import jax.numpy as jnp
import numpy as np
from jax import lax
from jax.experimental import pallas as pl
from jax.experimental.pallas import tpu as pltpu

D_MODEL = 1024
BATCH = 8
SEQ = 2048
CTX_LEN = 256
GRID_W = 64
MLA_HEADS = 8
MLA_NOPE = 64
MLA_ROPE = 32
MLA_V = 64
MLA_QK = MLA_NOPE + MLA_ROPE
MLA_Q_RANK = 256
MLA_KV_RANK = 128
GQA_HEADS = 8
GQA_KV_HEADS = 2
GQA_HEAD_DIM = 64
DENSE_FF = 2816
N_EXPERTS = 8
EXPERT_FF = 2816
ROPE_THETA = 10000.0
NORM_EPS = 1e-6
LOG2E = math.log2(math.e)
MLA_SCALE = MLA_QK ** -0.5
GQA_SCALE = GQA_HEAD_DIM ** -0.5

LANES = 128
N_LAT = BATCH * SEQ
N_CTX = BATCH * CTX_LEN
N_TOK = N_LAT + N_CTX
TM = 512
LAT_TILES = N_LAT // TM
TOK_TILES = N_TOK // TM
SEQ_TILES = SEQ // TM
Q_CHUNK = 256
V_WIDTH = 2 * LANES
IN_PAD = 1920
FF_SPLITS = (0, 1536, 2816)
ROW_SUB = D_MODEL // LANES
TMX = 512
SORT_TILES = 2 * N_LAT // TMX + N_EXPERTS
SORT_ROWS = SORT_TILES * TMX
Y2_ROWS = 2 * N_LAT + 2 * TMX
VMEM_LIMIT = 56 * 1024 * 1024

F32 = jnp.float32
BF16 = jnp.bfloat16


def _silu(x):
    return x / (1.0 + jnp.exp(-x))


def _params(*sem):
    return pltpu.CompilerParams(dimension_semantics=sem, vmem_limit_bytes=VMEM_LIMIT)


def _mod_kernel(c_ref, w_ref, b_ref, o_ref):
    s = _silu(c_ref[...])
    o_ref[...] = jnp.dot(s, w_ref[...], precision=lax.Precision.HIGHEST,
                         preferred_element_type=F32) + b_ref[...]


def _modulation(cc, w_mod, b_mod):
    n = w_mod.shape[1]
    bn = 1024
    out = pl.pallas_call(
        _mod_kernel,
        grid=(n // bn,),
        in_specs=[pl.BlockSpec((16, D_MODEL), lambda j: (0, 0)),
                  pl.BlockSpec((D_MODEL, bn), lambda j: (0, j)),
                  pl.BlockSpec((1, bn), lambda j: (0, j))],
        out_specs=pl.BlockSpec((16, bn), lambda j: (0, j)),
        out_shape=jax.ShapeDtypeStruct((16, n), F32),
        compiler_params=_params("parallel"),
        name="modulation",
    )(cc, w_mod, b_mod.reshape(1, n))
    return out.reshape(16, 6, D_MODEL)


def _tile_rows(lat_ref, ctx_ref):
    return jnp.where(pl.program_id(0) < LAT_TILES, lat_ref[...], ctx_ref[...])


def _proj_kernel(xl_ref, xc_ref, mod_ref, gn_ref, win_ref, glat_ref, wuq_ref, wkv_ref, g128_ref,
                 tab_ref, qa_ref, ka_ref, va_ref, qb_ref, kb_ref, vb_ref):
    x = _tile_rows(xl_ref, xc_ref)
    r = lax.rsqrt(jnp.mean(x * x, axis=-1, keepdims=True) + NORM_EPS)
    h = (x * r * gn_ref[...]) * (1.0 + mod_ref[0, 1:2, :]) + mod_ref[0, 0:1, :]
    p = jnp.dot(h.astype(BF16), win_ref[...], preferred_element_type=F32)

    lane = lax.broadcasted_iota(jnp.int32, (TM, LANES), 1)
    lo = lane < 64
    ones_col = jnp.where(lane == 0, 1.0, 0.0).astype(BF16)

    def tables(base, row):
        return tab_ref[base] * g128_ref[row:row + 1, :], tab_ref[base + 1] * g128_ref[row + 1:row + 2, :]

    cq = p[:, 0:256]
    rq = lax.rsqrt(jnp.mean(cq * cq, axis=-1, keepdims=True) + NORM_EPS)
    qa = jnp.dot((cq * rq * glat_ref[0:1, :]).astype(BF16), wuq_ref[...],
                 preferred_element_type=F32)
    ckv = p[:, 256:384]
    rkv = lax.rsqrt(jnp.mean(ckv * ckv, axis=-1, keepdims=True) + NORM_EPS)
    slab = jnp.concatenate([ckv * rkv * glat_ref[1:2, 0:128], p[:, 384:512]], axis=-1)
    kv = jnp.dot(slab.astype(BF16), wkv_ref[...], preferred_element_type=F32)

    def head_a(src, hd, cos_g, sin_g):
        blk = src[:, hd * LANES:(hd + 1) * LANES]
        partner = src[:, 1024 + hd * LANES:1024 + (hd + 1) * LANES]
        rr = lax.rsqrt(jnp.sum(blk * blk, axis=-1, keepdims=True) * (1.0 / MLA_QK) + NORM_EPS)
        return ((blk * cos_g + partner * sin_g) * rr).astype(BF16)

    cq_g, sq_g = tables(0, 0)
    ck_g, sk_g = tables(0, 2)
    for hd in range(MLA_HEADS):
        qa_ref[hd] = head_a(qa, hd, cq_g, sq_g)
        ka_ref[hd] = head_a(kv, hd, ck_g, sk_g)
    for pr in range(MLA_HEADS // 2):
        blk = kv[:, 2048 + pr * LANES:2048 + (pr + 1) * LANES]
        va_ref[2 * pr, :, 0:LANES] = jnp.where(lo, blk, 0.0).astype(BF16)
        va_ref[2 * pr + 1, :, 0:LANES] = jnp.where(lo, 0.0, blk).astype(BF16)
    for hd in range(MLA_HEADS):
        va_ref[hd, :, LANES:V_WIDTH] = ones_col

    def pair_b(blk, partner, cos_g, sin_g):
        sq = blk * blk
        s_lo = jnp.sum(jnp.where(lo, sq, 0.0), axis=-1, keepdims=True)
        s_hi = jnp.sum(jnp.where(lo, 0.0, sq), axis=-1, keepdims=True)
        rr = jnp.where(lo, lax.rsqrt(s_lo * (1.0 / GQA_HEAD_DIM) + NORM_EPS),
                       lax.rsqrt(s_hi * (1.0 / GQA_HEAD_DIM) + NORM_EPS))
        return (blk * cos_g + partner * sin_g) * rr

    cq_g, sq_g = tables(2, 4)
    ck_g, sk_g = tables(2, 6)
    for pr in range(GQA_HEADS // 2):
        blk = p[:, 512 + pr * LANES:512 + (pr + 1) * LANES]
        partner = p[:, 1280 + pr * LANES:1280 + (pr + 1) * LANES]
        qb_ref[pr] = pair_b(blk, partner, cq_g, sq_g).astype(BF16)
    kb = pair_b(p[:, 1024:1152], p[:, 1792:1920], ck_g, sk_g)
    kb_sw = pltpu.roll(kb, 64, 1)
    kb_ref[0] = jnp.where(lo, kb, 0.0).astype(BF16)
    kb_ref[1] = jnp.where(lo, 0.0, kb_sw).astype(BF16)
    kb_ref[2] = jnp.where(lo, kb_sw, 0.0).astype(BF16)
    kb_ref[3] = jnp.where(lo, 0.0, kb).astype(BF16)
    vb = p[:, 1152:1280]
    vb_sw = pltpu.roll(vb, 64, 1)
    vb_ref[0, :, 0:LANES] = jnp.where(lo, vb, 0.0).astype(BF16)
    vb_ref[1, :, 0:LANES] = jnp.where(lo, 0.0, vb_sw).astype(BF16)
    vb_ref[2, :, 0:LANES] = jnp.where(lo, vb_sw, 0.0).astype(BF16)
    vb_ref[3, :, 0:LANES] = jnp.where(lo, 0.0, vb).astype(BF16)
    for j in range(2 * GQA_KV_HEADS):
        vb_ref[j, :, LANES:V_WIDTH] = ones_col


def _mod_index(i):
    return jnp.where(i < LAT_TILES, i // SEQ_TILES, BATCH)


def _lat_ctx_specs(width, ctx_tile0):
    return [pl.BlockSpec((TM, width), lambda i: (jnp.minimum(i, LAT_TILES - 1), 0)),
            pl.BlockSpec((TM, width), lambda i: (ctx_tile0 + jnp.maximum(i - LAT_TILES, 0), 0))]


def _projection(x_lat, x_ctx, ctx_tile0, mod3, gn, w_in, glat, w_uq, w_kv, g128, tab):
    const2 = lambda i: (0, 0)
    head_out = lambda n, w: pl.BlockSpec((n, TM, w), lambda i: (0, i, 0))
    head_shape = lambda n, w: jax.ShapeDtypeStruct((n, N_TOK, w), BF16)
    return pl.pallas_call(
        _proj_kernel,
        grid=(TOK_TILES,),
        in_specs=_lat_ctx_specs(D_MODEL, ctx_tile0) + [
                  pl.BlockSpec((1, 6, D_MODEL), lambda i: (_mod_index(i), 0, 0)),
                  pl.BlockSpec((1, D_MODEL), const2),
                  pl.BlockSpec((D_MODEL, IN_PAD), const2),
                  pl.BlockSpec((2, 256), const2),
                  pl.BlockSpec((MLA_Q_RANK, 2 * MLA_HEADS * LANES), const2),
                  pl.BlockSpec((256, 2 * MLA_HEADS * LANES + MLA_HEADS * MLA_V), const2),
                  pl.BlockSpec((8, LANES), const2),
                  pl.BlockSpec((4, TM, LANES),
                               lambda i: (0, jnp.where(i < LAT_TILES, i % SEQ_TILES, SEQ_TILES), 0))],
        out_specs=[head_out(8, LANES), head_out(8, LANES), head_out(8, V_WIDTH),
                   head_out(4, LANES), head_out(4, LANES), head_out(4, V_WIDTH)],
        out_shape=[head_shape(8, LANES), head_shape(8, LANES), head_shape(8, V_WIDTH),
                   head_shape(4, LANES), head_shape(4, LANES), head_shape(4, V_WIDTH)],
        compiler_params=_params("parallel"),
        name="projection",
    )(x_lat, x_ctx, mod3, gn, w_in, glat, w_uq, w_kv, g128, tab)


def _pair_rows(q_refs, r0, n, kv_refs):
    nt = (((1,), (1,)), ((), ()))
    out = None
    for u, q_ref in enumerate(q_refs):
        q = q_ref[r0:r0 + n, :]
        scores = [lax.dot_general(q, k_ref[u], nt, preferred_element_type=F32)
                  for k_ref, _ in kv_refs]
        m = functools.reduce(jnp.maximum, [jnp.max(sc, axis=-1, keepdims=True) for sc in scores])
        acc = None
        for sc, (_, v_ref) in zip(scores, kv_refs):
            part = jnp.dot(jnp.exp2((sc - m).astype(BF16)), v_ref[u], preferred_element_type=F32)
            acc = part if acc is None else acc + part
        o = acc[:, 0:LANES] / acc[:, LANES:LANES + 1]
        out = o if out is None else out + o
    return out


def _attn_kernel(*refs, ctx_queries, cast):
    q0_ref, q1_ref, kl_ref, kc_ref, vl_ref, vc_ref = refs[:6]
    rest = list(refs[6:])
    qc_refs = (rest.pop(0), rest.pop(0)) if ctx_queries else None
    src_ref = rest.pop(0) if cast else None
    o_ref = rest.pop(0)
    oc_ref = rest.pop(0) if ctx_queries else None
    if cast:
        rest.pop(0)[...] = src_ref[...].astype(BF16)
    for r0 in range(0, SEQ, Q_CHUNK):
        out = _pair_rows((q0_ref, q1_ref), r0, Q_CHUNK, ((kc_ref, vc_ref), (kl_ref, vl_ref)))
        o_ref[r0:r0 + Q_CHUNK, :] = out.astype(o_ref.dtype)
    if ctx_queries:
        oc_ref[...] = _pair_rows(qc_refs, 0, CTX_LEN, ((kc_ref, vc_ref),)).astype(oc_ref.dtype)


def _attention(q, k, v, q_heads, kv_pair, ctx_queries, cast_src):
    n_units = 4
    ctx_blk0 = N_LAT // CTX_LEN

    def qspec(which, rows, blk0):
        return pl.BlockSpec((None, rows, LANES), lambda b, p: (q_heads(p)[which], blk0 + b, 0))

    def kvspec(rows, blk0, width):
        return pl.BlockSpec((2, rows, width), lambda b, p: (kv_pair(p), blk0 + b, 0))

    in_specs = [qspec(0, SEQ, 0), qspec(1, SEQ, 0), kvspec(SEQ, 0, LANES),
                kvspec(CTX_LEN, ctx_blk0, LANES), kvspec(SEQ, 0, V_WIDTH),
                kvspec(CTX_LEN, ctx_blk0, V_WIDTH)]
    args = [q, q, k, k, v, v]
    out_specs = [pl.BlockSpec((SEQ, LANES), lambda b, p: (b, p))]
    out_shape = [jax.ShapeDtypeStruct((N_LAT, n_units * LANES), BF16)]
    if ctx_queries:
        in_specs += [qspec(0, CTX_LEN, ctx_blk0), qspec(1, CTX_LEN, ctx_blk0)]
        args += [q, q]
        out_specs.append(pl.BlockSpec((CTX_LEN, LANES), lambda b, p: (b, p)))
        out_shape.append(jax.ShapeDtypeStruct((N_CTX, n_units * LANES), BF16))
    if cast_src is not None:
        rows, cols = cast_src.shape
        slab = pl.BlockSpec((rows // (BATCH * n_units), cols), lambda b, p: (b * n_units + p, 0))
        in_specs.append(slab)
        args.append(cast_src)
        out_specs.append(slab)
        out_shape.append(jax.ShapeDtypeStruct((rows, cols), BF16))
    outs = list(pl.pallas_call(
        functools.partial(_attn_kernel, ctx_queries=ctx_queries, cast=cast_src is not None),
        grid=(BATCH, n_units),
        in_specs=in_specs, out_specs=out_specs, out_shape=out_shape,
        compiler_params=_params("parallel", "parallel"),
        name="attention",
    )(*args))
    o_lat = outs.pop(0)
    o_ctx = outs.pop(0) if ctx_queries else None
    return o_lat, o_ctx, (outs.pop(0) if cast_src is not None else None)


_A_HEADS = lambda p: (2 * p, 2 * p + 1)
_A_KV = lambda p: p
_B_HEADS = lambda p: (p, p)
_B_KV = lambda p: p // 2


def _post_body(x, oa, ob, mod_ref, gn_ref, wo_ref):
    y = (jnp.dot(oa, wo_ref[0:512, :], preferred_element_type=F32)
         + jnp.dot(ob, wo_ref[512:1024, :], preferred_element_type=F32))
    x1 = x + mod_ref[0, 2:3, :] * y
    r = lax.rsqrt(jnp.mean(x1 * x1, axis=-1, keepdims=True) + NORM_EPS)
    h2 = (x1 * r * gn_ref[...]) * (1.0 + mod_ref[0, 4:5, :]) + mod_ref[0, 3:4, :]
    return x1, h2


def _swiglu(h, wg_ref, wu_ref, wd_ref):
    y = None
    for lo, hi in zip(FF_SPLITS[:-1], FF_SPLITS[1:]):
        g = jnp.dot(h, wg_ref[:, lo:hi], preferred_element_type=F32)
        u = jnp.dot(h, wu_ref[:, lo:hi], preferred_element_type=F32)
        yc = jnp.dot((_silu(g) * u).astype(BF16), wd_ref[lo:hi, :], preferred_element_type=F32)
        y = yc if y is None else y + yc
    return y


def _post_ffn_kernel(xl_ref, xc_ref, oa_ref, oac_ref, ob_ref, obc_ref, mod_ref, gn_ref, wo_ref,
                     wg_ref, wu_ref, wd_ref, o_ref):
    x1, h2 = _post_body(_tile_rows(xl_ref, xc_ref), _tile_rows(oa_ref, oac_ref),
                        _tile_rows(ob_ref, obc_ref), mod_ref, gn_ref, wo_ref)
    y = _swiglu(h2.astype(BF16), wg_ref, wu_ref, wd_ref)
    o_ref[...] = x1 + mod_ref[0, 5:6, :] * y


def _post_router_kernel(x_ref, oa_ref, ob_ref, mod_ref, gn_ref, wo_ref, wr_ref,
                        x1_ref, hrow_ref, route_ref, count_ref, carry_ref):
    x1, h2 = _post_body(x_ref[...], oa_ref[...], ob_ref[...], mod_ref, gn_ref, wo_ref)
    x1_ref[...] = x1
    for s in range(ROW_SUB):
        hrow_ref[pl.ds(s, TM, stride=ROW_SUB), :] = h2[:, s * LANES:(s + 1) * LANES]

    h_hi = h2.astype(BF16)
    h_lo = (h2 - h_hi.astype(F32)).astype(BF16)
    part = jnp.dot(h_hi, wr_ref[...], preferred_element_type=F32)
    logits = (part[:, 0:LANES] + part[:, LANES:2 * LANES]
              + jnp.dot(h_lo, wr_ref[:, 0:LANES], preferred_element_type=F32))
    lane = lax.broadcasted_iota(jnp.int32, logits.shape, 1)
    valid = lane < N_EXPERTS
    lg = jnp.where(valid, logits, -jnp.inf)
    e = jnp.exp(lg - jnp.max(lg, axis=-1, keepdims=True))
    probs = e / jnp.sum(e, axis=-1, keepdims=True)
    v1 = jnp.max(probs, axis=-1, keepdims=True)
    i1 = jnp.min(jnp.where(probs == v1, lane, LANES), axis=-1, keepdims=True)
    rest = jnp.where(valid & (lane != i1), probs, -1.0)
    v2 = jnp.max(rest, axis=-1, keepdims=True)
    i2 = jnp.min(jnp.where(rest == v2, lane, LANES), axis=-1, keepdims=True)

    @pl.when(pl.program_id(0) == 0)
    def _():
        carry_ref[...] = jnp.zeros_like(carry_ref)

    pick1 = lane == i1
    pick2 = lane == i2
    onehot = jnp.where(pick1, 1.0, 0.0) + jnp.where(pick2, 1.0, 0.0)
    row = lax.broadcasted_iota(jnp.int32, (TM, TM), 0)
    col = lax.broadcasted_iota(jnp.int32, (TM, TM), 1)
    before = jnp.where(col < row, 1.0, 0.0).astype(BF16)
    seen = jnp.dot(before, onehot.astype(BF16), preferred_element_type=F32) + carry_ref[0:1, :]
    rank1 = jnp.sum(jnp.where(pick1, seen, 0.0), axis=-1, keepdims=True)
    rank2 = jnp.sum(jnp.where(pick2, seen, 0.0), axis=-1, keepdims=True)
    carry_ref[...] = carry_ref[...] + jnp.sum(onehot, axis=0, keepdims=True)
    count_ref[...] = carry_ref[...]

    fields = (v1 / (v1 + v2), v2 / (v1 + v2), i1.astype(F32), i2.astype(F32), rank1, rank2)
    route = jnp.zeros((TM, LANES), F32)
    for k, val in enumerate(fields):
        route = jnp.where(lane == k, val, route)
    route_ref[...] = route


def _post_specs():
    const2 = lambda i: (0, 0)
    return [pl.BlockSpec((1, 6, D_MODEL), lambda i: (_mod_index(i), 0, 0)),
            pl.BlockSpec((1, D_MODEL), const2),
            pl.BlockSpec((D_MODEL, D_MODEL), const2, pipeline_mode=pl.Buffered(1))]


def _post_dense_ffn(x_lat, x_ctx, o_lat, o_ctx, mod3, gn, w_out, wg, wu, wd):
    resident = lambda shape: pl.BlockSpec(shape, lambda i: (0, 0), pipeline_mode=pl.Buffered(1))
    return pl.pallas_call(
        _post_ffn_kernel, grid=(TOK_TILES,),
        in_specs=(_lat_ctx_specs(D_MODEL, 0) + _lat_ctx_specs(512, 0) + _lat_ctx_specs(512, 0)
                  + _post_specs() + [resident((D_MODEL, DENSE_FF)), resident((D_MODEL, DENSE_FF)),
                                     resident((DENSE_FF, D_MODEL))]),
        out_specs=pl.BlockSpec((TM, D_MODEL), lambda i: (i, 0)),
        out_shape=jax.ShapeDtypeStruct((N_TOK, D_MODEL), F32),
        compiler_params=_params("parallel"), name="post_attention_ffn",
    )(x_lat, x_ctx, o_lat[0], o_ctx[0], o_lat[1], o_ctx[1], mod3, gn, w_out, wg, wu, wd)


def _post_attention_router(x_lat, o_lat, mod3, gn, w_out, w_router):
    const2 = lambda i: (0, 0)
    n_tiles = LAT_TILES
    rows = n_tiles * TM
    x1_spec = pl.BlockSpec((TM, D_MODEL), lambda i: (i, 0))
    x1_shape = jax.ShapeDtypeStruct((rows, D_MODEL), F32)
    half_spec = pl.BlockSpec((TM, 512), lambda i: (i, 0))
    in_specs = [x1_spec, half_spec, half_spec] + _post_specs()
    args = [x_lat, *o_lat, mod3, gn, w_out]
    return pl.pallas_call(
        _post_router_kernel, grid=(n_tiles,),
        in_specs=in_specs + [pl.BlockSpec((D_MODEL, 2 * LANES), const2)],
        out_specs=[x1_spec,
                   pl.BlockSpec((TM * ROW_SUB, LANES), lambda i: (i, 0)),
                   pl.BlockSpec((TM, LANES), lambda i: (i, 0)),
                   pl.BlockSpec((8, LANES), const2)],
        out_shape=[x1_shape,
                   jax.ShapeDtypeStruct((rows * ROW_SUB, LANES), F32),
                   jax.ShapeDtypeStruct((rows, LANES), F32),
                   jax.ShapeDtypeStruct((8, LANES), F32)],
        scratch_shapes=[pltpu.VMEM((8, LANES), F32)],
        compiler_params=_params("arbitrary"), name="post_attention_router",
    )(*args, w_router)


def _row(ref, r):
    return ref.at[pl.ds(pl.multiple_of(r * ROW_SUB, ROW_SUB), ROW_SUB)]


def _rows_to_matrix(ref, n):
    return jnp.concatenate([ref[pl.ds(s, n, stride=ROW_SUB), :] for s in range(ROW_SUB)], axis=-1)


def _tile_row(buf, j):
    return buf.at[j // 8, :, j % 8, :]


def _invert_kernel(dest_ref, inv_ref):
    def clear(r, carry):
        inv_ref[r] = -1
        return carry

    def place(a, carry):
        inv_ref[dest_ref[a]] = a
        return carry

    lax.fori_loop(0, SORT_ROWS, clear, 0, unroll=16)
    lax.fori_loop(0, 2 * N_LAT, place, 0, unroll=16)


def _invert(dest_flat):
    return pl.pallas_call(
        _invert_kernel,
        in_specs=[pl.BlockSpec(memory_space=pltpu.SMEM)],
        out_specs=pl.BlockSpec(memory_space=pltpu.SMEM),
        out_shape=jax.ShapeDtypeStruct((SORT_ROWS,), jnp.int32),
        name="moe_invert",
    )(dest_flat)


def _expert_kernel(te_ref, nu_ref, inv_ref, h_ref, wg_ref, wu_ref, wd_ref, y2_ref,
                   xbuf, ybuf, sem_g, sem_s):
    del te_ref
    t = pl.program_id(0)
    n_used = nu_ref[0]
    slot = lax.rem(t, 2)
    other = 1 - slot

    def gather_rows(tile, s):
        base = tile * TMX
        for j in range(TMX):
            token = lax.shift_right_logical(jnp.maximum(inv_ref[base + j], 0), 1)
            pltpu.make_async_copy(_row(h_ref, token), _tile_row(xbuf.at[s], j),
                                  sem_g.at[s]).start(priority=j % 2)

    def scatter_rows(tile, s, real, scratch_block):
        base = tile * TMX
        for j in range(TMX):
            code = inv_ref[base + j]
            row = jnp.where(jnp.logical_and(code >= 0, real),
                            (code & 1) * N_LAT + lax.shift_right_logical(code, 1),
                            2 * N_LAT + scratch_block * TMX + j)
            pltpu.make_async_copy(_tile_row(ybuf.at[s], j), _row(y2_ref, row),
                                  sem_s.at[s]).start(priority=j % 2)

    def wait_gather(s):
        pltpu.make_async_copy(xbuf.at[s], xbuf.at[s], sem_g.at[s]).wait()

    def wait_scatter(s):
        pltpu.make_async_copy(ybuf.at[s], ybuf.at[s], sem_s.at[s]).wait()

    @pl.when(t == 0)
    def _():
        ybuf[...] = jnp.zeros_like(ybuf)
        gather_rows(0, 0)

    @pl.when(t < n_used)
    def _():
        wait_gather(slot)
        x = jnp.concatenate([xbuf[slot, :, m].reshape(TMX, LANES) for m in range(ROW_SUB)],
                            axis=-1).astype(BF16)
        gather_rows(jnp.minimum(t + 1, n_used - 1), other)
        scatter_rows(jnp.maximum(t - 1, 0), other, t > 0, other)
        y = _swiglu(x, wg_ref.at[0], wu_ref.at[0], wd_ref.at[0])

        @pl.when(t > 0)
        def _():
            wait_scatter(slot)

        for m in range(ROW_SUB):
            ybuf[slot, :, m] = y[:, m * LANES:(m + 1) * LANES].reshape(TMX // 8, 8, LANES)

    @pl.when(t == n_used)
    def _():
        wait_gather(slot)
        scatter_rows(n_used - 1, other, True, other)
        wait_scatter(other)
        wait_scatter(slot)
        scatter_rows(n_used - 1, slot, False, 0)
        wait_scatter(slot)


def _experts(tile_expert, n_used, inv, hrow, wg, wu, wd):
    one = pl.Buffered(1)
    wspec = lambda shape: pl.BlockSpec(shape, lambda t, te, nu, iv: (te[t], 0, 0), pipeline_mode=one)
    grid_spec = pltpu.PrefetchScalarGridSpec(
        num_scalar_prefetch=3,
        grid=(SORT_TILES + 1,),
        in_specs=[pl.BlockSpec(memory_space=pl.ANY),
                  wspec((1, D_MODEL, EXPERT_FF)), wspec((1, D_MODEL, EXPERT_FF)),
                  wspec((1, EXPERT_FF, D_MODEL))],
        out_specs=pl.BlockSpec(memory_space=pl.ANY),
        scratch_shapes=[pltpu.VMEM((2, TMX // 8, ROW_SUB, 8, LANES), F32),
                        pltpu.VMEM((2, TMX // 8, ROW_SUB, 8, LANES), F32),
                        pltpu.SemaphoreType.DMA((2,)), pltpu.SemaphoreType.DMA((2,))])
    return pl.pallas_call(
        _expert_kernel,
        grid_spec=grid_spec,
        out_shape=jax.ShapeDtypeStruct((Y2_ROWS * ROW_SUB, LANES), F32),
        compiler_params=_params("arbitrary"),
        name="moe_experts",
    )(tile_expert, n_used, inv, hrow, wg, wu, wd)


def _combine_kernel(ya_ref, yb_ref, x1_ref, route_ref, mod_ref, o_ref):
    route = route_ref[...]
    y = (route[:, 0:1] * _rows_to_matrix(ya_ref, TM) + route[:, 1:2] * _rows_to_matrix(yb_ref, TM))
    o_ref[...] = x1_ref[...] + mod_ref[0, 5:6, :] * y


def _combine(y2, x1, route, mod3):
    return pl.pallas_call(
        _combine_kernel,
        grid=(LAT_TILES,),
        in_specs=[pl.BlockSpec((TM * ROW_SUB, LANES), lambda i: (i, 0)),
                  pl.BlockSpec((TM * ROW_SUB, LANES), lambda i: (LAT_TILES + i, 0)),
                  pl.BlockSpec((TM, D_MODEL), lambda i: (i, 0)),
                  pl.BlockSpec((TM, LANES), lambda i: (i, 0)),
                  pl.BlockSpec((1, 6, D_MODEL), lambda i: (i // SEQ_TILES, 0, 0))],
        out_specs=pl.BlockSpec((TM, D_MODEL), lambda i: (i, 0)),
        out_shape=jax.ShapeDtypeStruct((N_LAT, D_MODEL), F32),
        compiler_params=_params("parallel"),
        name="moe_combine",
    )(y2, y2, x1, route, mod3)


def _moe_ffn(hrow, x1, route, counts, mod3, wg, wu, wd):
    expert = route[:, 2:4].astype(jnp.int32)
    rank = route[:, 4:6].astype(jnp.int32)
    tiles = (counts[0, :N_EXPERTS].astype(jnp.int32) + TMX - 1) // TMX
    tile_end = jnp.cumsum(tiles)
    dest = (tile_end - tiles)[expert] * TMX + rank
    inv = _invert(dest.reshape(-1))
    n_used = tile_end[N_EXPERTS - 1:]
    tile_ids = jnp.minimum(jnp.arange(SORT_TILES + 1), n_used[0] - 1)
    tile_expert = jnp.sum(tile_ids[:, None] >= tile_end[None, :], axis=1).astype(jnp.int32)
    y2 = _experts(tile_expert, n_used.astype(jnp.int32), inv, hrow, wg, wu, wd)
    return _combine(y2, x1, route, mod3)


def _partner(half):
    idx = np.arange(4 * half)
    return np.where((idx // half) % 2 == 0, idx + half, idx - half)


_PARTNER_A = np.concatenate([np.arange(MLA_NOPE), MLA_NOPE + _partner(MLA_ROPE // 4),
                             np.arange(MLA_QK, LANES)])
_PARTNER_B = np.concatenate([_partner(GQA_HEAD_DIM // 4), GQA_HEAD_DIM + _partner(GQA_HEAD_DIM // 4)])


def _rope_tables():
    t = jnp.arange(SEQ)
    row = (t // GRID_W).astype(F32)
    col = (t % GRID_W).astype(F32)

    def one_axis(pos, half):
        freqs = ROPE_THETA ** (-jnp.arange(half, dtype=F32) / half)
        ang = pos[:, None] * freqs[None, :]
        cos, sin = jnp.cos(ang), jnp.sin(ang)
        return jnp.concatenate([cos, cos], -1), jnp.concatenate([-sin, sin], -1)

    def two_axes(half):
        r, c = one_axis(row, half), one_axis(col, half)
        return [jnp.concatenate([a, b], -1) for a, b in zip(r, c)]

    def pad_a(tbl, fill):
        return jnp.concatenate([jnp.full((SEQ, MLA_NOPE), fill, F32), tbl,
                                jnp.full((SEQ, LANES - MLA_QK), fill, F32)], -1)

    ta = two_axes(MLA_ROPE // 4)
    tb = [jnp.concatenate([x, x], -1) for x in two_axes(GQA_HEAD_DIM // 4)]
    tabs = [pad_a(ta[0], 1.0), pad_a(ta[1], 0.0)] + tb
    ident = [jnp.ones((TM, LANES), F32), jnp.zeros((TM, LANES), F32)]
    return jnp.stack([jnp.concatenate([tbl, idn], 0) for tbl, idn in zip(tabs, ident + ident)])


def _layer_weights(w_in, q_lat_norm, w_uq, kv_lat_norm, w_ukv, mla_q_gain, mla_k_gain,
                   gqa_q_gain, gqa_k_gain):
    split = MLA_Q_RANK + MLA_KV_RANK + MLA_ROPE
    perm_a = (np.arange(MLA_HEADS)[:, None] * LANES + _PARTNER_A[None, :]).reshape(-1)
    perm_b = (np.arange(GQA_HEADS // 2)[:, None] * LANES + _PARTNER_B[None, :]).reshape(-1)
    w_qb = w_in[:, split:split + GQA_HEADS * GQA_HEAD_DIM]
    w_kb = w_in[:, split + 512:split + 640]
    w_in_p = jnp.concatenate([w_in[:, :split], jnp.zeros((D_MODEL, 512 - split), F32),
                              w_in[:, split:], w_qb[:, perm_b], w_kb[:, _PARTNER_B]],
                             axis=1).astype(BF16)
    w_uq_p = jnp.pad(w_uq.reshape(MLA_Q_RANK, MLA_HEADS, MLA_QK),
                     ((0, 0), (0, 0), (0, LANES - MLA_QK))).reshape(MLA_Q_RANK, MLA_HEADS * LANES)
    w_uq_p = jnp.concatenate([w_uq_p, w_uq_p[:, perm_a]], axis=1)
    ukv = w_ukv.reshape(MLA_KV_RANK, MLA_HEADS, MLA_NOPE + MLA_V)
    w_k = jnp.pad(ukv[:, :, :MLA_NOPE], ((0, 0), (0, 0), (0, LANES - MLA_NOPE)))
    place = jnp.pad(jnp.eye(MLA_ROPE, dtype=F32), ((0, 0), (MLA_NOPE, LANES - MLA_QK)))
    place = jnp.broadcast_to(place[:, None, :], (MLA_ROPE, MLA_HEADS, LANES))
    w_k = jnp.concatenate([w_k, place,
                           jnp.zeros((256 - MLA_KV_RANK - MLA_ROPE, MLA_HEADS, LANES), F32)], 0)
    w_k = w_k.reshape(256, MLA_HEADS * LANES)
    w_v = jnp.pad(ukv[:, :, MLA_NOPE:].reshape(MLA_KV_RANK, MLA_HEADS * MLA_V),
                  ((0, 256 - MLA_KV_RANK), (0, 0)))
    w_kv = jnp.concatenate([w_k, w_k[:, perm_a], w_v], axis=1)
    glat = jnp.stack([q_lat_norm, jnp.pad(kv_lat_norm, (0, 256 - MLA_KV_RANK))])
    pad_qk = lambda g: jnp.pad(g, (0, LANES - MLA_QK))
    gains = [pad_qk(mla_q_gain) * (MLA_SCALE * LOG2E), pad_qk(mla_k_gain),
             jnp.tile(gqa_q_gain, 2) * (GQA_SCALE * LOG2E), jnp.tile(gqa_k_gain, 2)]
    partners = [_PARTNER_A, _PARTNER_A, _PARTNER_B, _PARTNER_B]
    g128 = jnp.stack([v for g, pm in zip(gains, partners) for v in (g, g[pm])])
    return w_in_p, glat, w_uq_p.astype(BF16), w_kv.astype(BF16), g128


def _mixer(x_lat, x_ctx, ctx_tile0, mod3, tab, norm_attn, w_in, q_lat_norm, w_uq, kv_lat_norm,
           w_ukv, mla_q_gain, mla_k_gain, gqa_q_gain, gqa_k_gain, with_ctx_queries, cast_a, cast_b):
    w_in_p, glat, w_uq_p, w_kv, g128 = _layer_weights(
        w_in, q_lat_norm, w_uq, kv_lat_norm, w_ukv, mla_q_gain, mla_k_gain, gqa_q_gain, gqa_k_gain)
    qa, ka, va, qb, kb, vb = _projection(x_lat, x_ctx, ctx_tile0, mod3,
                                         norm_attn.reshape(1, D_MODEL), w_in_p, glat,
                                         w_uq_p, w_kv, g128, tab)
    oa, oa_ctx, cast_a = _attention(qa, ka, va, _A_HEADS, _A_KV, with_ctx_queries, cast_a)
    ob, ob_ctx, cast_b = _attention(qb, kb, vb, _B_HEADS, _B_KV, with_ctx_queries, cast_b)
    return (oa, ob), ((oa_ctx, ob_ctx) if with_ctx_queries else None), (cast_a, cast_b)


def kernel(x, c, ctx, c_ctx, l0_w_mod, l0_b_mod, l0_norm_attn, l0_w_in, l0_q_lat_norm, l0_w_uq, l0_kv_lat_norm, l0_w_ukv, l0_mla_q_gain, l0_mla_k_gain, l0_gqa_q_gain, l0_gqa_k_gain, l0_w_out, l0_norm_ffn, l0_ffn_w_gate, l0_ffn_w_up, l0_ffn_w_down, l1_w_mod, l1_b_mod, l1_norm_attn, l1_w_in, l1_q_lat_norm, l1_w_uq, l1_kv_lat_norm, l1_w_ukv, l1_mla_q_gain, l1_mla_k_gain, l1_gqa_q_gain, l1_gqa_k_gain, l1_w_out, l1_norm_ffn, l1_router, l1_exp_w_gate, l1_exp_w_up, l1_exp_w_down):
    x_lat = x.reshape(N_LAT, D_MODEL)
    x_ctx = ctx.reshape(N_CTX, D_MODEL)
    cc = jnp.concatenate([c, c_ctx[None, :], jnp.zeros((16 - BATCH - 1, D_MODEL), F32)], axis=0)
    tab = _rope_tables()

    mod3 = _modulation(cc, l0_w_mod, l0_b_mod)
    o_lat, o_ctx, (wg_x, wu_x) = _mixer(
        x_lat, x_ctx, 0, mod3, tab, l0_norm_attn, l0_w_in, l0_q_lat_norm, l0_w_uq, l0_kv_lat_norm,
        l0_w_ukv, l0_mla_q_gain, l0_mla_k_gain, l0_gqa_q_gain, l0_gqa_k_gain, True,
        l1_exp_w_gate.reshape(N_EXPERTS * D_MODEL, EXPERT_FF),
        l1_exp_w_up.reshape(N_EXPERTS * D_MODEL, EXPERT_FF))
    xall = _post_dense_ffn(x_lat, x_ctx, o_lat, o_ctx, mod3, l0_norm_ffn.reshape(1, D_MODEL),
                           l0_w_out.astype(BF16), l0_ffn_w_gate.astype(BF16),
                           l0_ffn_w_up.astype(BF16), l0_ffn_w_down.astype(BF16))

    mod3 = _modulation(cc, l1_w_mod, l1_b_mod)
    o_lat, _, (wd_x, _) = _mixer(
        xall, xall, LAT_TILES, mod3, tab, l1_norm_attn, l1_w_in, l1_q_lat_norm, l1_w_uq,
        l1_kv_lat_norm, l1_w_ukv, l1_mla_q_gain, l1_mla_k_gain, l1_gqa_q_gain, l1_gqa_k_gain,
        False, l1_exp_w_down.reshape(N_EXPERTS * EXPERT_FF, D_MODEL), None)
    w_router = jnp.pad(l1_router, ((0, 0), (0, LANES - N_EXPERTS)))
    w_router_hi = w_router.astype(BF16)
    w_router = jnp.concatenate([w_router_hi, (w_router - w_router_hi.astype(F32)).astype(BF16)],
                               axis=1)
    x1, hrow, route, counts = _post_attention_router(xall, o_lat, mod3,
                                                     l1_norm_ffn.reshape(1, D_MODEL),
                                                     l1_w_out.astype(BF16), w_router)
    out = _moe_ffn(hrow, x1, route, counts, mod3,
                   wg_x.reshape(N_EXPERTS, D_MODEL, EXPERT_FF),
                   wu_x.reshape(N_EXPERTS, D_MODEL, EXPERT_FF),
                   wd_x.reshape(N_EXPERTS, EXPERT_FF, D_MODEL))
    return out.reshape(BATCH, SEQ, D_MODEL)
```

```python
import functools
import math

import jax
import jax.numpy as jnp
import numpy as np
from jax import lax
from jax.experimental import pallas as pl
from jax.experimental.pallas import tpu as pltpu

D_MODEL = 1024
BATCH = 8
SEQ = 2048
CTX_LEN = 256
GRID_W = 64
MLA_HEADS = 8
MLA_NOPE = 64
MLA_ROPE = 32
MLA_V = 64
MLA_QK = MLA_NOPE + MLA_ROPE
MLA_Q_RANK = 256
MLA_KV_RANK = 128
GQA_HEADS = 8
GQA_KV_HEADS = 2
GQA_HEAD_DIM = 64
DENSE_FF = 2816
N_EXPERTS = 8
EXPERT_FF = 2816
ROPE_THETA = 10000.0
NORM_EPS = 1e-6
LOG2E = math.log2(math.e)
MLA_SCALE = MLA_QK ** -0.5
GQA_SCALE = GQA_HEAD_DIM ** -0.5

LANES = 128
N_LAT = BATCH * SEQ
N_CTX = BATCH * CTX_LEN
N_TOK = N_LAT + N_CTX
TM = 512
LAT_TILES = N_LAT // TM
TOK_TILES = N_TOK // TM
SEQ_TILES = SEQ // TM
Q_CHUNK = 256
PROJ_CHUNK = 512
V_WIDTH = 2 * LANES
IN_PAD = 1920
FF_SPLITS = (0, 1536, 2816)
ROW_SUB = D_MODEL // LANES
TMX = 512
SORT_TILES = 2 * N_LAT // TMX + N_EXPERTS
SORT_ROWS = SORT_TILES * TMX
Y2_ROWS = 2 * N_LAT + 2 * TMX
VMEM_LIMIT = 56 * 1024 * 1024

F32 = jnp.float32
BF16 = jnp.bfloat16


def _silu(x):
    return x / (1.0 + jnp.exp(-x))


def _params(*sem):
    return pltpu.CompilerParams(dimension_semantics=sem, vmem_limit_bytes=VMEM_LIMIT)


def _mod_kernel(c_ref, w_ref, b_ref, o_ref):
    s = _silu(c_ref[...])
    o_ref[...] = jnp.dot(s, w_ref[...], precision=lax.Precision.HIGHEST,
                         preferred_element_type=F32) + b_ref[...]


def _modulation(cc, w_mod, b_mod):
    n = w_mod.shape[1]
    bn = 1024
    out = pl.pallas_call(
        _mod_kernel,
        grid=(n // bn,),
        in_specs=[pl.BlockSpec((16, D_MODEL), lambda j: (0, 0)),
                  pl.BlockSpec((D_MODEL, bn), lambda j: (0, j)),
                  pl.BlockSpec((1, bn), lambda j: (0, j))],
        out_specs=pl.BlockSpec((16, bn), lambda j: (0, j)),
        out_shape=jax.ShapeDtypeStruct((16, n), F32),
        compiler_params=_params("parallel"),
        name="modulation",
    )(cc, w_mod, b_mod.reshape(1, n))
    return out.reshape(16, 6, D_MODEL)


def _tile_rows(lat_ref, ctx_ref):
    return jnp.where(pl.program_id(0) < LAT_TILES, lat_ref[...], ctx_ref[...])


def _proj_kernel(xl_ref, xc_ref, mod_ref, gn_ref, win_ref, glat_ref, wuq_ref, wkv_ref, g128_ref,
                 tab_ref, qa_ref, ka_ref, va_ref, qb_ref, kb_ref, vb_ref):
    latent = pl.program_id(0) < LAT_TILES
    n = PROJ_CHUNK
    lane = lax.broadcasted_iota(jnp.int32, (n, LANES), 1)
    lo = lane < 64
    ones_col = jnp.where(lane == 0, 1.0, 0.0).astype(BF16)

    for r0 in range(0, TM, n):
        rows = slice(r0, r0 + n)
        x = jnp.where(latent, xl_ref[rows, :], xc_ref[rows, :])
        r = lax.rsqrt(jnp.mean(x * x, axis=-1, keepdims=True) + NORM_EPS)
        h = (x * r * gn_ref[...]) * (1.0 + mod_ref[0, 1:2, :]) + mod_ref[0, 0:1, :]
        p = jnp.dot(h.astype(BF16), win_ref[...], preferred_element_type=F32)

        def tables(base, row):
            return (tab_ref[base, rows, :] * g128_ref[row:row + 1, :],
                    tab_ref[base + 1, rows, :] * g128_ref[row + 1:row + 2, :])

        cq = p[:, 0:256]
        rq = lax.rsqrt(jnp.mean(cq * cq, axis=-1, keepdims=True) + NORM_EPS)
        qa = jnp.dot((cq * rq * glat_ref[0:1, :]).astype(BF16), wuq_ref[...],
                     preferred_element_type=F32)
        ckv = p[:, 256:384]
        rkv = lax.rsqrt(jnp.mean(ckv * ckv, axis=-1, keepdims=True) + NORM_EPS)
        slab = jnp.concatenate([ckv * rkv * glat_ref[1:2, 0:128], p[:, 384:512]], axis=-1)
        kv = jnp.dot(slab.astype(BF16), wkv_ref[...], preferred_element_type=F32)

        def head_a(src, hd, cos_g, sin_g):
            blk = src[:, hd * LANES:(hd + 1) * LANES]
            partner = src[:, 1024 + hd * LANES:1024 + (hd + 1) * LANES]
            rr = lax.rsqrt(jnp.sum(blk * blk, axis=-1, keepdims=True) * (1.0 / MLA_QK) + NORM_EPS)
            return ((blk * cos_g + partner * sin_g) * rr).astype(BF16)

        cq_g, sq_g = tables(0, 0)
        ck_g, sk_g = tables(0, 2)
        for hd in range(MLA_HEADS):
            qa_ref[hd, rows, :] = head_a(qa, hd, cq_g, sq_g)
            ka_ref[hd, rows, :] = head_a(kv, hd, ck_g, sk_g)
        for pr in range(MLA_HEADS // 2):
            blk = kv[:, 2048 + pr * LANES:2048 + (pr + 1) * LANES]
            va_ref[2 * pr, rows, 0:LANES] = jnp.where(lo, blk, 0.0).astype(BF16)
            va_ref[2 * pr + 1, rows, 0:LANES] = jnp.where(lo, 0.0, blk).astype(BF16)
        for hd in range(MLA_HEADS):
            va_ref[hd, rows, LANES:V_WIDTH] = ones_col

        def pair_b(blk, partner, cos_g, sin_g):
            sq = blk * blk
            s_lo = jnp.sum(jnp.where(lo, sq, 0.0), axis=-1, keepdims=True)
            s_hi = jnp.sum(jnp.where(lo, 0.0, sq), axis=-1, keepdims=True)
            rr = jnp.where(lo, lax.rsqrt(s_lo * (1.0 / GQA_HEAD_DIM) + NORM_EPS),
                           lax.rsqrt(s_hi * (1.0 / GQA_HEAD_DIM) + NORM_EPS))
            return (blk * cos_g + partner * sin_g) * rr

        cq_g, sq_g = tables(2, 4)
        ck_g, sk_g = tables(2, 6)
        for pr in range(GQA_HEADS // 2):
            blk = p[:, 512 + pr * LANES:512 + (pr + 1) * LANES]
            partner = p[:, 1280 + pr * LANES:1280 + (pr + 1) * LANES]
            qb_ref[pr, rows, :] = pair_b(blk, partner, cq_g, sq_g).astype(BF16)
        kb = pair_b(p[:, 1024:1152], p[:, 1792:1920], ck_g, sk_g)
        kb_sw = pltpu.roll(kb, 64, 1)
        kb_ref[0, rows, :] = jnp.where(lo, kb, 0.0).astype(BF16)
        kb_ref[1, rows, :] = jnp.where(lo, 0.0, kb_sw).astype(BF16)
        kb_ref[2, rows, :] = jnp.where(lo, kb_sw, 0.0).astype(BF16)
        kb_ref[3, rows, :] = jnp.where(lo, 0.0, kb).astype(BF16)
        vb = p[:, 1152:1280]
        vb_sw = pltpu.roll(vb, 64, 1)
        vb_ref[0, rows, 0:LANES] = jnp.where(lo, vb, 0.0).astype(BF16)
        vb_ref[1, rows, 0:LANES] = jnp.where(lo, 0.0, vb_sw).astype(BF16)
        vb_ref[2, rows, 0:LANES] = jnp.where(lo, vb_sw, 0.0).astype(BF16)
        vb_ref[3, rows, 0:LANES] = jnp.where(lo, 0.0, vb).astype(BF16)
        for j in range(2 * GQA_KV_HEADS):
            vb_ref[j, rows, LANES:V_WIDTH] = ones_col


def _mod_index(i):
    return jnp.where(i < LAT_TILES, i // SEQ_TILES, BATCH)


def _lat_ctx_specs(width, ctx_tile0):
    return [pl.BlockSpec((TM, width), lambda i: (jnp.minimum(i, LAT_TILES - 1), 0)),
            pl.BlockSpec((TM, width), lambda i: (ctx_tile0 + jnp.maximum(i - LAT_TILES, 0), 0))]


def _projection(x_lat, x_ctx, ctx_tile0, mod3, gn, w_in, glat, w_uq, w_kv, g128, tab):
    const2 = lambda i: (0, 0)
    head_out = lambda n, w: pl.BlockSpec((n, TM, w), lambda i: (0, i, 0))
    head_shape = lambda n, w: jax.ShapeDtypeStruct((n, N_TOK, w), BF16)
    return pl.pallas_call(
        _proj_kernel,
        grid=(TOK_TILES,),
        in_specs=_lat_ctx_specs(D_MODEL, ctx_tile0) + [
                  pl.BlockSpec((1, 6, D_MODEL), lambda i: (_mod_index(i), 0, 0)),
                  pl.BlockSpec((1, D_MODEL), const2),
                  pl.BlockSpec((D_MODEL, IN_PAD), const2),
                  pl.BlockSpec((2, 256), const2),
                  pl.BlockSpec((MLA_Q_RANK, 2 * MLA_HEADS * LANES), const2),
                  pl.BlockSpec((256, 2 * MLA_HEADS * LANES + MLA_HEADS * MLA_V), const2),
                  pl.BlockSpec((8, LANES), const2),
                  pl.BlockSpec((4, TM, LANES),
                               lambda i: (0, jnp.where(i < LAT_TILES, i % SEQ_TILES, SEQ_TILES), 0))],
        out_specs=[head_out(8, LANES), head_out(8, LANES), head_out(8, V_WIDTH),
                   head_out(4, LANES), head_out(4, LANES), head_out(4, V_WIDTH)],
        out_shape=[head_shape(8, LANES), head_shape(8, LANES), head_shape(8, V_WIDTH),
                   head_shape(4, LANES), head_shape(4, LANES), head_shape(4, V_WIDTH)],
        compiler_params=_params("parallel"),
        name="projection",
    )(x_lat, x_ctx, mod3, gn, w_in, glat, w_uq, w_kv, g128, tab)


def _pair_rows(q_refs, r0, n, kv_refs):
    nt = (((1,), (1,)), ((), ()))
    out = None
    for u, q_ref in enumerate(q_refs):
        q = q_ref[r0:r0 + n, :]
        scores = [lax.dot_general(q, k_ref[u], nt, preferred_element_type=F32)
                  for k_ref, _ in kv_refs]
        m = functools.reduce(jnp.maximum, [jnp.max(sc, axis=-1, keepdims=True) for sc in scores])
        acc = None
        for sc, (_, v_ref) in zip(scores, kv_refs):
            part = jnp.dot(jnp.exp2((sc - m).astype(BF16)), v_ref[u], preferred_element_type=F32)
            acc = part if acc is None else acc + part
        o = acc[:, 0:LANES] / acc[:, LANES:LANES + 1]
        out = o if out is None else out + o
    return out


def _attn_kernel(*refs, ctx_queries, n_casts):
    q0_ref, q1_ref, kl_ref, kc_ref, vl_ref, vc_ref = refs[:6]
    rest = list(refs[6:])
    qc_refs = (rest.pop(0), rest.pop(0)) if ctx_queries else None
    src_refs = [rest.pop(0) for _ in range(n_casts)]
    o_ref = rest.pop(0)
    oc_ref = rest.pop(0) if ctx_queries else None
    for src_ref in src_refs:
        rest.pop(0)[...] = src_ref[...].astype(BF16)
    for r0 in range(0, SEQ, Q_CHUNK):
        out = _pair_rows((q0_ref, q1_ref), r0, Q_CHUNK, ((kc_ref, vc_ref), (kl_ref, vl_ref)))
        o_ref[r0:r0 + Q_CHUNK, :] = out.astype(o_ref.dtype)
    if ctx_queries:
        oc_ref[...] = _pair_rows(qc_refs, 0, CTX_LEN, ((kc_ref, vc_ref),)).astype(oc_ref.dtype)


def _attention(q, k, v, q_heads, kv_pair, ctx_queries, cast_srcs):
    n_units = 4
    ctx_blk0 = N_LAT // CTX_LEN

    def qspec(which, rows, blk0):
        return pl.BlockSpec((None, rows, LANES), lambda b, p: (q_heads(p)[which], blk0 + b, 0))

    def kvspec(rows, blk0, width):
        return pl.BlockSpec((2, rows, width), lambda b, p: (kv_pair(p), blk0 + b, 0))

    in_specs = [qspec(0, SEQ, 0), qspec(1, SEQ, 0), kvspec(SEQ, 0, LANES),
                kvspec(CTX_LEN, ctx_blk0, LANES), kvspec(SEQ, 0, V_WIDTH),
                kvspec(CTX_LEN, ctx_blk0, V_WIDTH)]
    args = [q, q, k, k, v, v]
    out_specs = [pl.BlockSpec((SEQ, LANES), lambda b, p: (b, p))]
    out_shape = [jax.ShapeDtypeStruct((N_LAT, n_units * LANES), BF16)]
    if ctx_queries:
        in_specs += [qspec(0, CTX_LEN, ctx_blk0), qspec(1, CTX_LEN, ctx_blk0)]
        args += [q, q]
        out_specs.append(pl.BlockSpec((CTX_LEN, LANES), lambda b, p: (b, p)))
        out_shape.append(jax.ShapeDtypeStruct((N_CTX, n_units * LANES), BF16))
    for src in cast_srcs:
        rows, cols = src.shape
        slab = pl.BlockSpec((rows // (BATCH * n_units), cols), lambda b, p: (b * n_units + p, 0))
        in_specs.append(slab)
        args.append(src)
        out_specs.append(slab)
        out_shape.append(jax.ShapeDtypeStruct((rows, cols), BF16))
    outs = list(pl.pallas_call(
        functools.partial(_attn_kernel, ctx_queries=ctx_queries, n_casts=len(cast_srcs)),
        grid=(BATCH, n_units),
        in_specs=in_specs, out_specs=out_specs, out_shape=out_shape,
        compiler_params=_params("parallel", "parallel"),
        name="attention",
    )(*args))
    o_lat = outs.pop(0)
    o_ctx = outs.pop(0) if ctx_queries else None
    return o_lat, o_ctx, outs


_A_HEADS = lambda p: (2 * p, 2 * p + 1)
_A_KV = lambda p: p
_B_HEADS = lambda p: (p, p)
_B_KV = lambda p: p // 2


def _post_body(x, oa, ob, mod_ref, gn_ref, wo_ref):
    y = (jnp.dot(oa, wo_ref[0:512, :], preferred_element_type=F32)
         + jnp.dot(ob, wo_ref[512:1024, :], preferred_element_type=F32))
    x1 = x + mod_ref[0, 2:3, :] * y
    r = lax.rsqrt(jnp.mean(x1 * x1, axis=-1, keepdims=True) + NORM_EPS)
    h2 = (x1 * r * gn_ref[...]) * (1.0 + mod_ref[0, 4:5, :]) + mod_ref[0, 3:4, :]
    return x1, h2


def _swiglu(h, wg_ref, wu_ref, wd_ref):
    y = None
    for lo, hi in zip(FF_SPLITS[:-1], FF_SPLITS[1:]):
        g = jnp.dot(h, wg_ref[:, lo:hi], preferred_element_type=F32)
        u = jnp.dot(h, wu_ref[:, lo:hi], preferred_element_type=F32)
        yc = jnp.dot((_silu(g) * u).astype(BF16), wd_ref[lo:hi, :], preferred_element_type=F32)
        y = yc if y is None else y + yc
    return y


def _post_ffn_kernel(xl_ref, xc_ref, oa_ref, oac_ref, ob_ref, obc_ref, mod_ref, gn_ref, wo_ref,
                     wg_ref, wu_ref, wd_ref, o_ref):
    x1, h2 = _post_body(_tile_rows(xl_ref, xc_ref), _tile_rows(oa_ref, oac_ref),
                        _tile_rows(ob_ref, obc_ref), mod_ref, gn_ref, wo_ref)
    y = _swiglu(h2.astype(BF16), wg_ref, wu_ref, wd_ref)
    o_ref[...] = x1 + mod_ref[0, 5:6, :] * y


def _post_router_kernel(x_ref, oa_ref, ob_ref, mod_ref, gn_ref, wo_ref, wr_ref,
                        x1_ref, hrow_ref, route_ref, count_ref, carry_ref):
    x1, h2 = _post_body(x_ref[...], oa_ref[...], ob_ref[...], mod_ref, gn_ref, wo_ref)
    x1_ref[...] = x1
    for s in range(ROW_SUB):
        hrow_ref[pl.ds(s, TM, stride=ROW_SUB), :] = h2[:, s * LANES:(s + 1) * LANES]

    h_hi = h2.astype(BF16)
    h_lo = (h2 - h_hi.astype(F32)).astype(BF16)
    part = jnp.dot(h_hi, wr_ref[...], preferred_element_type=F32)
    logits = (part[:, 0:LANES] + part[:, LANES:2 * LANES]
              + jnp.dot(h_lo, wr_ref[:, 0:LANES], preferred_element_type=F32))
    lane = lax.broadcasted_iota(jnp.int32, logits.shape, 1)
    valid = lane < N_EXPERTS
    lg = jnp.where(valid, logits, -jnp.inf)
    e = jnp.exp(lg - jnp.max(lg, axis=-1, keepdims=True))
    probs = e / jnp.sum(e, axis=-1, keepdims=True)
    v1 = jnp.max(probs, axis=-1, keepdims=True)
    i1 = jnp.min(jnp.where(probs == v1, lane, LANES), axis=-1, keepdims=True)
    rest = jnp.where(valid & (lane != i1), probs, -1.0)
    v2 = jnp.max(rest, axis=-1, keepdims=True)
    i2 = jnp.min(jnp.where(rest == v2, lane, LANES), axis=-1, keepdims=True)

    @pl.when(pl.program_id(0) == 0)
    def _():
        carry_ref[...] = jnp.zeros_like(carry_ref)

    pick1 = lane == i1
    pick2 = lane == i2
    onehot = jnp.where(pick1, 1.0, 0.0) + jnp.where(pick2, 1.0, 0.0)
    row = lax.broadcasted_iota(jnp.int32, (TM, TM), 0)
    col = lax.broadcasted_iota(jnp.int32, (TM, TM), 1)
    before = jnp.where(col < row, 1.0, 0.0).astype(BF16)
    seen = jnp.dot(before, onehot.astype(BF16), preferred_element_type=F32) + carry_ref[0:1, :]
    rank1 = jnp.sum(jnp.where(pick1, seen, 0.0), axis=-1, keepdims=True)
    rank2 = jnp.sum(jnp.where(pick2, seen, 0.0), axis=-1, keepdims=True)
    carry_ref[...] = carry_ref[...] + jnp.sum(onehot, axis=0, keepdims=True)
    count_ref[...] = carry_ref[...]

    fields = (v1 / (v1 + v2), v2 / (v1 + v2), i1.astype(F32), i2.astype(F32), rank1, rank2)
    route = jnp.zeros((TM, LANES), F32)
    for k, val in enumerate(fields):
        route = jnp.where(lane == k, val, route)
    route_ref[...] = route


def _post_specs():
    const2 = lambda i: (0, 0)
    return [pl.BlockSpec((1, 6, D_MODEL), lambda i: (_mod_index(i), 0, 0)),
            pl.BlockSpec((1, D_MODEL), const2),
            pl.BlockSpec((D_MODEL, D_MODEL), const2, pipeline_mode=pl.Buffered(1))]


def _post_dense_ffn(x_lat, x_ctx, o_lat, o_ctx, mod3, gn, w_out, wg, wu, wd):
    resident = lambda shape: pl.BlockSpec(shape, lambda i: (0, 0), pipeline_mode=pl.Buffered(1))
    return pl.pallas_call(
        _post_ffn_kernel, grid=(TOK_TILES,),
        in_specs=(_lat_ctx_specs(D_MODEL, 0) + _lat_ctx_specs(512, 0) + _lat_ctx_specs(512, 0)
                  + _post_specs() + [resident((D_MODEL, DENSE_FF)), resident((D_MODEL, DENSE_FF)),
                                     resident((DENSE_FF, D_MODEL))]),
        out_specs=pl.BlockSpec((TM, D_MODEL), lambda i: (i, 0)),
        out_shape=jax.ShapeDtypeStruct((N_TOK, D_MODEL), F32),
        compiler_params=_params("parallel"), name="post_attention_ffn",
    )(x_lat, x_ctx, o_lat[0], o_ctx[0], o_lat[1], o_ctx[1], mod3, gn, w_out, wg, wu, wd)


def _post_attention_router(x_lat, o_lat, mod3, gn, w_out, w_router):
    const2 = lambda i: (0, 0)
    n_tiles = LAT_TILES
    rows = n_tiles * TM
    x1_spec = pl.BlockSpec((TM, D_MODEL), lambda i: (i, 0))
    x1_shape = jax.ShapeDtypeStruct((rows, D_MODEL), F32)
    half_spec = pl.BlockSpec((TM, 512), lambda i: (i, 0))
    in_specs = [x1_spec, half_spec, half_spec] + _post_specs()
    args = [x_lat, *o_lat, mod3, gn, w_out]
    return pl.pallas_call(
        _post_router_kernel, grid=(n_tiles,),
        in_specs=in_specs + [pl.BlockSpec((D_MODEL, 2 * LANES), const2)],
        out_specs=[x1_spec,
                   pl.BlockSpec((TM * ROW_SUB, LANES), lambda i: (i, 0)),
                   pl.BlockSpec((TM, LANES), lambda i: (i, 0)),
                   pl.BlockSpec((8, LANES), const2)],
        out_shape=[x1_shape,
                   jax.ShapeDtypeStruct((rows * ROW_SUB, LANES), F32),
                   jax.ShapeDtypeStruct((rows, LANES), F32),
                   jax.ShapeDtypeStruct((8, LANES), F32)],
        scratch_shapes=[pltpu.VMEM((8, LANES), F32)],
        compiler_params=_params("arbitrary"), name="post_attention_router",
    )(*args, w_router)


def _row(ref, r):
    return ref.at[pl.ds(pl.multiple_of(r * ROW_SUB, ROW_SUB), ROW_SUB)]


def _rows_to_matrix(ref, n):
    return jnp.concatenate([ref[pl.ds(s, n, stride=ROW_SUB), :] for s in range(ROW_SUB)], axis=-1)


def _tile_row(buf, j):
    return buf.at[j // 8, :, j % 8, :]


def _invert_kernel(gaps_ref, dest_ref, inv_ref):
    def clear(r, carry):
        inv_ref[r] = -1
        return carry

    def place(a, carry):
        inv_ref[dest_ref[a]] = a
        return carry

    for e in range(N_EXPERTS):
        lax.fori_loop(gaps_ref[2 * e], gaps_ref[2 * e + 1], clear, 0)
    lax.fori_loop(gaps_ref[2 * N_EXPERTS], SORT_ROWS, clear, 0)
    lax.fori_loop(0, 2 * N_LAT, place, 0, unroll=16)


def _invert(gaps, dest_flat):
    return pl.pallas_call(
        _invert_kernel,
        in_specs=[pl.BlockSpec(memory_space=pltpu.SMEM), pl.BlockSpec(memory_space=pltpu.SMEM)],
        out_specs=pl.BlockSpec(memory_space=pltpu.SMEM),
        out_shape=jax.ShapeDtypeStruct((SORT_ROWS,), jnp.int32),
        name="moe_invert",
    )(gaps, dest_flat)


def _expert_kernel(te_ref, nu_ref, inv_ref, h_ref, wg_ref, wu_ref, wd_ref, y2_ref,
                   xbuf, ybuf, sem_g, sem_s):
    del te_ref
    t = pl.program_id(0)
    n_used = nu_ref[0]
    slot = lax.rem(t, 2)
    other = 1 - slot

    def gather_rows(tile, s):
        base = tile * TMX
        for j in range(TMX):
            token = lax.shift_right_logical(jnp.maximum(inv_ref[base + j], 0), 1)
            pltpu.make_async_copy(_row(h_ref, token), _tile_row(xbuf.at[s], j),
                                  sem_g.at[s]).start(priority=j % 2)

    def scatter_rows(tile, s, real, scratch_block):
        base = tile * TMX
        for j in range(TMX):
            code = inv_ref[base + j]
            row = jnp.where(jnp.logical_and(code >= 0, real),
                            (code & 1) * N_LAT + lax.shift_right_logical(code, 1),
                            2 * N_LAT + scratch_block * TMX + j)
            pltpu.make_async_copy(_tile_row(ybuf.at[s], j), _row(y2_ref, row),
                                  sem_s.at[s]).start(priority=j % 2)

    def wait_gather(s):
        pltpu.make_async_copy(xbuf.at[s], xbuf.at[s], sem_g.at[s]).wait()

    def wait_scatter(s):
        pltpu.make_async_copy(ybuf.at[s], ybuf.at[s], sem_s.at[s]).wait()

    @pl.when(t == 0)
    def _():
        ybuf[...] = jnp.zeros_like(ybuf)
        gather_rows(0, 0)

    @pl.when(t < n_used)
    def _():
        wait_gather(slot)
        x = jnp.concatenate([xbuf[slot, :, m].reshape(TMX, LANES) for m in range(ROW_SUB)],
                            axis=-1).astype(BF16)
        gather_rows(jnp.minimum(t + 1, n_used - 1), other)
        scatter_rows(jnp.maximum(t - 1, 0), other, t > 0, other)
        y = _swiglu(x, wg_ref.at[0], wu_ref.at[0], wd_ref.at[0])

        @pl.when(t > 0)
        def _():
            wait_scatter(slot)

        for m in range(ROW_SUB):
            ybuf[slot, :, m] = y[:, m * LANES:(m + 1) * LANES].reshape(TMX // 8, 8, LANES)

    @pl.when(t == n_used)
    def _():
        wait_gather(slot)
        scatter_rows(n_used - 1, other, True, other)
        wait_scatter(other)
        wait_scatter(slot)
        scatter_rows(n_used - 1, slot, False, 0)
        wait_scatter(slot)


def _experts(tile_expert, n_used, inv, hrow, wg, wu, wd):
    one = pl.Buffered(1)
    wspec = lambda shape: pl.BlockSpec(shape, lambda t, te, nu, iv: (te[t], 0, 0), pipeline_mode=one)
    grid_spec = pltpu.PrefetchScalarGridSpec(
        num_scalar_prefetch=3,
        grid=(SORT_TILES + 1,),
        in_specs=[pl.BlockSpec(memory_space=pl.ANY),
                  wspec((1, D_MODEL, EXPERT_FF)), wspec((1, D_MODEL, EXPERT_FF)),
                  wspec((1, EXPERT_FF, D_MODEL))],
        out_specs=pl.BlockSpec(memory_space=pl.ANY),
        scratch_shapes=[pltpu.VMEM((2, TMX // 8, ROW_SUB, 8, LANES), F32),
                        pltpu.VMEM((2, TMX // 8, ROW_SUB, 8, LANES), F32),
                        pltpu.SemaphoreType.DMA((2,)), pltpu.SemaphoreType.DMA((2,))])
    return pl.pallas_call(
        _expert_kernel,
        grid_spec=grid_spec,
        out_shape=jax.ShapeDtypeStruct((Y2_ROWS * ROW_SUB, LANES), F32),
        compiler_params=_params("arbitrary"),
        name="moe_experts",
    )(tile_expert, n_used, inv, hrow, wg, wu, wd)


def _combine_kernel(ya_ref, yb_ref, x1_ref, route_ref, mod_ref, o_ref):
    route = route_ref[...]
    y = (route[:, 0:1] * _rows_to_matrix(ya_ref, TM) + route[:, 1:2] * _rows_to_matrix(yb_ref, TM))
    o_ref[...] = x1_ref[...] + mod_ref[0, 5:6, :] * y


def _combine(y2, x1, route, mod3):
    return pl.pallas_call(
        _combine_kernel,
        grid=(LAT_TILES,),
        in_specs=[pl.BlockSpec((TM * ROW_SUB, LANES), lambda i: (i, 0)),
                  pl.BlockSpec((TM * ROW_SUB, LANES), lambda i: (LAT_TILES + i, 0)),
                  pl.BlockSpec((TM, D_MODEL), lambda i: (i, 0)),
                  pl.BlockSpec((TM, LANES), lambda i: (i, 0)),
                  pl.BlockSpec((1, 6, D_MODEL), lambda i: (i // SEQ_TILES, 0, 0))],
        out_specs=pl.BlockSpec((TM, D_MODEL), lambda i: (i, 0)),
        out_shape=jax.ShapeDtypeStruct((N_LAT, D_MODEL), F32),
        compiler_params=_params("parallel"),
        name="moe_combine",
    )(y2, y2, x1, route, mod3)


def _moe_ffn(hrow, x1, route, counts, mod3, wg, wu, wd):
    expert = route[:, 2:4].astype(jnp.int32)
    rank = route[:, 4:6].astype(jnp.int32)
    count = counts[0, :N_EXPERTS].astype(jnp.int32)
    tiles = (count + TMX - 1) // TMX
    tile_end = jnp.cumsum(tiles)
    dest = (tile_end - tiles)[expert] * TMX + rank
    gaps = jnp.stack([(tile_end - tiles) * TMX + count, tile_end * TMX], axis=1).reshape(-1)
    inv = _invert(jnp.concatenate([gaps, tile_end[N_EXPERTS - 1:] * TMX]), dest.reshape(-1))
    n_used = tile_end[N_EXPERTS - 1:]
    tile_ids = jnp.minimum(jnp.arange(SORT_TILES + 1), n_used[0] - 1)
    tile_expert = jnp.sum(tile_ids[:, None] >= tile_end[None, :], axis=1).astype(jnp.int32)
    y2 = _experts(tile_expert, n_used.astype(jnp.int32), inv, hrow, wg, wu, wd)
    return _combine(y2, x1, route, mod3)


def _partner(half):
    idx = np.arange(4 * half)
    return np.where((idx // half) % 2 == 0, idx + half, idx - half)


_PARTNER_A = np.concatenate([np.arange(MLA_NOPE), MLA_NOPE + _partner(MLA_ROPE // 4),
                             np.arange(MLA_QK, LANES)])
_PARTNER_B = np.concatenate([_partner(GQA_HEAD_DIM // 4), GQA_HEAD_DIM + _partner(GQA_HEAD_DIM // 4)])


def _rope_tables():
    f32 = np.float32
    t = np.arange(SEQ)
    row = (t // GRID_W).astype(f32)
    col = (t % GRID_W).astype(f32)

    def one_axis(pos, half):
        freqs = f32(ROPE_THETA) ** (-np.arange(half, dtype=f32) / f32(half))
        ang = pos[:, None] * freqs[None, :]
        cos, sin = np.cos(ang), np.sin(ang)
        return np.concatenate([cos, cos], -1), np.concatenate([-sin, sin], -1)

    def two_axes(half):
        r, c = one_axis(row, half), one_axis(col, half)
        return [np.concatenate([a, b], -1) for a, b in zip(r, c)]

    def pad_a(tbl, fill):
        return np.concatenate([np.full((SEQ, MLA_NOPE), fill, f32), tbl,
                               np.full((SEQ, LANES - MLA_QK), fill, f32)], -1)

    ta = two_axes(MLA_ROPE // 4)
    tb = [np.concatenate([x, x], -1) for x in two_axes(GQA_HEAD_DIM // 4)]
    tabs = [pad_a(ta[0], 1.0), pad_a(ta[1], 0.0)] + tb
    ident = [np.ones((TM, LANES), f32), np.zeros((TM, LANES), f32)]
    return np.stack([np.concatenate([tbl, idn], 0)
                     for tbl, idn in zip(tabs, ident + ident)]).astype(f32)


def _layer_weights(w_in, q_lat_norm, w_uq, kv_lat_norm, w_ukv, mla_q_gain, mla_k_gain,
                   gqa_q_gain, gqa_k_gain):
    split = MLA_Q_RANK + MLA_KV_RANK + MLA_ROPE
    perm_a = (np.arange(MLA_HEADS)[:, None] * LANES + _PARTNER_A[None, :]).reshape(-1)
    perm_b = (np.arange(GQA_HEADS // 2)[:, None] * LANES + _PARTNER_B[None, :]).reshape(-1)
    w_qb = w_in[:, split:split + GQA_HEADS * GQA_HEAD_DIM]
    w_kb = w_in[:, split + 512:split + 640]
    w_in_p = jnp.concatenate([w_in[:, :split], jnp.zeros((D_MODEL, 512 - split), F32),
                              w_in[:, split:], w_qb[:, perm_b], w_kb[:, _PARTNER_B]],
                             axis=1).astype(BF16)
    w_uq_p = jnp.pad(w_uq.reshape(MLA_Q_RANK, MLA_HEADS, MLA_QK),
                     ((0, 0), (0, 0), (0, LANES - MLA_QK))).reshape(MLA_Q_RANK, MLA_HEADS * LANES)
    w_uq_p = jnp.concatenate([w_uq_p, w_uq_p[:, perm_a]], axis=1)
    ukv = w_ukv.reshape(MLA_KV_RANK, MLA_HEADS, MLA_NOPE + MLA_V)
    w_k = jnp.pad(ukv[:, :, :MLA_NOPE], ((0, 0), (0, 0), (0, LANES - MLA_NOPE)))
    place = jnp.pad(jnp.eye(MLA_ROPE, dtype=F32), ((0, 0), (MLA_NOPE, LANES - MLA_QK)))
    place = jnp.broadcast_to(place[:, None, :], (MLA_ROPE, MLA_HEADS, LANES))
    w_k = jnp.concatenate([w_k, place,
                           jnp.zeros((256 - MLA_KV_RANK - MLA_ROPE, MLA_HEADS, LANES), F32)], 0)
    w_k = w_k.reshape(256, MLA_HEADS * LANES)
    w_v = jnp.pad(ukv[:, :, MLA_NOPE:].reshape(MLA_KV_RANK, MLA_HEADS * MLA_V),
                  ((0, 256 - MLA_KV_RANK), (0, 0)))
    w_kv = jnp.concatenate([w_k, w_k[:, perm_a], w_v], axis=1)
    glat = jnp.stack([q_lat_norm, jnp.pad(kv_lat_norm, (0, 256 - MLA_KV_RANK))])
    pad_qk = lambda g: jnp.pad(g, (0, LANES - MLA_QK))
    gains = [pad_qk(mla_q_gain) * (MLA_SCALE * LOG2E), pad_qk(mla_k_gain),
             jnp.tile(gqa_q_gain, 2) * (GQA_SCALE * LOG2E), jnp.tile(gqa_k_gain, 2)]
    partners = [_PARTNER_A, _PARTNER_A, _PARTNER_B, _PARTNER_B]
    g128 = jnp.stack([v for g, pm in zip(gains, partners) for v in (g, g[pm])])
    return w_in_p, glat, w_uq_p.astype(BF16), w_kv.astype(BF16), g128


def _mixer(x_lat, x_ctx, ctx_tile0, mod3, tab, norm_attn, w_in, q_lat_norm, w_uq, kv_lat_norm,
           w_ukv, mla_q_gain, mla_k_gain, gqa_q_gain, gqa_k_gain, with_ctx_queries, cast_a, cast_b):
    w_in_p, glat, w_uq_p, w_kv, g128 = _layer_weights(
        w_in, q_lat_norm, w_uq, kv_lat_norm, w_ukv, mla_q_gain, mla_k_gain, gqa_q_gain, gqa_k_gain)
    qa, ka, va, qb, kb, vb = _projection(x_lat, x_ctx, ctx_tile0, mod3,
                                         norm_attn.reshape(1, D_MODEL), w_in_p, glat,
                                         w_uq_p, w_kv, g128, tab)
    oa, oa_ctx, cast_a = _attention(qa, ka, va, _A_HEADS, _A_KV, with_ctx_queries, cast_a)
    ob, ob_ctx, cast_b = _attention(qb, kb, vb, _B_HEADS, _B_KV, with_ctx_queries, cast_b)
    return (oa, ob), ((oa_ctx, ob_ctx) if with_ctx_queries else None), cast_a + cast_b


def kernel(x, c, ctx, c_ctx, l0_w_mod, l0_b_mod, l0_norm_attn, l0_w_in, l0_q_lat_norm, l0_w_uq, l0_kv_lat_norm, l0_w_ukv, l0_mla_q_gain, l0_mla_k_gain, l0_gqa_q_gain, l0_gqa_k_gain, l0_w_out, l0_norm_ffn, l0_ffn_w_gate, l0_ffn_w_up, l0_ffn_w_down, l1_w_mod, l1_b_mod, l1_norm_attn, l1_w_in, l1_q_lat_norm, l1_w_uq, l1_kv_lat_norm, l1_w_ukv, l1_mla_q_gain, l1_mla_k_gain, l1_gqa_q_gain, l1_gqa_k_gain, l1_w_out, l1_norm_ffn, l1_router, l1_exp_w_gate, l1_exp_w_up, l1_exp_w_down):
    x_lat = x.reshape(N_LAT, D_MODEL)
    x_ctx = ctx.reshape(N_CTX, D_MODEL)
    cc = jnp.concatenate([c, c_ctx[None, :], jnp.zeros((16 - BATCH - 1, D_MODEL), F32)], axis=0)
    tab = jnp.asarray(_rope_tables())

    mod3 = _modulation(cc, l0_w_mod, l0_b_mod)
    o_lat, o_ctx, (wg_x, wg_0, wu_x, wu_0) = _mixer(
        x_lat, x_ctx, 0, mod3, tab, l0_norm_attn, l0_w_in, l0_q_lat_norm, l0_w_uq, l0_kv_lat_norm,
        l0_w_ukv, l0_mla_q_gain, l0_mla_k_gain, l0_gqa_q_gain, l0_gqa_k_gain, True,
        [l1_exp_w_gate.reshape(N_EXPERTS * D_MODEL, EXPERT_FF), l0_ffn_w_gate],
        [l1_exp_w_up.reshape(N_EXPERTS * D_MODEL, EXPERT_FF), l0_ffn_w_up])
    xall = _post_dense_ffn(x_lat, x_ctx, o_lat, o_ctx, mod3, l0_norm_ffn.reshape(1, D_MODEL),
                           l0_w_out.astype(BF16), wg_0, wu_0, l0_ffn_w_down.astype(BF16))

    mod3 = _modulation(cc, l1_w_mod, l1_b_mod)
    o_lat, _, (wd_x,) = _mixer(
        xall, xall, LAT_TILES, mod3, tab, l1_norm_attn, l1_w_in, l1_q_lat_norm, l1_w_uq,
        l1_kv_lat_norm, l1_w_ukv, l1_mla_q_gain, l1_mla_k_gain, l1_gqa_q_gain, l1_gqa_k_gain,
        False, [l1_exp_w_down.reshape(N_EXPERTS * EXPERT_FF, D_MODEL)], [])
    w_router = jnp.pad(l1_router, ((0, 0), (0, LANES - N_EXPERTS)))
    w_router_hi = w_router.astype(BF16)
    w_router = jnp.concatenate([w_router_hi, (w_router - w_router_hi.astype(F32)).astype(BF16)],
                               axis=1)
    x1, hrow, route, counts = _post_attention_router(xall, o_lat, mod3,
                                                     l1_norm_ffn.reshape(1, D_MODEL),
                                                     l1_w_out.astype(BF16), w_router)
    out = _moe_ffn(hrow, x1, route, counts, mod3,
                   wg_x.reshape(N_EXPERTS, D_MODEL, EXPERT_FF),
                   wu_x.reshape(N_EXPERTS, D_MODEL, EXPERT_FF),
                   wd_x.reshape(N_EXPERTS, EXPERT_FF, D_MODEL))
    return out.reshape(BATCH, SEQ, D_MODEL)
```

```python
import functools
import math

import jax
import jax.numpy as jnp
import numpy as np
from jax import lax
from jax.experimental import pallas as pl
from jax.experimental.pallas import tpu as pltpu

D_MODEL = 1024
BATCH = 8
SEQ = 2048
CTX_LEN = 256
GRID_W = 64
MLA_HEADS = 8
MLA_NOPE = 64
MLA_ROPE = 32
MLA_V = 64
MLA_QK = MLA_NOPE + MLA_ROPE
MLA_Q_RANK = 256
MLA_KV_RANK = 128
GQA_HEADS = 8
GQA_KV_HEADS = 2
GQA_HEAD_DIM = 64
DENSE_FF = 2816
N_EXPERTS = 8
EXPERT_FF = 2816
ROPE_THETA = 10000.0
NORM_EPS = 1e-6
LOG2E = math.log2(math.e)
MLA_SCALE = MLA_QK ** -0.5
GQA_SCALE = GQA_HEAD_DIM ** -0.5

LANES = 128
N_LAT = BATCH * SEQ
N_CTX = BATCH * CTX_LEN
N_TOK = N_LAT + N_CTX
TM = 512
LAT_TILES = N_LAT // TM
TOK_TILES = N_TOK // TM
SEQ_TILES = SEQ // TM
Q_CHUNK = 256
PROJ_CHUNK = 512
ROUTE_CHUNK = 512
V_WIDTH = 2 * LANES
IN_PAD = 1920
FF_SPLITS = (0, 1536, 2816)
ROW_SUB = D_MODEL // LANES
TMX = 512
SORT_TILES = 2 * N_LAT // TMX + N_EXPERTS
SORT_ROWS = SORT_TILES * TMX
Y2_ROWS = 2 * N_LAT + 2 * TMX
assert N_LAT & (N_LAT - 1) == 0
VMEM_LIMIT = 56 * 1024 * 1024

F32 = jnp.float32
BF16 = jnp.bfloat16


def _silu(x):
    return x / (1.0 + jnp.exp(-x))


def _params(*sem):
    return pltpu.CompilerParams(dimension_semantics=sem, vmem_limit_bytes=VMEM_LIMIT)


def _mod_kernel(c_ref, w_ref, b_ref, o_ref):
    s = _silu(c_ref[...])
    o_ref[...] = jnp.dot(s, w_ref[...], precision=lax.Precision.HIGHEST,
                         preferred_element_type=F32) + b_ref[...]


def _modulation(cc, w_mod, b_mod):
    n = w_mod.shape[1]
    bn = 1024
    out = pl.pallas_call(
        _mod_kernel,
        grid=(n // bn,),
        in_specs=[pl.BlockSpec((16, D_MODEL), lambda j: (0, 0)),
                  pl.BlockSpec((D_MODEL, bn), lambda j: (0, j)),
                  pl.BlockSpec((1, bn), lambda j: (0, j))],
        out_specs=pl.BlockSpec((16, bn), lambda j: (0, j)),
        out_shape=jax.ShapeDtypeStruct((16, n), F32),
        compiler_params=_params("parallel"),
        name="modulation",
    )(cc, w_mod, b_mod.reshape(1, n))
    return out.reshape(16, 6, D_MODEL)


def _tile_rows(lat_ref, ctx_ref):
    return jnp.where(pl.program_id(0) < LAT_TILES, lat_ref[...], ctx_ref[...])


def _proj_kernel(xl_ref, xc_ref, mod_ref, gn_ref, win_ref, glat_ref, wuq_ref, wkv_ref, g128_ref,
                 tab_ref, qa_ref, ka_ref, va_ref, qb_ref, kb_ref, vb_ref):
    latent = pl.program_id(0) < LAT_TILES
    n = PROJ_CHUNK
    lane = lax.broadcasted_iota(jnp.int32, (n, LANES), 1)
    lo = lane < 64
    ones_col = jnp.where(lane == 0, 1.0, 0.0).astype(BF16)

    for r0 in range(0, TM, n):
        rows = slice(r0, r0 + n)
        x = jnp.where(latent, xl_ref[rows, :], xc_ref[rows, :])
        r = lax.rsqrt(jnp.mean(x * x, axis=-1, keepdims=True) + NORM_EPS)
        h = (x * r * gn_ref[...]) * (1.0 + mod_ref[0, 1:2, :]) + mod_ref[0, 0:1, :]
        p = jnp.dot(h.astype(BF16), win_ref[...], preferred_element_type=F32)

        def tables(base, row):
            return (tab_ref[base, rows, :] * g128_ref[row:row + 1, :],
                    tab_ref[base + 1, rows, :] * g128_ref[row + 1:row + 2, :])

        cq = p[:, 0:256]
        rq = lax.rsqrt(jnp.mean(cq * cq, axis=-1, keepdims=True) + NORM_EPS)
        qa = jnp.dot((cq * rq * glat_ref[0:1, :]).astype(BF16), wuq_ref[...],
                     preferred_element_type=F32)
        ckv = p[:, 256:384]
        rkv = lax.rsqrt(jnp.mean(ckv * ckv, axis=-1, keepdims=True) + NORM_EPS)
        slab = jnp.concatenate([ckv * rkv * glat_ref[1:2, 0:128], p[:, 384:512]], axis=-1)
        kv = jnp.dot(slab.astype(BF16), wkv_ref[...], preferred_element_type=F32)

        def head_a(src, hd, cos_g, sin_g):
            blk = src[:, hd * LANES:(hd + 1) * LANES]
            partner = src[:, 1024 + hd * LANES:1024 + (hd + 1) * LANES]
            rr = lax.rsqrt(jnp.sum(blk * blk, axis=-1, keepdims=True) * (1.0 / MLA_QK) + NORM_EPS)
            return ((blk * cos_g + partner * sin_g) * rr).astype(BF16)

        cq_g, sq_g = tables(0, 0)
        ck_g, sk_g = tables(0, 2)
        for hd in range(MLA_HEADS):
            qa_ref[hd, rows, :] = head_a(qa, hd, cq_g, sq_g)
            ka_ref[hd, rows, :] = head_a(kv, hd, ck_g, sk_g)
        for pr in range(MLA_HEADS // 2):
            blk = kv[:, 2048 + pr * LANES:2048 + (pr + 1) * LANES]
            va_ref[2 * pr, rows, 0:LANES] = jnp.where(lo, blk, 0.0).astype(BF16)
            va_ref[2 * pr + 1, rows, 0:LANES] = jnp.where(lo, 0.0, blk).astype(BF16)
        for hd in range(MLA_HEADS):
            va_ref[hd, rows, LANES:V_WIDTH] = ones_col

        def pair_b(blk, partner, cos_g, sin_g):
            sq = blk * blk
            s_lo = jnp.sum(jnp.where(lo, sq, 0.0), axis=-1, keepdims=True)
            s_hi = jnp.sum(jnp.where(lo, 0.0, sq), axis=-1, keepdims=True)
            rr = jnp.where(lo, lax.rsqrt(s_lo * (1.0 / GQA_HEAD_DIM) + NORM_EPS),
                           lax.rsqrt(s_hi * (1.0 / GQA_HEAD_DIM) + NORM_EPS))
            return (blk * cos_g + partner * sin_g) * rr

        cq_g, sq_g = tables(2, 4)
        ck_g, sk_g = tables(2, 6)
        for pr in range(GQA_HEADS // 2):
            blk = p[:, 512 + pr * LANES:512 + (pr + 1) * LANES]
            partner = p[:, 1280 + pr * LANES:1280 + (pr + 1) * LANES]
            qb_ref[pr, rows, :] = pair_b(blk, partner, cq_g, sq_g).astype(BF16)
        kb = pair_b(p[:, 1024:1152], p[:, 1792:1920], ck_g, sk_g)
        kb_sw = pltpu.roll(kb, 64, 1)
        kb_ref[0, rows, :] = jnp.where(lo, kb, 0.0).astype(BF16)
        kb_ref[1, rows, :] = jnp.where(lo, 0.0, kb_sw).astype(BF16)
        kb_ref[2, rows, :] = jnp.where(lo, kb_sw, 0.0).astype(BF16)
        kb_ref[3, rows, :] = jnp.where(lo, 0.0, kb).astype(BF16)
        vb = p[:, 1152:1280]
        vb_sw = pltpu.roll(vb, 64, 1)
        vb_ref[0, rows, 0:LANES] = jnp.where(lo, vb, 0.0).astype(BF16)
        vb_ref[1, rows, 0:LANES] = jnp.where(lo, 0.0, vb_sw).astype(BF16)
        vb_ref[2, rows, 0:LANES] = jnp.where(lo, vb_sw, 0.0).astype(BF16)
        vb_ref[3, rows, 0:LANES] = jnp.where(lo, 0.0, vb).astype(BF16)
        for j in range(2 * GQA_KV_HEADS):
            vb_ref[j, rows, LANES:V_WIDTH] = ones_col


def _mod_index(i):
    return jnp.where(i < LAT_TILES, i // SEQ_TILES, BATCH)


def _lat_ctx_specs(width, ctx_tile0):
    return [pl.BlockSpec((TM, width), lambda i: (jnp.minimum(i, LAT_TILES - 1), 0)),
            pl.BlockSpec((TM, width), lambda i: (ctx_tile0 + jnp.maximum(i - LAT_TILES, 0), 0))]


def _projection(x_lat, x_ctx, ctx_tile0, mod3, gn, w_in, glat, w_uq, w_kv, g128, tab):
    const2 = lambda i: (0, 0)
    head_out = lambda n, w: pl.BlockSpec((n, TM, w), lambda i: (0, i, 0))
    head_shape = lambda n, w: jax.ShapeDtypeStruct((n, N_TOK, w), BF16)
    return pl.pallas_call(
        _proj_kernel,
        grid=(TOK_TILES,),
        in_specs=_lat_ctx_specs(D_MODEL, ctx_tile0) + [
                  pl.BlockSpec((1, 6, D_MODEL), lambda i: (_mod_index(i), 0, 0)),
                  pl.BlockSpec((1, D_MODEL), const2),
                  pl.BlockSpec((D_MODEL, IN_PAD), const2, pipeline_mode=pl.Buffered(1)),
                  pl.BlockSpec((2, 256), const2),
                  pl.BlockSpec((MLA_Q_RANK, 2 * MLA_HEADS * LANES), const2,
                               pipeline_mode=pl.Buffered(1)),
                  pl.BlockSpec((256, 2 * MLA_HEADS * LANES + MLA_HEADS * MLA_V), const2,
                               pipeline_mode=pl.Buffered(1)),
                  pl.BlockSpec((8, LANES), const2),
                  pl.BlockSpec((4, TM, LANES),
                               lambda i: (0, jnp.where(i < LAT_TILES, i % SEQ_TILES, SEQ_TILES), 0))],
        out_specs=[head_out(8, LANES), head_out(8, LANES), head_out(8, V_WIDTH),
                   head_out(4, LANES), head_out(4, LANES), head_out(4, V_WIDTH)],
        out_shape=[head_shape(8, LANES), head_shape(8, LANES), head_shape(8, V_WIDTH),
                   head_shape(4, LANES), head_shape(4, LANES), head_shape(4, V_WIDTH)],
        compiler_params=_params("parallel"),
        name="projection",
    )(x_lat, x_ctx, mod3, gn, w_in, glat, w_uq, w_kv, g128, tab)


def _pair_rows(q_refs, r0, n, kv_refs):
    nt = (((1,), (1,)), ((), ()))
    out = None
    for u, q_ref in enumerate(q_refs):
        q = q_ref[r0:r0 + n, :]
        scores = [lax.dot_general(q, k_ref[u], nt, preferred_element_type=F32)
                  for k_ref, _ in kv_refs]
        m = functools.reduce(jnp.maximum, [jnp.max(sc, axis=-1, keepdims=True) for sc in scores])
        acc = None
        for sc, (_, v_ref) in zip(scores, kv_refs):
            part = jnp.dot(jnp.exp2((sc - m).astype(BF16)), v_ref[u], preferred_element_type=F32)
            acc = part if acc is None else acc + part
        o = acc[:, 0:LANES] / acc[:, LANES:LANES + 1]
        out = o if out is None else out + o
    return out


def _attn_kernel(*refs, ctx_queries, n_casts):
    q0_ref, q1_ref, kl_ref, kc_ref, vl_ref, vc_ref = refs[:6]
    rest = list(refs[6:])
    qc_refs = (rest.pop(0), rest.pop(0)) if ctx_queries else None
    src_refs = [rest.pop(0) for _ in range(n_casts)]
    o_ref = rest.pop(0)
    oc_ref = rest.pop(0) if ctx_queries else None
    for src_ref in src_refs:
        rest.pop(0)[...] = src_ref[...].astype(BF16)
    for r0 in range(0, SEQ, Q_CHUNK):
        out = _pair_rows((q0_ref, q1_ref), r0, Q_CHUNK, ((kc_ref, vc_ref), (kl_ref, vl_ref)))
        o_ref[r0:r0 + Q_CHUNK, :] = out.astype(o_ref.dtype)
    if ctx_queries:
        oc_ref[...] = _pair_rows(qc_refs, 0, CTX_LEN, ((kc_ref, vc_ref),)).astype(oc_ref.dtype)


def _attention(q, k, v, q_heads, kv_pair, ctx_queries, cast_srcs):
    n_units = 4
    ctx_blk0 = N_LAT // CTX_LEN

    def qspec(which, rows, blk0):
        return pl.BlockSpec((None, rows, LANES), lambda b, p: (q_heads(p)[which], blk0 + b, 0))

    def kvspec(rows, blk0, width):
        return pl.BlockSpec((2, rows, width), lambda b, p: (kv_pair(p), blk0 + b, 0))

    in_specs = [qspec(0, SEQ, 0), qspec(1, SEQ, 0), kvspec(SEQ, 0, LANES),
                kvspec(CTX_LEN, ctx_blk0, LANES), kvspec(SEQ, 0, V_WIDTH),
                kvspec(CTX_LEN, ctx_blk0, V_WIDTH)]
    args = [q, q, k, k, v, v]
    out_specs = [pl.BlockSpec((SEQ, LANES), lambda b, p: (b, p))]
    out_shape = [jax.ShapeDtypeStruct((N_LAT, n_units * LANES), BF16)]
    if ctx_queries:
        in_specs += [qspec(0, CTX_LEN, ctx_blk0), qspec(1, CTX_LEN, ctx_blk0)]
        args += [q, q]
        out_specs.append(pl.BlockSpec((CTX_LEN, LANES), lambda b, p: (b, p)))
        out_shape.append(jax.ShapeDtypeStruct((N_CTX, n_units * LANES), BF16))
    for src in cast_srcs:
        rows, cols = src.shape
        slab = pl.BlockSpec((rows // (BATCH * n_units), cols), lambda b, p: (b * n_units + p, 0))
        in_specs.append(slab)
        args.append(src)
        out_specs.append(slab)
        out_shape.append(jax.ShapeDtypeStruct((rows, cols), BF16))
    outs = list(pl.pallas_call(
        functools.partial(_attn_kernel, ctx_queries=ctx_queries, n_casts=len(cast_srcs)),
        grid=(BATCH, n_units),
        in_specs=in_specs, out_specs=out_specs, out_shape=out_shape,
        compiler_params=_params("parallel", "parallel"),
        name="attention",
    )(*args))
    o_lat = outs.pop(0)
    o_ctx = outs.pop(0) if ctx_queries else None
    return o_lat, o_ctx, outs


_A_HEADS = lambda p: (2 * p, 2 * p + 1)
_A_KV = lambda p: p
_B_HEADS = lambda p: (p, p)
_B_KV = lambda p: p // 2


def _post_body(x, oa, ob, mod_ref, gn_ref, wo_ref):
    y = (jnp.dot(oa, wo_ref[0:512, :], preferred_element_type=F32)
         + jnp.dot(ob, wo_ref[512:1024, :], preferred_element_type=F32))
    x1 = x + mod_ref[0, 2:3, :] * y
    r = lax.rsqrt(jnp.mean(x1 * x1, axis=-1, keepdims=True) + NORM_EPS)
    h2 = (x1 * r * gn_ref[...]) * (1.0 + mod_ref[0, 4:5, :]) + mod_ref[0, 3:4, :]
    return x1, h2


def _swiglu(h, wg_ref, wu_ref, wd_ref):
    y = None
    for lo, hi in zip(FF_SPLITS[:-1], FF_SPLITS[1:]):
        g = jnp.dot(h, wg_ref[:, lo:hi], preferred_element_type=F32)
        u = jnp.dot(h, wu_ref[:, lo:hi], preferred_element_type=F32)
        yc = jnp.dot((_silu(g) * u).astype(BF16), wd_ref[lo:hi, :], preferred_element_type=F32)
        y = yc if y is None else y + yc
    return y


def _post_ffn_kernel(xl_ref, xc_ref, oa_ref, oac_ref, ob_ref, obc_ref, mod_ref, gn_ref, wo_ref,
                     wg_ref, wu_ref, wd_ref, o_ref):
    x1, h2 = _post_body(_tile_rows(xl_ref, xc_ref), _tile_rows(oa_ref, oac_ref),
                        _tile_rows(ob_ref, obc_ref), mod_ref, gn_ref, wo_ref)
    y = _swiglu(h2.astype(BF16), wg_ref, wu_ref, wd_ref)
    o_ref[...] = x1 + mod_ref[0, 5:6, :] * y


def _post_router_kernel(x_ref, oa_ref, ob_ref, mod_ref, gn_ref, wo_ref, wr_ref,
                        x1_ref, hrow_ref, route_ref, route_t_ref, count_ref, carry_ref):
    @pl.when(pl.program_id(0) == 0)
    def _():
        carry_ref[...] = jnp.zeros_like(carry_ref)

    n = ROUTE_CHUNK
    lane = lax.broadcasted_iota(jnp.int32, (n, LANES), 1)
    valid = lane < N_EXPERTS
    row = lax.broadcasted_iota(jnp.int32, (n, n), 0)
    col = lax.broadcasted_iota(jnp.int32, (n, n), 1)
    before = jnp.where(col < row, 1.0, 0.0).astype(BF16)
    carry = carry_ref[0:1, :]

    for r0 in range(0, TM, n):
        rows = slice(r0, r0 + n)
        x1, h2 = _post_body(x_ref[rows, :], oa_ref[rows, :], ob_ref[rows, :], mod_ref, gn_ref,
                            wo_ref)
        x1_ref[rows, :] = x1
        for s in range(ROW_SUB):
            hrow_ref[pl.ds(r0 * ROW_SUB + s, n, stride=ROW_SUB), :] = h2[:, s * LANES:(s + 1) * LANES]

        h_hi = h2.astype(BF16)
        h_lo = (h2 - h_hi.astype(F32)).astype(BF16)
        part = jnp.dot(h_hi, wr_ref[...], preferred_element_type=F32)
        logits = (part[:, 0:LANES] + part[:, LANES:2 * LANES]
                  + jnp.dot(h_lo, wr_ref[:, 0:LANES], preferred_element_type=F32))
        lg = jnp.where(valid, logits, -jnp.inf)
        e = jnp.exp(lg - jnp.max(lg, axis=-1, keepdims=True))
        probs = e / jnp.sum(e, axis=-1, keepdims=True)
        v1 = jnp.max(probs, axis=-1, keepdims=True)
        i1 = jnp.min(jnp.where(probs == v1, lane, LANES), axis=-1, keepdims=True)
        rest = jnp.where(valid & (lane != i1), probs, -1.0)
        v2 = jnp.max(rest, axis=-1, keepdims=True)
        i2 = jnp.min(jnp.where(rest == v2, lane, LANES), axis=-1, keepdims=True)

        pick1 = lane == i1
        pick2 = lane == i2
        onehot = jnp.where(pick1, 1.0, 0.0) + jnp.where(pick2, 1.0, 0.0)
        seen = jnp.dot(before, onehot.astype(BF16), preferred_element_type=F32) + carry
        rank1 = jnp.sum(jnp.where(pick1, seen, 0.0), axis=-1, keepdims=True)
        rank2 = jnp.sum(jnp.where(pick2, seen, 0.0), axis=-1, keepdims=True)
        carry = carry + jnp.sum(onehot, axis=0, keepdims=True)

        fields = (v1 / (v1 + v2), v2 / (v1 + v2), i1.astype(F32), i2.astype(F32), rank1, rank2)
        route = jnp.zeros((n, LANES), F32)
        for k, val in enumerate(fields):
            route = jnp.where(lane == k, val, route)
        route_ref[rows, :] = route
        route_t_ref[:, rows] = jnp.transpose(route)[0:8, :]

    carry_ref[...] = jnp.broadcast_to(carry, carry_ref.shape)
    count_ref[...] = jnp.broadcast_to(carry, count_ref.shape)


def _post_specs():
    const2 = lambda i: (0, 0)
    return [pl.BlockSpec((1, 6, D_MODEL), lambda i: (_mod_index(i), 0, 0)),
            pl.BlockSpec((1, D_MODEL), const2),
            pl.BlockSpec((D_MODEL, D_MODEL), const2, pipeline_mode=pl.Buffered(1))]


def _post_dense_ffn(x_lat, x_ctx, o_lat, o_ctx, mod3, gn, w_out, wg, wu, wd):
    resident = lambda shape: pl.BlockSpec(shape, lambda i: (0, 0), pipeline_mode=pl.Buffered(1))
    return pl.pallas_call(
        _post_ffn_kernel, grid=(TOK_TILES,),
        in_specs=(_lat_ctx_specs(D_MODEL, 0) + _lat_ctx_specs(512, 0) + _lat_ctx_specs(512, 0)
                  + _post_specs() + [resident((D_MODEL, DENSE_FF)), resident((D_MODEL, DENSE_FF)),
                                     resident((DENSE_FF, D_MODEL))]),
        out_specs=pl.BlockSpec((TM, D_MODEL), lambda i: (i, 0)),
        out_shape=jax.ShapeDtypeStruct((N_TOK, D_MODEL), F32),
        compiler_params=_params("parallel"), name="post_attention_ffn",
    )(x_lat, x_ctx, o_lat[0], o_ctx[0], o_lat[1], o_ctx[1], mod3, gn, w_out, wg, wu, wd)


def _post_attention_router(x_lat, o_lat, mod3, gn, w_out, w_router):
    const2 = lambda i: (0, 0)
    n_tiles = LAT_TILES
    rows = n_tiles * TM
    x1_spec = pl.BlockSpec((TM, D_MODEL), lambda i: (i, 0))
    x1_shape = jax.ShapeDtypeStruct((rows, D_MODEL), F32)
    half_spec = pl.BlockSpec((TM, 512), lambda i: (i, 0))
    in_specs = [x1_spec, half_spec, half_spec] + _post_specs()
    args = [x_lat, *o_lat, mod3, gn, w_out]
    return pl.pallas_call(
        _post_router_kernel, grid=(n_tiles,),
        in_specs=in_specs + [pl.BlockSpec((D_MODEL, 2 * LANES), const2)],
        out_specs=[x1_spec,
                   pl.BlockSpec((TM * ROW_SUB, LANES), lambda i: (i, 0)),
                   pl.BlockSpec((TM, LANES), lambda i: (i, 0)),
                   pl.BlockSpec((8, TM), lambda i: (0, i)),
                   pl.BlockSpec((8, LANES), const2)],
        out_shape=[x1_shape,
                   jax.ShapeDtypeStruct((rows * ROW_SUB, LANES), F32),
                   jax.ShapeDtypeStruct((rows, LANES), F32),
                   jax.ShapeDtypeStruct((8, rows), F32),
                   jax.ShapeDtypeStruct((8, LANES), F32)],
        scratch_shapes=[pltpu.VMEM((8, LANES), F32)],
        compiler_params=_params("arbitrary"), name="post_attention_router",
    )(*args, w_router)


def _row(ref, r):
    return ref.at[pl.ds(pl.multiple_of(r * ROW_SUB, ROW_SUB), ROW_SUB)]


def _rows_to_matrix(ref, n):
    return jnp.concatenate([ref[pl.ds(s, n, stride=ROW_SUB), :] for s in range(ROW_SUB)], axis=-1)


def _tile_row(buf, j):
    return buf.at[j // 8, :, j % 8, :]


def _invert_kernel(gaps_ref, dest_ref, inv_ref):
    def clear(r, carry):
        inv_ref[r] = -1
        return carry

    def place(a, carry):
        inv_ref[dest_ref[a]] = a
        return carry

    for e in range(N_EXPERTS):
        lax.fori_loop(gaps_ref[2 * e], gaps_ref[2 * e + 1], clear, 0)
    lax.fori_loop(gaps_ref[2 * N_EXPERTS], SORT_ROWS, clear, 0)
    lax.fori_loop(0, 2 * N_LAT, place, 0, unroll=16)


def _invert(gaps, dest_flat):
    return pl.pallas_call(
        _invert_kernel,
        in_specs=[pl.BlockSpec(memory_space=pltpu.SMEM), pl.BlockSpec(memory_space=pltpu.SMEM)],
        out_specs=pl.BlockSpec(memory_space=pltpu.SMEM),
        out_shape=jax.ShapeDtypeStruct((SORT_ROWS,), jnp.int32),
        name="moe_invert",
    )(gaps, dest_flat)


def _expert_kernel(te_ref, nu_ref, inv_ref, h_ref, wg_ref, wu_ref, wd_ref, y2_ref,
                   xbuf, ybuf, sem_g, sem_s):
    del te_ref
    t = pl.program_id(0)
    n_used = nu_ref[0]
    slot = lax.rem(t, 2)
    other = 1 - slot

    def gather_rows(tile, s):
        base = tile * TMX
        for j in range(TMX):
            token = jnp.maximum(inv_ref[base + j], 0) & (N_LAT - 1)
            pltpu.make_async_copy(_row(h_ref, token), _tile_row(xbuf.at[s], j),
                                  sem_g.at[s]).start(priority=j % 2)

    def scatter_rows(tile, s, real, scratch_block):
        base = tile * TMX
        for j in range(TMX):
            code = inv_ref[base + j]
            row = jnp.where(jnp.logical_and(code >= 0, real), code,
                            2 * N_LAT + scratch_block * TMX + j)
            pltpu.make_async_copy(_tile_row(ybuf.at[s], j), _row(y2_ref, row),
                                  sem_s.at[s]).start(priority=j % 2)

    def wait_gather(s):
        pltpu.make_async_copy(xbuf.at[s], xbuf.at[s], sem_g.at[s]).wait()

    def wait_scatter(s):
        pltpu.make_async_copy(ybuf.at[s], ybuf.at[s], sem_s.at[s]).wait()

    @pl.when(t == 0)
    def _():
        ybuf[...] = jnp.zeros_like(ybuf)
        gather_rows(0, 0)

    @pl.when(t < n_used)
    def _():
        wait_gather(slot)
        x = jnp.concatenate([xbuf[slot, :, m].reshape(TMX, LANES) for m in range(ROW_SUB)],
                            axis=-1).astype(BF16)
        gather_rows(jnp.minimum(t + 1, n_used - 1), other)
        scatter_rows(jnp.maximum(t - 1, 0), other, t > 0, other)
        y = _swiglu(x, wg_ref.at[0], wu_ref.at[0], wd_ref.at[0])

        @pl.when(t > 0)
        def _():
            wait_scatter(slot)

        for m in range(ROW_SUB):
            ybuf[slot, :, m] = y[:, m * LANES:(m + 1) * LANES].reshape(TMX // 8, 8, LANES)

    @pl.when(t == n_used)
    def _():
        wait_gather(slot)
        scatter_rows(n_used - 1, other, True, other)
        wait_scatter(other)
        wait_scatter(slot)
        scatter_rows(n_used - 1, slot, False, 0)
        wait_scatter(slot)


def _experts(tile_expert, n_used, inv, hrow, wg, wu, wd):
    one = pl.Buffered(1)
    wspec = lambda shape: pl.BlockSpec(shape, lambda t, te, nu, iv: (te[t], 0, 0), pipeline_mode=one)
    grid_spec = pltpu.PrefetchScalarGridSpec(
        num_scalar_prefetch=3,
        grid=(SORT_TILES + 1,),
        in_specs=[pl.BlockSpec(memory_space=pl.ANY),
                  wspec((1, D_MODEL, EXPERT_FF)), wspec((1, D_MODEL, EXPERT_FF)),
                  wspec((1, EXPERT_FF, D_MODEL))],
        out_specs=pl.BlockSpec(memory_space=pl.ANY),
        scratch_shapes=[pltpu.VMEM((2, TMX // 8, ROW_SUB, 8, LANES), F32),
                        pltpu.VMEM((2, TMX // 8, ROW_SUB, 8, LANES), F32),
                        pltpu.SemaphoreType.DMA((2,)), pltpu.SemaphoreType.DMA((2,))])
    return pl.pallas_call(
        _expert_kernel,
        grid_spec=grid_spec,
        out_shape=jax.ShapeDtypeStruct((Y2_ROWS * ROW_SUB, LANES), F32),
        compiler_params=_params("arbitrary"),
        name="moe_experts",
    )(tile_expert, n_used, inv, hrow, wg, wu, wd)


def _combine_kernel(ya_ref, yb_ref, x1_ref, route_ref, mod_ref, o_ref):
    route = route_ref[...]
    y = (route[:, 0:1] * _rows_to_matrix(ya_ref, TM) + route[:, 1:2] * _rows_to_matrix(yb_ref, TM))
    o_ref[...] = x1_ref[...] + mod_ref[0, 5:6, :] * y


def _combine(y2, x1, route, mod3):
    return pl.pallas_call(
        _combine_kernel,
        grid=(LAT_TILES,),
        in_specs=[pl.BlockSpec((TM * ROW_SUB, LANES), lambda i: (i, 0)),
                  pl.BlockSpec((TM * ROW_SUB, LANES), lambda i: (LAT_TILES + i, 0)),
                  pl.BlockSpec((TM, D_MODEL), lambda i: (i, 0)),
                  pl.BlockSpec((TM, LANES), lambda i: (i, 0)),
                  pl.BlockSpec((1, 6, D_MODEL), lambda i: (i // SEQ_TILES, 0, 0))],
        out_specs=pl.BlockSpec((TM, D_MODEL), lambda i: (i, 0)),
        out_shape=jax.ShapeDtypeStruct((N_LAT, D_MODEL), F32),
        compiler_params=_params("parallel"),
        name="moe_combine",
    )(y2, y2, x1, route, mod3)


def _moe_ffn(hrow, x1, route, route_t, counts, mod3, wg, wu, wd):
    expert = route_t[2:4].astype(jnp.int32)
    rank = route_t[4:6].astype(jnp.int32)
    count = counts[0, :N_EXPERTS].astype(jnp.int32)
    tiles = (count + TMX - 1) // TMX
    tile_end = jnp.cumsum(tiles)
    dest = (tile_end - tiles)[expert] * TMX + rank
    gaps = jnp.stack([(tile_end - tiles) * TMX + count, tile_end * TMX], axis=1).reshape(-1)
    inv = _invert(jnp.concatenate([gaps, tile_end[N_EXPERTS - 1:] * TMX]), dest.reshape(-1))
    n_used = tile_end[N_EXPERTS - 1:]
    tile_ids = jnp.minimum(jnp.arange(SORT_TILES + 1), n_used[0] - 1)
    tile_expert = jnp.sum(tile_ids[:, None] >= tile_end[None, :], axis=1).astype(jnp.int32)
    y2 = _experts(tile_expert, n_used.astype(jnp.int32), inv, hrow, wg, wu, wd)
    return _combine(y2, x1, route, mod3)


def _partner(half):
    idx = np.arange(4 * half)
    return np.where((idx // half) % 2 == 0, idx + half, idx - half)


_PARTNER_A = np.concatenate([np.arange(MLA_NOPE), MLA_NOPE + _partner(MLA_ROPE // 4),
                             np.arange(MLA_QK, LANES)])
_PARTNER_B = np.concatenate([_partner(GQA_HEAD_DIM // 4), GQA_HEAD_DIM + _partner(GQA_HEAD_DIM // 4)])


def _rope_tables():
    f32 = np.float32
    t = np.arange(SEQ)
    row = (t // GRID_W).astype(f32)
    col = (t % GRID_W).astype(f32)

    def one_axis(pos, half):
        freqs = f32(ROPE_THETA) ** (-np.arange(half, dtype=f32) / f32(half))
        ang = pos[:, None] * freqs[None, :]
        cos, sin = np.cos(ang), np.sin(ang)
        return np.concatenate([cos, cos], -1), np.concatenate([-sin, sin], -1)

    def two_axes(half):
        r, c = one_axis(row, half), one_axis(col, half)
        return [np.concatenate([a, b], -1) for a, b in zip(r, c)]

    def pad_a(tbl, fill):
        return np.concatenate([np.full((SEQ, MLA_NOPE), fill, f32), tbl,
                               np.full((SEQ, LANES - MLA_QK), fill, f32)], -1)

    ta = two_axes(MLA_ROPE // 4)
    tb = [np.concatenate([x, x], -1) for x in two_axes(GQA_HEAD_DIM // 4)]
    tabs = [pad_a(ta[0], 1.0), pad_a(ta[1], 0.0)] + tb
    ident = [np.ones((TM, LANES), f32), np.zeros((TM, LANES), f32)]
    return np.stack([np.concatenate([tbl, idn], 0)
                     for tbl, idn in zip(tabs, ident + ident)]).astype(f32)


def _layer_weights(w_in, q_lat_norm, w_uq, kv_lat_norm, w_ukv, mla_q_gain, mla_k_gain,
                   gqa_q_gain, gqa_k_gain):
    split = MLA_Q_RANK + MLA_KV_RANK + MLA_ROPE
    perm_a = (np.arange(MLA_HEADS)[:, None] * LANES + _PARTNER_A[None, :]).reshape(-1)
    perm_b = (np.arange(GQA_HEADS // 2)[:, None] * LANES + _PARTNER_B[None, :]).reshape(-1)
    w_qb = w_in[:, split:split + GQA_HEADS * GQA_HEAD_DIM]
    w_kb = w_in[:, split + 512:split + 640]
    w_in_p = jnp.concatenate([w_in[:, :split], jnp.zeros((D_MODEL, 512 - split), F32),
                              w_in[:, split:], w_qb[:, perm_b], w_kb[:, _PARTNER_B]],
                             axis=1).astype(BF16)
    w_uq_p = jnp.pad(w_uq.reshape(MLA_Q_RANK, MLA_HEADS, MLA_QK),
                     ((0, 0), (0, 0), (0, LANES - MLA_QK))).reshape(MLA_Q_RANK, MLA_HEADS * LANES)
    w_uq_p = jnp.concatenate([w_uq_p, w_uq_p[:, perm_a]], axis=1)
    ukv = w_ukv.reshape(MLA_KV_RANK, MLA_HEADS, MLA_NOPE + MLA_V)
    w_k = jnp.pad(ukv[:, :, :MLA_NOPE], ((0, 0), (0, 0), (0, LANES - MLA_NOPE)))
    place = jnp.pad(jnp.eye(MLA_ROPE, dtype=F32), ((0, 0), (MLA_NOPE, LANES - MLA_QK)))
    place = jnp.broadcast_to(place[:, None, :], (MLA_ROPE, MLA_HEADS, LANES))
    w_k = jnp.concatenate([w_k, place,
                           jnp.zeros((256 - MLA_KV_RANK - MLA_ROPE, MLA_HEADS, LANES), F32)], 0)
    w_k = w_k.reshape(256, MLA_HEADS * LANES)
    w_v = jnp.pad(ukv[:, :, MLA_NOPE:].reshape(MLA_KV_RANK, MLA_HEADS * MLA_V),
                  ((0, 256 - MLA_KV_RANK), (0, 0)))
    w_kv = jnp.concatenate([w_k, w_k[:, perm_a], w_v], axis=1)
    glat = jnp.stack([q_lat_norm, jnp.pad(kv_lat_norm, (0, 256 - MLA_KV_RANK))])
    pad_qk = lambda g: jnp.pad(g, (0, LANES - MLA_QK))
    gains = [pad_qk(mla_q_gain) * (MLA_SCALE * LOG2E), pad_qk(mla_k_gain),
             jnp.tile(gqa_q_gain, 2) * (GQA_SCALE * LOG2E), jnp.tile(gqa_k_gain, 2)]
    partners = [_PARTNER_A, _PARTNER_A, _PARTNER_B, _PARTNER_B]
    g128 = jnp.stack([v for g, pm in zip(gains, partners) for v in (g, g[pm])])
    return w_in_p, glat, w_uq_p.astype(BF16), w_kv.astype(BF16), g128


def _mixer(x_lat, x_ctx, ctx_tile0, mod3, tab, norm_attn, w_in, q_lat_norm, w_uq, kv_lat_norm,
           w_ukv, mla_q_gain, mla_k_gain, gqa_q_gain, gqa_k_gain, with_ctx_queries, cast_a, cast_b):
    w_in_p, glat, w_uq_p, w_kv, g128 = _layer_weights(
        w_in, q_lat_norm, w_uq, kv_lat_norm, w_ukv, mla_q_gain, mla_k_gain, gqa_q_gain, gqa_k_gain)
    qa, ka, va, qb, kb, vb = _projection(x_lat, x_ctx, ctx_tile0, mod3,
                                         norm_attn.reshape(1, D_MODEL), w_in_p, glat,
                                         w_uq_p, w_kv, g128, tab)
    oa, oa_ctx, cast_a = _attention(qa, ka, va, _A_HEADS, _A_KV, with_ctx_queries, cast_a)
    ob, ob_ctx, cast_b = _attention(qb, kb, vb, _B_HEADS, _B_KV, with_ctx_queries, cast_b)
    return (oa, ob), ((oa_ctx, ob_ctx) if with_ctx_queries else None), cast_a + cast_b


def kernel(x, c, ctx, c_ctx, l0_w_mod, l0_b_mod, l0_norm_attn, l0_w_in, l0_q_lat_norm, l0_w_uq, l0_kv_lat_norm, l0_w_ukv, l0_mla_q_gain, l0_mla_k_gain, l0_gqa_q_gain, l0_gqa_k_gain, l0_w_out, l0_norm_ffn, l0_ffn_w_gate, l0_ffn_w_up, l0_ffn_w_down, l1_w_mod, l1_b_mod, l1_norm_attn, l1_w_in, l1_q_lat_norm, l1_w_uq, l1_kv_lat_norm, l1_w_ukv, l1_mla_q_gain, l1_mla_k_gain, l1_gqa_q_gain, l1_gqa_k_gain, l1_w_out, l1_norm_ffn, l1_router, l1_exp_w_gate, l1_exp_w_up, l1_exp_w_down):
    x_lat = x.reshape(N_LAT, D_MODEL)
    x_ctx = ctx.reshape(N_CTX, D_MODEL)
    cc = jnp.concatenate([c, c_ctx[None, :], jnp.zeros((16 - BATCH - 1, D_MODEL), F32)], axis=0)
    tab = jnp.asarray(_rope_tables())

    mod3 = _modulation(cc, l0_w_mod, l0_b_mod)
    o_lat, o_ctx, (wg_x, wg_0, wu_x, wu_0) = _mixer(
        x_lat, x_ctx, 0, mod3, tab, l0_norm_attn, l0_w_in, l0_q_lat_norm, l0_w_uq, l0_kv_lat_norm,
        l0_w_ukv, l0_mla_q_gain, l0_mla_k_gain, l0_gqa_q_gain, l0_gqa_k_gain, True,
        [l1_exp_w_gate.reshape(N_EXPERTS * D_MODEL, EXPERT_FF), l0_ffn_w_gate],
        [l1_exp_w_up.reshape(N_EXPERTS * D_MODEL, EXPERT_FF), l0_ffn_w_up])
    xall = _post_dense_ffn(x_lat, x_ctx, o_lat, o_ctx, mod3, l0_norm_ffn.reshape(1, D_MODEL),
                           l0_w_out.astype(BF16), wg_0, wu_0, l0_ffn_w_down.astype(BF16))

    mod3 = _modulation(cc, l1_w_mod, l1_b_mod)
    o_lat, _, (wd_x,) = _mixer(
        xall, xall, LAT_TILES, mod3, tab, l1_norm_attn, l1_w_in, l1_q_lat_norm, l1_w_uq,
        l1_kv_lat_norm, l1_w_ukv, l1_mla_q_gain, l1_mla_k_gain, l1_gqa_q_gain, l1_gqa_k_gain,
        False, [l1_exp_w_down.reshape(N_EXPERTS * EXPERT_FF, D_MODEL)], [])
    w_router = jnp.pad(l1_router, ((0, 0), (0, LANES - N_EXPERTS)))
    w_router_hi = w_router.astype(BF16)
    w_router = jnp.concatenate([w_router_hi, (w_router - w_router_hi.astype(F32)).astype(BF16)],
                               axis=1)
    x1, hrow, route, route_t, counts = _post_attention_router(xall, o_lat, mod3,
                                                     l1_norm_ffn.reshape(1, D_MODEL),
                                                     l1_w_out.astype(BF16), w_router)
    out = _moe_ffn(hrow, x1, route, route_t, counts, mod3,
                   wg_x.reshape(N_EXPERTS, D_MODEL, EXPERT_FF),
                   wu_x.reshape(N_EXPERTS, D_MODEL, EXPERT_FF),
                   wd_x.reshape(N_EXPERTS, EXPERT_FF, D_MODEL))
    return out.reshape(BATCH, SEQ, D_MODEL)
```

```python
import functools
import math

import jax
import jax.numpy as jnp
import numpy as np
from jax import lax
from jax.experimental import pallas as pl
from jax.experimental.pallas import tpu as pltpu

D_MODEL = 1024
BATCH = 8
SEQ = 2048
CTX_LEN = 256
GRID_W = 64
MLA_HEADS = 8
MLA_NOPE = 64
MLA_ROPE = 32
MLA_V = 64
MLA_QK = MLA_NOPE + MLA_ROPE
MLA_Q_RANK = 256
MLA_KV_RANK = 128
GQA_HEADS = 8
GQA_KV_HEADS = 2
GQA_HEAD_DIM = 64
DENSE_FF = 2816
N_EXPERTS = 8
EXPERT_FF = 2816
ROPE_THETA = 10000.0
NORM_EPS = 1e-6
LOG2E = math.log2(math.e)
MLA_SCALE = MLA_QK ** -0.5
GQA_SCALE = GQA_HEAD_DIM ** -0.5

LANES = 128
N_LAT = BATCH * SEQ
N_CTX = BATCH * CTX_LEN
N_TOK = N_LAT + N_CTX
TM = 512
LAT_TILES = N_LAT // TM
TOK_TILES = N_TOK // TM
SEQ_TILES = SEQ // TM
Q_CHUNK = 256
PROJ_CHUNK = 512
ROUTE_CHUNK = 512
V_WIDTH = 2 * LANES
IN_PAD = 1920
FF_SPLITS = (0, 1536, 2816)
ROW_SUB = D_MODEL // LANES
TMX = 512
SORT_TILES = 2 * N_LAT // TMX + N_EXPERTS
SORT_ROWS = SORT_TILES * TMX
Y2_ROWS = 2 * N_LAT + 2 * TMX
assert N_LAT & (N_LAT - 1) == 0
VMEM_LIMIT = 56 * 1024 * 1024

F32 = jnp.float32
BF16 = jnp.bfloat16


def _silu(x):
    return x / (1.0 + jnp.exp(-x))


def _params(*sem):
    return pltpu.CompilerParams(dimension_semantics=sem, vmem_limit_bytes=VMEM_LIMIT)


def _mod_kernel(c_ref, w_ref, b_ref, o_ref):
    s = _silu(c_ref[...])
    o_ref[...] = jnp.dot(s, w_ref[...], precision=lax.Precision.HIGHEST,
                         preferred_element_type=F32) + b_ref[...]


def _modulation(cc, w_mod, b_mod):
    n = w_mod.shape[1]
    bn = 1024
    out = pl.pallas_call(
        _mod_kernel,
        grid=(n // bn,),
        in_specs=[pl.BlockSpec((16, D_MODEL), lambda j: (0, 0)),
                  pl.BlockSpec((D_MODEL, bn), lambda j: (0, j)),
                  pl.BlockSpec((1, bn), lambda j: (0, j))],
        out_specs=pl.BlockSpec((16, bn), lambda j: (0, j)),
        out_shape=jax.ShapeDtypeStruct((16, n), F32),
        compiler_params=_params("parallel"),
        name="modulation",
    )(cc, w_mod, b_mod.reshape(1, n))
    return out.reshape(16, 6, D_MODEL)


def _tile_rows(lat_ref, ctx_ref):
    return jnp.where(pl.program_id(0) < LAT_TILES, lat_ref[...], ctx_ref[...])


def _proj_kernel(xl_ref, xc_ref, mod_ref, gn_ref, win_ref, glat_ref, wuq_ref, wkv_ref, g128_ref,
                 tab_ref, qa_ref, ka_ref, va_ref, qb_ref, kb_ref, vb_ref):
    latent = pl.program_id(0) < LAT_TILES
    n = PROJ_CHUNK
    lane = lax.broadcasted_iota(jnp.int32, (n, LANES), 1)
    lo = lane < 64
    ones_col = jnp.where(lane == 0, 1.0, 0.0).astype(BF16)

    for r0 in range(0, TM, n):
        rows = slice(r0, r0 + n)
        x = jnp.where(latent, xl_ref[rows, :], xc_ref[rows, :])
        r = lax.rsqrt(jnp.mean(x * x, axis=-1, keepdims=True) + NORM_EPS)
        h = (x * r * gn_ref[...]) * (1.0 + mod_ref[0, 1:2, :]) + mod_ref[0, 0:1, :]
        p = jnp.dot(h.astype(BF16), win_ref[...], preferred_element_type=F32)

        def tables(base, row):
            return (tab_ref[base, rows, :] * g128_ref[row:row + 1, :],
                    tab_ref[base + 1, rows, :] * g128_ref[row + 1:row + 2, :])

        cq = p[:, 0:256]
        rq = lax.rsqrt(jnp.mean(cq * cq, axis=-1, keepdims=True) + NORM_EPS)
        qa = jnp.dot((cq * rq * glat_ref[0:1, :]).astype(BF16), wuq_ref[...],
                     preferred_element_type=F32)
        ckv = p[:, 256:384]
        rkv = lax.rsqrt(jnp.mean(ckv * ckv, axis=-1, keepdims=True) + NORM_EPS)
        slab = jnp.concatenate([ckv * rkv * glat_ref[1:2, 0:128], p[:, 384:512]], axis=-1)
        kv = jnp.dot(slab.astype(BF16), wkv_ref[...], preferred_element_type=F32)

        def head_a(src, hd, cos_g, sin_g):
            blk = src[:, hd * LANES:(hd + 1) * LANES]
            partner = src[:, 1024 + hd * LANES:1024 + (hd + 1) * LANES]
            rr = lax.rsqrt(jnp.sum(blk * blk, axis=-1, keepdims=True) * (1.0 / MLA_QK) + NORM_EPS)
            return ((blk * cos_g + partner * sin_g) * rr).astype(BF16)

        cq_g, sq_g = tables(0, 0)
        ck_g, sk_g = tables(0, 2)
        for hd in range(MLA_HEADS):
            qa_ref[hd, rows, :] = head_a(qa, hd, cq_g, sq_g)
            ka_ref[hd, rows, :] = head_a(kv, hd, ck_g, sk_g)
        for pr in range(MLA_HEADS // 2):
            blk = kv[:, 2048 + pr * LANES:2048 + (pr + 1) * LANES]
            va_ref[2 * pr, rows, 0:LANES] = jnp.where(lo, blk, 0.0).astype(BF16)
            va_ref[2 * pr + 1, rows, 0:LANES] = jnp.where(lo, 0.0, blk).astype(BF16)
        for hd in range(MLA_HEADS):
            va_ref[hd, rows, LANES:V_WIDTH] = ones_col

        def pair_b(blk, partner, cos_g, sin_g):
            sq = blk * blk
            s_lo = jnp.sum(jnp.where(lo, sq, 0.0), axis=-1, keepdims=True)
            s_hi = jnp.sum(jnp.where(lo, 0.0, sq), axis=-1, keepdims=True)
            rr = jnp.where(lo, lax.rsqrt(s_lo * (1.0 / GQA_HEAD_DIM) + NORM_EPS),
                           lax.rsqrt(s_hi * (1.0 / GQA_HEAD_DIM) + NORM_EPS))
            return (blk * cos_g + partner * sin_g) * rr

        cq_g, sq_g = tables(2, 4)
        ck_g, sk_g = tables(2, 6)
        for pr in range(GQA_HEADS // 2):
            blk = p[:, 512 + pr * LANES:512 + (pr + 1) * LANES]
            partner = p[:, 1280 + pr * LANES:1280 + (pr + 1) * LANES]
            qb_ref[pr, rows, :] = pair_b(blk, partner, cq_g, sq_g).astype(BF16)
        kb = pair_b(p[:, 1024:1152], p[:, 1792:1920], ck_g, sk_g)
        kb_sw = pltpu.roll(kb, 64, 1)
        kb_ref[0, rows, :] = jnp.where(lo, kb, 0.0).astype(BF16)
        kb_ref[1, rows, :] = jnp.where(lo, 0.0, kb_sw).astype(BF16)
        kb_ref[2, rows, :] = jnp.where(lo, kb_sw, 0.0).astype(BF16)
        kb_ref[3, rows, :] = jnp.where(lo, 0.0, kb).astype(BF16)
        vb = p[:, 1152:1280]
        vb_sw = pltpu.roll(vb, 64, 1)
        vb_ref[0, rows, 0:LANES] = jnp.where(lo, vb, 0.0).astype(BF16)
        vb_ref[1, rows, 0:LANES] = jnp.where(lo, 0.0, vb_sw).astype(BF16)
        vb_ref[2, rows, 0:LANES] = jnp.where(lo, vb_sw, 0.0).astype(BF16)
        vb_ref[3, rows, 0:LANES] = jnp.where(lo, 0.0, vb).astype(BF16)
        for j in range(2 * GQA_KV_HEADS):
            vb_ref[j, rows, LANES:V_WIDTH] = ones_col


def _mod_index(i):
    return jnp.where(i < LAT_TILES, i // SEQ_TILES, BATCH)


def _lat_ctx_specs(width, ctx_tile0):
    return [pl.BlockSpec((TM, width), lambda i: (jnp.minimum(i, LAT_TILES - 1), 0)),
            pl.BlockSpec((TM, width), lambda i: (ctx_tile0 + jnp.maximum(i - LAT_TILES, 0), 0))]


def _projection(x_lat, x_ctx, ctx_tile0, mod3, gn, w_in, glat, w_uq, w_kv, g128, tab):
    const2 = lambda i: (0, 0)
    head_out = lambda n, w: pl.BlockSpec((n, TM, w), lambda i: (0, i, 0))
    head_shape = lambda n, w: jax.ShapeDtypeStruct((n, N_TOK, w), BF16)
    return pl.pallas_call(
        _proj_kernel,
        grid=(TOK_TILES,),
        in_specs=_lat_ctx_specs(D_MODEL, ctx_tile0) + [
                  pl.BlockSpec((1, 6, D_MODEL), lambda i: (_mod_index(i), 0, 0)),
                  pl.BlockSpec((1, D_MODEL), const2),
                  pl.BlockSpec((D_MODEL, IN_PAD), const2, pipeline_mode=pl.Buffered(1)),
                  pl.BlockSpec((2, 256), const2),
                  pl.BlockSpec((MLA_Q_RANK, 2 * MLA_HEADS * LANES), const2,
                               pipeline_mode=pl.Buffered(1)),
                  pl.BlockSpec((256, 2 * MLA_HEADS * LANES + MLA_HEADS * MLA_V), const2,
                               pipeline_mode=pl.Buffered(1)),
                  pl.BlockSpec((8, LANES), const2),
                  pl.BlockSpec((4, TM, LANES),
                               lambda i: (0, jnp.where(i < LAT_TILES, i % SEQ_TILES, SEQ_TILES), 0))],
        out_specs=[head_out(8, LANES), head_out(8, LANES), head_out(8, V_WIDTH),
                   head_out(4, LANES), head_out(4, LANES), head_out(4, V_WIDTH)],
        out_shape=[head_shape(8, LANES), head_shape(8, LANES), head_shape(8, V_WIDTH),
                   head_shape(4, LANES), head_shape(4, LANES), head_shape(4, V_WIDTH)],
        compiler_params=_params("parallel"),
        name="projection",
    )(x_lat, x_ctx, mod3, gn, w_in, glat, w_uq, w_kv, g128, tab)


def _pair_rows(q_refs, r0, n, kv_refs):
    nt = (((1,), (1,)), ((), ()))
    out = None
    for u, q_ref in enumerate(q_refs):
        q = q_ref[r0:r0 + n, :]
        scores = [lax.dot_general(q, k_ref[u], nt, preferred_element_type=F32)
                  for k_ref, _ in kv_refs]
        m = functools.reduce(jnp.maximum, [jnp.max(sc, axis=-1, keepdims=True) for sc in scores])
        acc = None
        for sc, (_, v_ref) in zip(scores, kv_refs):
            part = jnp.dot(jnp.exp2((sc - m).astype(BF16)), v_ref[u], preferred_element_type=F32)
            acc = part if acc is None else acc + part
        o = acc[:, 0:LANES] / acc[:, LANES:LANES + 1]
        out = o if out is None else out + o
    return out


def _attn_kernel(*refs, ctx_queries, n_casts):
    q0_ref, q1_ref, kl_ref, kc_ref, vl_ref, vc_ref = refs[:6]
    rest = list(refs[6:])
    qc_refs = (rest.pop(0), rest.pop(0)) if ctx_queries else None
    src_refs = [rest.pop(0) for _ in range(n_casts)]
    o_ref = rest.pop(0)
    oc_ref = rest.pop(0) if ctx_queries else None
    for src_ref in src_refs:
        rest.pop(0)[...] = src_ref[...].astype(BF16)
    for r0 in range(0, SEQ, Q_CHUNK):
        out = _pair_rows((q0_ref, q1_ref), r0, Q_CHUNK, ((kc_ref, vc_ref), (kl_ref, vl_ref)))
        o_ref[r0:r0 + Q_CHUNK, :] = out.astype(o_ref.dtype)
    if ctx_queries:
        oc_ref[...] = _pair_rows(qc_refs, 0, CTX_LEN, ((kc_ref, vc_ref),)).astype(oc_ref.dtype)


def _attention(q, k, v, q_heads, kv_pair, ctx_queries, cast_srcs):
    n_units = 4
    ctx_blk0 = N_LAT // CTX_LEN

    def qspec(which, rows, blk0):
        return pl.BlockSpec((None, rows, LANES), lambda b, p: (q_heads(p)[which], blk0 + b, 0))

    def kvspec(rows, blk0, width):
        return pl.BlockSpec((2, rows, width), lambda b, p: (kv_pair(p), blk0 + b, 0))

    in_specs = [qspec(0, SEQ, 0), qspec(1, SEQ, 0), kvspec(SEQ, 0, LANES),
                kvspec(CTX_LEN, ctx_blk0, LANES), kvspec(SEQ, 0, V_WIDTH),
                kvspec(CTX_LEN, ctx_blk0, V_WIDTH)]
    args = [q, q, k, k, v, v]
    out_specs = [pl.BlockSpec((SEQ, LANES), lambda b, p: (b, p))]
    out_shape = [jax.ShapeDtypeStruct((N_LAT, n_units * LANES), BF16)]
    if ctx_queries:
        in_specs += [qspec(0, CTX_LEN, ctx_blk0), qspec(1, CTX_LEN, ctx_blk0)]
        args += [q, q]
        out_specs.append(pl.BlockSpec((CTX_LEN, LANES), lambda b, p: (b, p)))
        out_shape.append(jax.ShapeDtypeStruct((N_CTX, n_units * LANES), BF16))
    for src in cast_srcs:
        rows, cols = src.shape
        slab = pl.BlockSpec((rows // (BATCH * n_units), cols), lambda b, p: (b * n_units + p, 0))
        in_specs.append(slab)
        args.append(src)
        out_specs.append(slab)
        out_shape.append(jax.ShapeDtypeStruct((rows, cols), BF16))
    outs = list(pl.pallas_call(
        functools.partial(_attn_kernel, ctx_queries=ctx_queries, n_casts=len(cast_srcs)),
        grid=(BATCH, n_units),
        in_specs=in_specs, out_specs=out_specs, out_shape=out_shape,
        compiler_params=_params("parallel", "parallel"),
        name="attention",
    )(*args))
    o_lat = outs.pop(0)
    o_ctx = outs.pop(0) if ctx_queries else None
    return o_lat, o_ctx, outs


_A_HEADS = lambda p: (2 * p, 2 * p + 1)
_A_KV = lambda p: p
_B_HEADS = lambda p: (p, p)
_B_KV = lambda p: p // 2


def _post_body(x, oa, ob, mod_ref, gn_ref, wo_ref):
    y = (jnp.dot(oa, wo_ref[0:512, :], preferred_element_type=F32)
         + jnp.dot(ob, wo_ref[512:1024, :], preferred_element_type=F32))
    x1 = x + mod_ref[0, 2:3, :] * y
    r = lax.rsqrt(jnp.mean(x1 * x1, axis=-1, keepdims=True) + NORM_EPS)
    h2 = (x1 * r * gn_ref[...]) * (1.0 + mod_ref[0, 4:5, :]) + mod_ref[0, 3:4, :]
    return x1, h2


def _swiglu(h, wg_ref, wu_ref, wd_ref):
    y = None
    for lo, hi in zip(FF_SPLITS[:-1], FF_SPLITS[1:]):
        g = jnp.dot(h, wg_ref[:, lo:hi], preferred_element_type=F32)
        u = jnp.dot(h, wu_ref[:, lo:hi], preferred_element_type=F32)
        yc = jnp.dot((_silu(g) * u).astype(BF16), wd_ref[lo:hi, :], preferred_element_type=F32)
        y = yc if y is None else y + yc
    return y


def _post_ffn_kernel(xl_ref, xc_ref, oa_ref, oac_ref, ob_ref, obc_ref, mod_ref, gn_ref, wo_ref,
                     wg_ref, wu_ref, wd_ref, o_ref):
    x1, h2 = _post_body(_tile_rows(xl_ref, xc_ref), _tile_rows(oa_ref, oac_ref),
                        _tile_rows(ob_ref, obc_ref), mod_ref, gn_ref, wo_ref)
    y = _swiglu(h2.astype(BF16), wg_ref, wu_ref, wd_ref)
    o_ref[...] = x1 + mod_ref[0, 5:6, :] * y


def _post_router_kernel(x_ref, oa_ref, ob_ref, mod_ref, gn_ref, wo_ref, wr_ref,
                        x1_ref, hrow_ref, route_ref, route_t_ref, count_ref, carry_ref):
    @pl.when(pl.program_id(0) == 0)
    def _():
        carry_ref[...] = jnp.zeros_like(carry_ref)

    n = ROUTE_CHUNK
    lane = lax.broadcasted_iota(jnp.int32, (n, LANES), 1)
    valid = lane < N_EXPERTS
    row = lax.broadcasted_iota(jnp.int32, (n, n), 0)
    col = lax.broadcasted_iota(jnp.int32, (n, n), 1)
    before = jnp.where(col < row, 1.0, 0.0).astype(BF16)
    carry = carry_ref[0:1, :]

    for r0 in range(0, TM, n):
        rows = slice(r0, r0 + n)
        x1, h2 = _post_body(x_ref[rows, :], oa_ref[rows, :], ob_ref[rows, :], mod_ref, gn_ref,
                            wo_ref)
        x1_ref[rows, :] = x1
        for s in range(ROW_SUB):
            hrow_ref[pl.ds(r0 * ROW_SUB + s, n, stride=ROW_SUB), :] = h2[:, s * LANES:(s + 1) * LANES]

        h_hi = h2.astype(BF16)
        h_lo = (h2 - h_hi.astype(F32)).astype(BF16)
        part = jnp.dot(h_hi, wr_ref[...], preferred_element_type=F32)
        logits = (part[:, 0:LANES] + part[:, LANES:2 * LANES]
                  + jnp.dot(h_lo, wr_ref[:, 0:LANES], preferred_element_type=F32))
        lg = jnp.where(valid, logits, -jnp.inf)
        e = jnp.exp(lg - jnp.max(lg, axis=-1, keepdims=True))
        probs = e / jnp.sum(e, axis=-1, keepdims=True)
        v1 = jnp.max(probs, axis=-1, keepdims=True)
        i1 = jnp.min(jnp.where(probs == v1, lane, LANES), axis=-1, keepdims=True)
        rest = jnp.where(valid & (lane != i1), probs, -1.0)
        v2 = jnp.max(rest, axis=-1, keepdims=True)
        i2 = jnp.min(jnp.where(rest == v2, lane, LANES), axis=-1, keepdims=True)

        pick1 = lane == i1
        pick2 = lane == i2
        onehot = jnp.where(pick1, 1.0, 0.0) + jnp.where(pick2, 1.0, 0.0)
        seen = jnp.dot(before, onehot.astype(BF16), preferred_element_type=F32) + carry
        rank1 = jnp.sum(jnp.where(pick1, seen, 0.0), axis=-1, keepdims=True)
        rank2 = jnp.sum(jnp.where(pick2, seen, 0.0), axis=-1, keepdims=True)
        carry = carry + jnp.sum(onehot, axis=0, keepdims=True)

        fields = (v1 / (v1 + v2), v2 / (v1 + v2), i1.astype(F32), i2.astype(F32), rank1, rank2)
        route = jnp.zeros((n, LANES), F32)
        for k, val in enumerate(fields):
            route = jnp.where(lane == k, val, route)
        route_ref[rows, :] = route
        route_t_ref[:, rows] = jnp.transpose(route)[0:8, :]

    carry_ref[...] = jnp.broadcast_to(carry, carry_ref.shape)
    count_ref[...] = jnp.broadcast_to(carry, count_ref.shape)


def _post_specs():
    const2 = lambda i: (0, 0)
    return [pl.BlockSpec((1, 6, D_MODEL), lambda i: (_mod_index(i), 0, 0)),
            pl.BlockSpec((1, D_MODEL), const2),
            pl.BlockSpec((D_MODEL, D_MODEL), const2, pipeline_mode=pl.Buffered(1))]


def _post_dense_ffn(x_lat, x_ctx, o_lat, o_ctx, mod3, gn, w_out, wg, wu, wd):
    resident = lambda shape: pl.BlockSpec(shape, lambda i: (0, 0), pipeline_mode=pl.Buffered(1))
    return pl.pallas_call(
        _post_ffn_kernel, grid=(TOK_TILES,),
        in_specs=(_lat_ctx_specs(D_MODEL, 0) + _lat_ctx_specs(512, 0) + _lat_ctx_specs(512, 0)
                  + _post_specs() + [resident((D_MODEL, DENSE_FF)), resident((D_MODEL, DENSE_FF)),
                                     resident((DENSE_FF, D_MODEL))]),
        out_specs=pl.BlockSpec((TM, D_MODEL), lambda i: (i, 0)),
        out_shape=jax.ShapeDtypeStruct((N_TOK, D_MODEL), F32),
        compiler_params=_params("parallel"), name="post_attention_ffn",
    )(x_lat, x_ctx, o_lat[0], o_ctx[0], o_lat[1], o_ctx[1], mod3, gn, w_out, wg, wu, wd)


def _post_attention_router(x_lat, o_lat, mod3, gn, w_out, w_router):
    const2 = lambda i: (0, 0)
    n_tiles = LAT_TILES
    rows = n_tiles * TM
    x1_spec = pl.BlockSpec((TM, D_MODEL), lambda i: (i, 0))
    x1_shape = jax.ShapeDtypeStruct((rows, D_MODEL), F32)
    half_spec = pl.BlockSpec((TM, 512), lambda i: (i, 0))
    in_specs = [x1_spec, half_spec, half_spec] + _post_specs()
    args = [x_lat, *o_lat, mod3, gn, w_out]
    return pl.pallas_call(
        _post_router_kernel, grid=(n_tiles,),
        in_specs=in_specs + [pl.BlockSpec((D_MODEL, 2 * LANES), const2)],
        out_specs=[x1_spec,
                   pl.BlockSpec((TM * ROW_SUB, LANES), lambda i: (i, 0)),
                   pl.BlockSpec((TM, LANES), lambda i: (i, 0)),
                   pl.BlockSpec((8, TM), lambda i: (0, i)),
                   pl.BlockSpec((8, LANES), const2)],
        out_shape=[x1_shape,
                   jax.ShapeDtypeStruct((rows * ROW_SUB, LANES), F32),
                   jax.ShapeDtypeStruct((rows, LANES), F32),
                   jax.ShapeDtypeStruct((8, rows), F32),
                   jax.ShapeDtypeStruct((8, LANES), F32)],
        scratch_shapes=[pltpu.VMEM((8, LANES), F32)],
        compiler_params=_params("arbitrary"), name="post_attention_router",
    )(*args, w_router)


def _row(ref, r):
    return ref.at[pl.ds(pl.multiple_of(r * ROW_SUB, ROW_SUB), ROW_SUB)]


def _rows_to_matrix(ref, n):
    return jnp.concatenate([ref[pl.ds(s, n, stride=ROW_SUB), :] for s in range(ROW_SUB)], axis=-1)


def _tile_row(buf, j):
    return buf.at[j // 8, :, j % 8, :]


def _invert_kernel(tiles_ref, dest_ref, inv_ref):
    def clear_tile(t):
        def clear(r, carry):
            inv_ref[t * TMX + r] = -1
            return carry
        lax.fori_loop(0, TMX, clear, 0, unroll=16)

    def clear_tail(t, carry):
        clear_tile(t)
        return carry

    def place(a, carry):
        inv_ref[dest_ref[a]] = a
        return carry

    for e in range(N_EXPERTS):
        clear_tile(tiles_ref[e])
    lax.fori_loop(tiles_ref[N_EXPERTS], SORT_TILES, clear_tail, 0)
    lax.fori_loop(0, 2 * N_LAT, place, 0, unroll=16)


def _invert(clear_tiles, dest_flat):
    return pl.pallas_call(
        _invert_kernel,
        in_specs=[pl.BlockSpec(memory_space=pltpu.SMEM), pl.BlockSpec(memory_space=pltpu.SMEM)],
        out_specs=pl.BlockSpec(memory_space=pltpu.SMEM),
        out_shape=jax.ShapeDtypeStruct((SORT_ROWS,), jnp.int32),
        name="moe_invert",
    )(clear_tiles, dest_flat)


def _expert_kernel(te_ref, nu_ref, inv_ref, h_ref, wg_ref, wu_ref, wd_ref, y2_ref,
                   xbuf, ybuf, sem_g, sem_s):
    del te_ref
    t = pl.program_id(0)
    n_used = nu_ref[0]
    slot = lax.rem(t, 2)
    other = 1 - slot

    def gather_rows(tile, s):
        base = tile * TMX
        for j in range(TMX):
            token = jnp.maximum(inv_ref[base + j], 0) & (N_LAT - 1)
            pltpu.make_async_copy(_row(h_ref, token), _tile_row(xbuf.at[s], j),
                                  sem_g.at[s]).start(priority=j % 2)

    def scatter_rows(tile, s, real, scratch_block):
        base = tile * TMX
        for j in range(TMX):
            code = inv_ref[base + j]
            row = jnp.where(jnp.logical_and(code >= 0, real), code,
                            2 * N_LAT + scratch_block * TMX + j)
            pltpu.make_async_copy(_tile_row(ybuf.at[s], j), _row(y2_ref, row),
                                  sem_s.at[s]).start(priority=j % 2)

    def wait_gather(s):
        pltpu.make_async_copy(xbuf.at[s], xbuf.at[s], sem_g.at[s]).wait()

    def wait_scatter(s):
        pltpu.make_async_copy(ybuf.at[s], ybuf.at[s], sem_s.at[s]).wait()

    @pl.when(t == 0)
    def _():
        ybuf[...] = jnp.zeros_like(ybuf)
        gather_rows(0, 0)

    @pl.when(t < n_used)
    def _():
        wait_gather(slot)
        x = jnp.concatenate([xbuf[slot, :, m].reshape(TMX, LANES) for m in range(ROW_SUB)],
                            axis=-1).astype(BF16)
        gather_rows(jnp.minimum(t + 1, n_used - 1), other)
        scatter_rows(jnp.maximum(t - 1, 0), other, t > 0, other)
        y = _swiglu(x, wg_ref.at[0], wu_ref.at[0], wd_ref.at[0])

        @pl.when(t > 0)
        def _():
            wait_scatter(slot)

        for m in range(ROW_SUB):
            ybuf[slot, :, m] = y[:, m * LANES:(m + 1) * LANES].reshape(TMX // 8, 8, LANES)

    @pl.when(t == n_used)
    def _():
        wait_gather(slot)
        scatter_rows(n_used - 1, other, True, other)
        wait_scatter(other)
        wait_scatter(slot)
        scatter_rows(n_used - 1, slot, False, 0)
        wait_scatter(slot)


def _experts(tile_expert, n_used, inv, hrow, wg, wu, wd):
    one = pl.Buffered(1)
    wspec = lambda shape: pl.BlockSpec(shape, lambda t, te, nu, iv: (te[t], 0, 0), pipeline_mode=one)
    grid_spec = pltpu.PrefetchScalarGridSpec(
        num_scalar_prefetch=3,
        grid=(SORT_TILES + 1,),
        in_specs=[pl.BlockSpec(memory_space=pl.ANY),
                  wspec((1, D_MODEL, EXPERT_FF)), wspec((1, D_MODEL, EXPERT_FF)),
                  wspec((1, EXPERT_FF, D_MODEL))],
        out_specs=pl.BlockSpec(memory_space=pl.ANY),
        scratch_shapes=[pltpu.VMEM((2, TMX // 8, ROW_SUB, 8, LANES), F32),
                        pltpu.VMEM((2, TMX // 8, ROW_SUB, 8, LANES), F32),
                        pltpu.SemaphoreType.DMA((2,)), pltpu.SemaphoreType.DMA((2,))])
    return pl.pallas_call(
        _expert_kernel,
        grid_spec=grid_spec,
        out_shape=jax.ShapeDtypeStruct((Y2_ROWS * ROW_SUB, LANES), F32),
        compiler_params=_params("arbitrary"),
        name="moe_experts",
    )(tile_expert, n_used, inv, hrow, wg, wu, wd)


def _combine_kernel(ya_ref, yb_ref, x1_ref, route_ref, mod_ref, o_ref):
    route = route_ref[...]
    y = (route[:, 0:1] * _rows_to_matrix(ya_ref, TM) + route[:, 1:2] * _rows_to_matrix(yb_ref, TM))
    o_ref[...] = x1_ref[...] + mod_ref[0, 5:6, :] * y


def _combine(y2, x1, route, mod3):
    return pl.pallas_call(
        _combine_kernel,
        grid=(LAT_TILES,),
        in_specs=[pl.BlockSpec((TM * ROW_SUB, LANES), lambda i: (i, 0)),
                  pl.BlockSpec((TM * ROW_SUB, LANES), lambda i: (LAT_TILES + i, 0)),
                  pl.BlockSpec((TM, D_MODEL), lambda i: (i, 0)),
                  pl.BlockSpec((TM, LANES), lambda i: (i, 0)),
                  pl.BlockSpec((1, 6, D_MODEL), lambda i: (i // SEQ_TILES, 0, 0))],
        out_specs=pl.BlockSpec((TM, D_MODEL), lambda i: (i, 0)),
        out_shape=jax.ShapeDtypeStruct((N_LAT, D_MODEL), F32),
        compiler_params=_params("parallel"),
        name="moe_combine",
    )(y2, y2, x1, route, mod3)


def _moe_ffn(hrow, x1, route, route_t, counts, mod3, wg, wu, wd):
    expert = route_t[2:4].astype(jnp.int32)
    rank = route_t[4:6].astype(jnp.int32)
    count = counts[0, :N_EXPERTS].astype(jnp.int32)
    tiles = (count + TMX - 1) // TMX
    tile_end = jnp.cumsum(tiles)
    first_row = (tile_end - tiles) * TMX
    ids = jnp.arange(N_EXPERTS, dtype=jnp.int32)[:, None, None]
    dest = jnp.sum(jnp.where(expert[None] == ids, first_row[:, None, None], 0), axis=0) + rank
    last_tiles = jnp.maximum(tile_end - 1, 0)
    inv = _invert(jnp.concatenate([last_tiles, tile_end[N_EXPERTS - 1:]]), dest.reshape(-1))
    n_used = tile_end[N_EXPERTS - 1:]
    tile_ids = jnp.minimum(jnp.arange(SORT_TILES + 1), n_used[0] - 1)
    tile_expert = jnp.sum(tile_ids[:, None] >= tile_end[None, :], axis=1).astype(jnp.int32)
    y2 = _experts(tile_expert, n_used.astype(jnp.int32), inv, hrow, wg, wu, wd)
    return _combine(y2, x1, route, mod3)


def _partner(half):
    idx = np.arange(4 * half)
    return np.where((idx // half) % 2 == 0, idx + half, idx - half)


_PARTNER_A = np.concatenate([np.arange(MLA_NOPE), MLA_NOPE + _partner(MLA_ROPE // 4),
                             np.arange(MLA_QK, LANES)])
_PARTNER_B = np.concatenate([_partner(GQA_HEAD_DIM // 4), GQA_HEAD_DIM + _partner(GQA_HEAD_DIM // 4)])


def _rope_tables():
    f32 = np.float32
    t = np.arange(SEQ)
    row = (t // GRID_W).astype(f32)
    col = (t % GRID_W).astype(f32)

    def one_axis(pos, half):
        freqs = f32(ROPE_THETA) ** (-np.arange(half, dtype=f32) / f32(half))
        ang = pos[:, None] * freqs[None, :]
        cos, sin = np.cos(ang), np.sin(ang)
        return np.concatenate([cos, cos], -1), np.concatenate([-sin, sin], -1)

    def two_axes(half):
        r, c = one_axis(row, half), one_axis(col, half)
        return [np.concatenate([a, b], -1) for a, b in zip(r, c)]

    def pad_a(tbl, fill):
        return np.concatenate([np.full((SEQ, MLA_NOPE), fill, f32), tbl,
                               np.full((SEQ, LANES - MLA_QK), fill, f32)], -1)

    ta = two_axes(MLA_ROPE // 4)
    tb = [np.concatenate([x, x], -1) for x in two_axes(GQA_HEAD_DIM // 4)]
    tabs = [pad_a(ta[0], 1.0), pad_a(ta[1], 0.0)] + tb
    ident = [np.ones((TM, LANES), f32), np.zeros((TM, LANES), f32)]
    return np.stack([np.concatenate([tbl, idn], 0)
                     for tbl, idn in zip(tabs, ident + ident)]).astype(f32)


def _layer_weights(w_in, q_lat_norm, w_uq, kv_lat_norm, w_ukv, mla_q_gain, mla_k_gain,
                   gqa_q_gain, gqa_k_gain):
    split = MLA_Q_RANK + MLA_KV_RANK + MLA_ROPE
    perm_a = (np.arange(MLA_HEADS)[:, None] * LANES + _PARTNER_A[None, :]).reshape(-1)
    perm_b = (np.arange(GQA_HEADS // 2)[:, None] * LANES + _PARTNER_B[None, :]).reshape(-1)
    w_qb = w_in[:, split:split + GQA_HEADS * GQA_HEAD_DIM]
    w_kb = w_in[:, split + 512:split + 640]
    w_in_p = jnp.concatenate([w_in[:, :split], jnp.zeros((D_MODEL, 512 - split), F32),
                              w_in[:, split:], w_qb[:, perm_b], w_kb[:, _PARTNER_B]],
                             axis=1).astype(BF16)
    w_uq_p = jnp.pad(w_uq.reshape(MLA_Q_RANK, MLA_HEADS, MLA_QK),
                     ((0, 0), (0, 0), (0, LANES - MLA_QK))).reshape(MLA_Q_RANK, MLA_HEADS * LANES)
    w_uq_p = jnp.concatenate([w_uq_p, w_uq_p[:, perm_a]], axis=1)
    ukv = w_ukv.reshape(MLA_KV_RANK, MLA_HEADS, MLA_NOPE + MLA_V)
    w_k = jnp.pad(ukv[:, :, :MLA_NOPE], ((0, 0), (0, 0), (0, LANES - MLA_NOPE)))
    place = jnp.pad(jnp.eye(MLA_ROPE, dtype=F32), ((0, 0), (MLA_NOPE, LANES - MLA_QK)))
    place = jnp.broadcast_to(place[:, None, :], (MLA_ROPE, MLA_HEADS, LANES))
    w_k = jnp.concatenate([w_k, place,
                           jnp.zeros((256 - MLA_KV_RANK - MLA_ROPE, MLA_HEADS, LANES), F32)], 0)
    w_k = w_k.reshape(256, MLA_HEADS * LANES)
    w_v = jnp.pad(ukv[:, :, MLA_NOPE:].reshape(MLA_KV_RANK, MLA_HEADS * MLA_V),
                  ((0, 256 - MLA_KV_RANK), (0, 0)))
    w_kv = jnp.concatenate([w_k, w_k[:, perm_a], w_v], axis=1)
    glat = jnp.stack([q_lat_norm, jnp.pad(kv_lat_norm, (0, 256 - MLA_KV_RANK))])
    pad_qk = lambda g: jnp.pad(g, (0, LANES - MLA_QK))
    gains = [pad_qk(mla_q_gain) * (MLA_SCALE * LOG2E), pad_qk(mla_k_gain),
             jnp.tile(gqa_q_gain, 2) * (GQA_SCALE * LOG2E), jnp.tile(gqa_k_gain, 2)]
    partners = [_PARTNER_A, _PARTNER_A, _PARTNER_B, _PARTNER_B]
    g128 = jnp.stack([v for g, pm in zip(gains, partners) for v in (g, g[pm])])
    return w_in_p, glat, w_uq_p.astype(BF16), w_kv.astype(BF16), g128


def _mixer(x_lat, x_ctx, ctx_tile0, mod3, tab, norm_attn, w_in, q_lat_norm, w_uq, kv_lat_norm,
           w_ukv, mla_q_gain, mla_k_gain, gqa_q_gain, gqa_k_gain, with_ctx_queries, cast_a, cast_b):
    w_in_p, glat, w_uq_p, w_kv, g128 = _layer_weights(
        w_in, q_lat_norm, w_uq, kv_lat_norm, w_ukv, mla_q_gain, mla_k_gain, gqa_q_gain, gqa_k_gain)
    qa, ka, va, qb, kb, vb = _projection(x_lat, x_ctx, ctx_tile0, mod3,
                                         norm_attn.reshape(1, D_MODEL), w_in_p, glat,
                                         w_uq_p, w_kv, g128, tab)
    oa, oa_ctx, cast_a = _attention(qa, ka, va, _A_HEADS, _A_KV, with_ctx_queries, cast_a)
    ob, ob_ctx, cast_b = _attention(qb, kb, vb, _B_HEADS, _B_KV, with_ctx_queries, cast_b)
    return (oa, ob), ((oa_ctx, ob_ctx) if with_ctx_queries else None), cast_a + cast_b


def kernel(x, c, ctx, c_ctx, l0_w_mod, l0_b_mod, l0_norm_attn, l0_w_in, l0_q_lat_norm, l0_w_uq, l0_kv_lat_norm, l0_w_ukv, l0_mla_q_gain, l0_mla_k_gain, l0_gqa_q_gain, l0_gqa_k_gain, l0_w_out, l0_norm_ffn, l0_ffn_w_gate, l0_ffn_w_up, l0_ffn_w_down, l1_w_mod, l1_b_mod, l1_norm_attn, l1_w_in, l1_q_lat_norm, l1_w_uq, l1_kv_lat_norm, l1_w_ukv, l1_mla_q_gain, l1_mla_k_gain, l1_gqa_q_gain, l1_gqa_k_gain, l1_w_out, l1_norm_ffn, l1_router, l1_exp_w_gate, l1_exp_w_up, l1_exp_w_down):
    x_lat = x.reshape(N_LAT, D_MODEL)
    x_ctx = ctx.reshape(N_CTX, D_MODEL)
    cc = jnp.concatenate([c, c_ctx[None, :], jnp.zeros((16 - BATCH - 1, D_MODEL), F32)], axis=0)
    tab = jnp.asarray(_rope_tables())

    mod3 = _modulation(cc, l0_w_mod, l0_b_mod)
    o_lat, o_ctx, (wg_x, wg_0, wu_x, wu_0) = _mixer(
        x_lat, x_ctx, 0, mod3, tab, l0_norm_attn, l0_w_in, l0_q_lat_norm, l0_w_uq, l0_kv_lat_norm,
        l0_w_ukv, l0_mla_q_gain, l0_mla_k_gain, l0_gqa_q_gain, l0_gqa_k_gain, True,
        [l1_exp_w_gate.reshape(N_EXPERTS * D_MODEL, EXPERT_FF), l0_ffn_w_gate],
        [l1_exp_w_up.reshape(N_EXPERTS * D_MODEL, EXPERT_FF), l0_ffn_w_up])
    xall = _post_dense_ffn(x_lat, x_ctx, o_lat, o_ctx, mod3, l0_norm_ffn.reshape(1, D_MODEL),
                           l0_w_out.astype(BF16), wg_0, wu_0, l0_ffn_w_down.astype(BF16))

    mod3 = _modulation(cc, l1_w_mod, l1_b_mod)
    o_lat, _, (wd_x,) = _mixer(
        xall, xall, LAT_TILES, mod3, tab, l1_norm_attn, l1_w_in, l1_q_lat_norm, l1_w_uq,
        l1_kv_lat_norm, l1_w_ukv, l1_mla_q_gain, l1_mla_k_gain, l1_gqa_q_gain, l1_gqa_k_gain,
        False, [l1_exp_w_down.reshape(N_EXPERTS * EXPERT_FF, D_MODEL)], [])
    w_router = jnp.pad(l1_router, ((0, 0), (0, LANES - N_EXPERTS)))
    w_router_hi = w_router.astype(BF16)
    w_router = jnp.concatenate([w_router_hi, (w_router - w_router_hi.astype(F32)).astype(BF16)],
                               axis=1)
    x1, hrow, route, route_t, counts = _post_attention_router(xall, o_lat, mod3,
                                                     l1_norm_ffn.reshape(1, D_MODEL),
                                                     l1_w_out.astype(BF16), w_router)
    out = _moe_ffn(hrow, x1, route, route_t, counts, mod3,
                   wg_x.reshape(N_EXPERTS, D_MODEL, EXPERT_FF),
                   wu_x.reshape(N_EXPERTS, D_MODEL, EXPERT_FF),
                   wd_x.reshape(N_EXPERTS, EXPERT_FF, D_MODEL))
    return out.reshape(BATCH, SEQ, D_MODEL)
```

```python
import functools
import math

import jax
import jax.numpy as jnp
import numpy as np
from jax import lax
from jax.experimental import pallas as pl
from jax.experimental.pallas import tpu as pltpu

D_MODEL = 1024
BATCH = 8
SEQ = 2048
CTX_LEN = 256
GRID_W = 64
MLA_HEADS = 8
MLA_NOPE = 64
MLA_ROPE = 32
MLA_V = 64
MLA_QK = MLA_NOPE + MLA_ROPE
MLA_Q_RANK = 256
MLA_KV_RANK = 128
GQA_HEADS = 8
GQA_KV_HEADS = 2
GQA_HEAD_DIM = 64
DENSE_FF = 2816
N_EXPERTS = 8
EXPERT_FF = 2816
ROPE_THETA = 10000.0
NORM_EPS = 1e-6
LOG2E = math.log2(math.e)
MLA_SCALE = MLA_QK ** -0.5
GQA_SCALE = GQA_HEAD_DIM ** -0.5

LANES = 128
N_LAT = BATCH * SEQ
N_CTX = BATCH * CTX_LEN
N_TOK = N_LAT + N_CTX
TM = 512
LAT_TILES = N_LAT // TM
TOK_TILES = N_TOK // TM
SEQ_TILES = SEQ // TM
Q_CHUNK = 256
PROJ_CHUNK = 512
ROUTE_CHUNK = 512
V_WIDTH = 2 * LANES
IN_PAD = 1920
FF_SPLITS = (0, 1536, 2816)
ROW_SUB = D_MODEL // LANES
TMX = 512
SORT_TILES = 2 * N_LAT // TMX + N_EXPERTS
SORT_ROWS = SORT_TILES * TMX
Y2_ROWS = 2 * N_LAT + 2 * TMX
assert N_LAT & (N_LAT - 1) == 0
VMEM_LIMIT = 56 * 1024 * 1024

F32 = jnp.float32
BF16 = jnp.bfloat16


def _silu(x):
    return x / (1.0 + jnp.exp(-x))


def _params(*sem):
    return pltpu.CompilerParams(dimension_semantics=sem, vmem_limit_bytes=VMEM_LIMIT)


def _mod_kernel(c_ref, w_ref, b_ref, o_ref):
    s = _silu(c_ref[...])
    o_ref[...] = jnp.dot(s, w_ref[...], precision=lax.Precision.HIGHEST,
                         preferred_element_type=F32) + b_ref[...]


def _modulation(cc, w_mod, b_mod):
    n = w_mod.shape[1]
    bn = 1024
    out = pl.pallas_call(
        _mod_kernel,
        grid=(n // bn,),
        in_specs=[pl.BlockSpec((16, D_MODEL), lambda j: (0, 0)),
                  pl.BlockSpec((D_MODEL, bn), lambda j: (0, j)),
                  pl.BlockSpec((1, bn), lambda j: (0, j))],
        out_specs=pl.BlockSpec((16, bn), lambda j: (0, j)),
        out_shape=jax.ShapeDtypeStruct((16, n), F32),
        compiler_params=_params("parallel"),
        name="modulation",
    )(cc, w_mod, b_mod.reshape(1, n))
    return out.reshape(16, 6, D_MODEL)


def _tile_rows(lat_ref, ctx_ref):
    return jnp.where(pl.program_id(0) < LAT_TILES, lat_ref[...], ctx_ref[...])


def _proj_kernel(xl_ref, xc_ref, mod_ref, gn_ref, win_ref, glat_ref, wuq_ref, wkv_ref, g128_ref,
                 tab_ref, qa_ref, ka_ref, va_ref, qb_ref, kb_ref, vb_ref):
    latent = pl.program_id(0) < LAT_TILES
    n = PROJ_CHUNK
    lane = lax.broadcasted_iota(jnp.int32, (n, LANES), 1)
    lo = lane < 64
    ones_col = jnp.where(lane == 0, 1.0, 0.0).astype(BF16)

    for r0 in range(0, TM, n):
        rows = slice(r0, r0 + n)
        x = jnp.where(latent, xl_ref[rows, :], xc_ref[rows, :])
        r = lax.rsqrt(jnp.mean(x * x, axis=-1, keepdims=True) + NORM_EPS)
        h = (x * r * gn_ref[...]) * (1.0 + mod_ref[0, 1:2, :]) + mod_ref[0, 0:1, :]
        p = jnp.dot(h.astype(BF16), win_ref[...], preferred_element_type=F32)

        def tables(base, row):
            return (tab_ref[base, rows, :] * g128_ref[row:row + 1, :],
                    tab_ref[base + 1, rows, :] * g128_ref[row + 1:row + 2, :])

        cq = p[:, 0:256]
        rq = lax.rsqrt(jnp.mean(cq * cq, axis=-1, keepdims=True) + NORM_EPS)
        qa = jnp.dot((cq * rq * glat_ref[0:1, :]).astype(BF16), wuq_ref[...],
                     preferred_element_type=F32)
        ckv = p[:, 256:384]
        rkv = lax.rsqrt(jnp.mean(ckv * ckv, axis=-1, keepdims=True) + NORM_EPS)
        slab = jnp.concatenate([ckv * rkv * glat_ref[1:2, 0:128], p[:, 384:512]], axis=-1)
        kv = jnp.dot(slab.astype(BF16), wkv_ref[...], preferred_element_type=F32)

        def head_a(src, hd, cos_g, sin_g):
            blk = src[:, hd * LANES:(hd + 1) * LANES]
            partner = src[:, 1024 + hd * LANES:1024 + (hd + 1) * LANES]
            rr = lax.rsqrt(jnp.sum(blk * blk, axis=-1, keepdims=True) * (1.0 / MLA_QK) + NORM_EPS)
            return ((blk * cos_g + partner * sin_g) * rr).astype(BF16)

        cq_g, sq_g = tables(0, 0)
        ck_g, sk_g = tables(0, 2)
        for hd in range(MLA_HEADS):
            qa_ref[hd, rows, :] = head_a(qa, hd, cq_g, sq_g)
            ka_ref[hd, rows, :] = head_a(kv, hd, ck_g, sk_g)
        for pr in range(MLA_HEADS // 2):
            blk = kv[:, 2048 + pr * LANES:2048 + (pr + 1) * LANES]
            va_ref[2 * pr, rows, 0:LANES] = jnp.where(lo, blk, 0.0).astype(BF16)
            va_ref[2 * pr + 1, rows, 0:LANES] = jnp.where(lo, 0.0, blk).astype(BF16)
        for hd in range(MLA_HEADS):
            va_ref[hd, rows, LANES:V_WIDTH] = ones_col

        def pair_b(blk, partner, cos_g, sin_g):
            sq = blk * blk
            s_lo = jnp.sum(jnp.where(lo, sq, 0.0), axis=-1, keepdims=True)
            s_hi = jnp.sum(jnp.where(lo, 0.0, sq), axis=-1, keepdims=True)
            rr = jnp.where(lo, lax.rsqrt(s_lo * (1.0 / GQA_HEAD_DIM) + NORM_EPS),
                           lax.rsqrt(s_hi * (1.0 / GQA_HEAD_DIM) + NORM_EPS))
            return (blk * cos_g + partner * sin_g) * rr

        cq_g, sq_g = tables(2, 4)
        ck_g, sk_g = tables(2, 6)
        for pr in range(GQA_HEADS // 2):
            blk = p[:, 512 + pr * LANES:512 + (pr + 1) * LANES]
            partner = p[:, 1280 + pr * LANES:1280 + (pr + 1) * LANES]
            qb_ref[pr, rows, :] = pair_b(blk, partner, cq_g, sq_g).astype(BF16)
        kb = pair_b(p[:, 1024:1152], p[:, 1792:1920], ck_g, sk_g)
        kb_sw = pltpu.roll(kb, 64, 1)
        kb_ref[0, rows, :] = jnp.where(lo, kb, 0.0).astype(BF16)
        kb_ref[1, rows, :] = jnp.where(lo, 0.0, kb_sw).astype(BF16)
        kb_ref[2, rows, :] = jnp.where(lo, kb_sw, 0.0).astype(BF16)
        kb_ref[3, rows, :] = jnp.where(lo, 0.0, kb).astype(BF16)
        vb = p[:, 1152:1280]
        vb_sw = pltpu.roll(vb, 64, 1)
        vb_ref[0, rows, 0:LANES] = jnp.where(lo, vb, 0.0).astype(BF16)
        vb_ref[1, rows, 0:LANES] = jnp.where(lo, 0.0, vb_sw).astype(BF16)
        vb_ref[2, rows, 0:LANES] = jnp.where(lo, vb_sw, 0.0).astype(BF16)
        vb_ref[3, rows, 0:LANES] = jnp.where(lo, 0.0, vb).astype(BF16)
        for j in range(2 * GQA_KV_HEADS):
            vb_ref[j, rows, LANES:V_WIDTH] = ones_col


def _mod_index(i):
    return jnp.where(i < LAT_TILES, i // SEQ_TILES, BATCH)


def _lat_ctx_specs(width, ctx_tile0):
    return [pl.BlockSpec((TM, width), lambda i: (jnp.minimum(i, LAT_TILES - 1), 0)),
            pl.BlockSpec((TM, width), lambda i: (ctx_tile0 + jnp.maximum(i - LAT_TILES, 0), 0))]


def _projection(x_lat, x_ctx, ctx_tile0, mod3, gn, w_in, glat, w_uq, w_kv, g128, tab):
    const2 = lambda i: (0, 0)
    head_out = lambda n, w: pl.BlockSpec((n, TM, w), lambda i: (0, i, 0))
    head_shape = lambda n, w: jax.ShapeDtypeStruct((n, N_TOK, w), BF16)
    return pl.pallas_call(
        _proj_kernel,
        grid=(TOK_TILES,),
        in_specs=_lat_ctx_specs(D_MODEL, ctx_tile0) + [
                  pl.BlockSpec((1, 6, D_MODEL), lambda i: (_mod_index(i), 0, 0)),
                  pl.BlockSpec((1, D_MODEL), const2),
                  pl.BlockSpec((D_MODEL, IN_PAD), const2, pipeline_mode=pl.Buffered(1)),
                  pl.BlockSpec((2, 256), const2),
                  pl.BlockSpec((MLA_Q_RANK, 2 * MLA_HEADS * LANES), const2,
                               pipeline_mode=pl.Buffered(1)),
                  pl.BlockSpec((256, 2 * MLA_HEADS * LANES + MLA_HEADS * MLA_V), const2,
                               pipeline_mode=pl.Buffered(1)),
                  pl.BlockSpec((8, LANES), const2),
                  pl.BlockSpec((4, TM, LANES),
                               lambda i: (0, jnp.where(i < LAT_TILES, i % SEQ_TILES, SEQ_TILES), 0))],
        out_specs=[head_out(8, LANES), head_out(8, LANES), head_out(8, V_WIDTH),
                   head_out(4, LANES), head_out(4, LANES), head_out(4, V_WIDTH)],
        out_shape=[head_shape(8, LANES), head_shape(8, LANES), head_shape(8, V_WIDTH),
                   head_shape(4, LANES), head_shape(4, LANES), head_shape(4, V_WIDTH)],
        compiler_params=_params("parallel"),
        name="projection",
    )(x_lat, x_ctx, mod3, gn, w_in, glat, w_uq, w_kv, g128, tab)


def _pair_rows(q_refs, r0, n, kv_refs):
    nt = (((1,), (1,)), ((), ()))
    out = None
    for u, q_ref in enumerate(q_refs):
        q = q_ref[r0:r0 + n, :]
        scores = [lax.dot_general(q, k_ref[u], nt, preferred_element_type=F32)
                  for k_ref, _ in kv_refs]
        m = functools.reduce(jnp.maximum, [jnp.max(sc, axis=-1, keepdims=True) for sc in scores])
        acc = None
        for sc, (_, v_ref) in zip(scores, kv_refs):
            part = jnp.dot(jnp.exp2((sc - m).astype(BF16)), v_ref[u], preferred_element_type=F32)
            acc = part if acc is None else acc + part
        o = acc[:, 0:LANES] / acc[:, LANES:LANES + 1]
        out = o if out is None else out + o
    return out


def _attn_kernel(*refs, ctx_queries, n_casts):
    q0_ref, q1_ref, kl_ref, kc_ref, vl_ref, vc_ref = refs[:6]
    rest = list(refs[6:])
    qc_refs = (rest.pop(0), rest.pop(0)) if ctx_queries else None
    src_refs = [rest.pop(0) for _ in range(n_casts)]
    o_ref = rest.pop(0)
    oc_ref = rest.pop(0) if ctx_queries else None
    for src_ref in src_refs:
        rest.pop(0)[...] = src_ref[...].astype(BF16)
    for r0 in range(0, SEQ, Q_CHUNK):
        out = _pair_rows((q0_ref, q1_ref), r0, Q_CHUNK, ((kc_ref, vc_ref), (kl_ref, vl_ref)))
        o_ref[r0:r0 + Q_CHUNK, :] = out.astype(o_ref.dtype)
    if ctx_queries:
        oc_ref[...] = _pair_rows(qc_refs, 0, CTX_LEN, ((kc_ref, vc_ref),)).astype(oc_ref.dtype)


def _attention(q, k, v, q_heads, kv_pair, ctx_queries, cast_srcs):
    n_units = 4
    ctx_blk0 = N_LAT // CTX_LEN

    def qspec(which, rows, blk0):
        return pl.BlockSpec((None, rows, LANES), lambda b, p: (q_heads(p)[which], blk0 + b, 0))

    def kvspec(rows, blk0, width):
        return pl.BlockSpec((2, rows, width), lambda b, p: (kv_pair(p), blk0 + b, 0))

    in_specs = [qspec(0, SEQ, 0), qspec(1, SEQ, 0), kvspec(SEQ, 0, LANES),
                kvspec(CTX_LEN, ctx_blk0, LANES), kvspec(SEQ, 0, V_WIDTH),
                kvspec(CTX_LEN, ctx_blk0, V_WIDTH)]
    args = [q, q, k, k, v, v]
    out_specs = [pl.BlockSpec((SEQ, LANES), lambda b, p: (b, p))]
    out_shape = [jax.ShapeDtypeStruct((N_LAT, n_units * LANES), BF16)]
    if ctx_queries:
        in_specs += [qspec(0, CTX_LEN, ctx_blk0), qspec(1, CTX_LEN, ctx_blk0)]
        args += [q, q]
        out_specs.append(pl.BlockSpec((CTX_LEN, LANES), lambda b, p: (b, p)))
        out_shape.append(jax.ShapeDtypeStruct((N_CTX, n_units * LANES), BF16))
    for src in cast_srcs:
        rows, cols = src.shape
        slab = pl.BlockSpec((rows // (BATCH * n_units), cols), lambda b, p: (b * n_units + p, 0))
        in_specs.append(slab)
        args.append(src)
        out_specs.append(slab)
        out_shape.append(jax.ShapeDtypeStruct((rows, cols), BF16))
    outs = list(pl.pallas_call(
        functools.partial(_attn_kernel, ctx_queries=ctx_queries, n_casts=len(cast_srcs)),
        grid=(BATCH, n_units),
        in_specs=in_specs, out_specs=out_specs, out_shape=out_shape,
        compiler_params=_params("parallel", "parallel"),
        name="attention",
    )(*args))
    o_lat = outs.pop(0)
    o_ctx = outs.pop(0) if ctx_queries else None
    return o_lat, o_ctx, outs


_A_HEADS = lambda p: (2 * p, 2 * p + 1)
_A_KV = lambda p: p
_B_HEADS = lambda p: (p, p)
_B_KV = lambda p: p // 2


def _post_body(x, oa, ob, mod_ref, gn_ref, wo_ref):
    y = (jnp.dot(oa, wo_ref[0:512, :], preferred_element_type=F32)
         + jnp.dot(ob, wo_ref[512:1024, :], preferred_element_type=F32))
    x1 = x + mod_ref[0, 2:3, :] * y
    r = lax.rsqrt(jnp.mean(x1 * x1, axis=-1, keepdims=True) + NORM_EPS)
    h2 = (x1 * r * gn_ref[...]) * (1.0 + mod_ref[0, 4:5, :]) + mod_ref[0, 3:4, :]
    return x1, h2


def _swiglu(h, wg_ref, wu_ref, wd_ref):
    y = None
    for lo, hi in zip(FF_SPLITS[:-1], FF_SPLITS[1:]):
        g = jnp.dot(h, wg_ref[:, lo:hi], preferred_element_type=F32)
        u = jnp.dot(h, wu_ref[:, lo:hi], preferred_element_type=F32)
        yc = jnp.dot((_silu(g) * u).astype(BF16), wd_ref[lo:hi, :], preferred_element_type=F32)
        y = yc if y is None else y + yc
    return y


def _post_ffn_kernel(xl_ref, xc_ref, oa_ref, oac_ref, ob_ref, obc_ref, mod_ref, gn_ref, wo_ref,
                     wg_ref, wu_ref, wd_ref, o_ref):
    x1, h2 = _post_body(_tile_rows(xl_ref, xc_ref), _tile_rows(oa_ref, oac_ref),
                        _tile_rows(ob_ref, obc_ref), mod_ref, gn_ref, wo_ref)
    y = _swiglu(h2.astype(BF16), wg_ref, wu_ref, wd_ref)
    o_ref[...] = x1 + mod_ref[0, 5:6, :] * y


def _post_router_kernel(x_ref, oa_ref, ob_ref, mod_ref, gn_ref, wo_ref, wr_ref,
                        x1_ref, hrow_ref, route_ref, route_t_ref, count_ref, carry_ref):
    @pl.when(pl.program_id(0) == 0)
    def _():
        carry_ref[...] = jnp.zeros_like(carry_ref)

    n = ROUTE_CHUNK
    lane = lax.broadcasted_iota(jnp.int32, (n, LANES), 1)
    valid = lane < N_EXPERTS
    row = lax.broadcasted_iota(jnp.int32, (n, n), 0)
    col = lax.broadcasted_iota(jnp.int32, (n, n), 1)
    before = jnp.where(col < row, 1.0, 0.0).astype(BF16)
    carry = carry_ref[0:1, :]

    for r0 in range(0, TM, n):
        rows = slice(r0, r0 + n)
        x1, h2 = _post_body(x_ref[rows, :], oa_ref[rows, :], ob_ref[rows, :], mod_ref, gn_ref,
                            wo_ref)
        x1_ref[rows, :] = x1
        for s in range(ROW_SUB):
            hrow_ref[pl.ds(r0 * ROW_SUB + s, n, stride=ROW_SUB), :] = h2[:, s * LANES:(s + 1) * LANES]

        h_hi = h2.astype(BF16)
        h_lo = (h2 - h_hi.astype(F32)).astype(BF16)
        part = jnp.dot(h_hi, wr_ref[...], preferred_element_type=F32)
        logits = (part[:, 0:LANES] + part[:, LANES:2 * LANES]
                  + jnp.dot(h_lo, wr_ref[:, 0:LANES], preferred_element_type=F32))
        lg = jnp.where(valid, logits, -jnp.inf)
        e = jnp.exp(lg - jnp.max(lg, axis=-1, keepdims=True))
        probs = e / jnp.sum(e, axis=-1, keepdims=True)
        v1 = jnp.max(probs, axis=-1, keepdims=True)
        i1 = jnp.min(jnp.where(probs == v1, lane, LANES), axis=-1, keepdims=True)
        rest = jnp.where(valid & (lane != i1), probs, -1.0)
        v2 = jnp.max(rest, axis=-1, keepdims=True)
        i2 = jnp.min(jnp.where(rest == v2, lane, LANES), axis=-1, keepdims=True)

        pick1 = lane == i1
        pick2 = lane == i2
        onehot = jnp.where(pick1, 1.0, 0.0) + jnp.where(pick2, 1.0, 0.0)
        seen = jnp.dot(before, onehot.astype(BF16), preferred_element_type=F32) + carry
        rank1 = jnp.sum(jnp.where(pick1, seen, 0.0), axis=-1, keepdims=True)
        rank2 = jnp.sum(jnp.where(pick2, seen, 0.0), axis=-1, keepdims=True)
        carry = carry + jnp.sum(onehot, axis=0, keepdims=True)

        fields = (v1 / (v1 + v2), v2 / (v1 + v2), i1.astype(F32), i2.astype(F32), rank1, rank2)
        route = jnp.zeros((n, LANES), F32)
        for k, val in enumerate(fields):
            route = jnp.where(lane == k, val, route)
        route_ref[rows, :] = route
        route_t_ref[:, rows] = jnp.transpose(route)[0:8, :]

    carry_ref[...] = jnp.broadcast_to(carry, carry_ref.shape)
    count_ref[...] = jnp.broadcast_to(carry, count_ref.shape)


def _post_specs():
    const2 = lambda i: (0, 0)
    return [pl.BlockSpec((1, 6, D_MODEL), lambda i: (_mod_index(i), 0, 0)),
            pl.BlockSpec((1, D_MODEL), const2),
            pl.BlockSpec((D_MODEL, D_MODEL), const2, pipeline_mode=pl.Buffered(1))]


def _post_dense_ffn(x_lat, x_ctx, o_lat, o_ctx, mod3, gn, w_out, wg, wu, wd):
    resident = lambda shape: pl.BlockSpec(shape, lambda i: (0, 0), pipeline_mode=pl.Buffered(1))
    return pl.pallas_call(
        _post_ffn_kernel, grid=(TOK_TILES,),
        in_specs=(_lat_ctx_specs(D_MODEL, 0) + _lat_ctx_specs(512, 0) + _lat_ctx_specs(512, 0)
                  + _post_specs() + [resident((D_MODEL, DENSE_FF)), resident((D_MODEL, DENSE_FF)),
                                     resident((DENSE_FF, D_MODEL))]),
        out_specs=pl.BlockSpec((TM, D_MODEL), lambda i: (i, 0)),
        out_shape=jax.ShapeDtypeStruct((N_TOK, D_MODEL), F32),
        compiler_params=_params("parallel"), name="post_attention_ffn",
    )(x_lat, x_ctx, o_lat[0], o_ctx[0], o_lat[1], o_ctx[1], mod3, gn, w_out, wg, wu, wd)


def _post_attention_router(x_lat, o_lat, mod3, gn, w_out, w_router):
    const2 = lambda i: (0, 0)
    n_tiles = LAT_TILES
    rows = n_tiles * TM
    x1_spec = pl.BlockSpec((TM, D_MODEL), lambda i: (i, 0))
    x1_shape = jax.ShapeDtypeStruct((rows, D_MODEL), F32)
    half_spec = pl.BlockSpec((TM, 512), lambda i: (i, 0))
    in_specs = [x1_spec, half_spec, half_spec] + _post_specs()
    args = [x_lat, *o_lat, mod3, gn, w_out]
    return pl.pallas_call(
        _post_router_kernel, grid=(n_tiles,),
        in_specs=in_specs + [pl.BlockSpec((D_MODEL, 2 * LANES), const2)],
        out_specs=[x1_spec,
                   pl.BlockSpec((TM * ROW_SUB, LANES), lambda i: (i, 0)),
                   pl.BlockSpec((TM, LANES), lambda i: (i, 0)),
                   pl.BlockSpec((8, TM), lambda i: (0, i)),
                   pl.BlockSpec((8, LANES), const2)],
        out_shape=[x1_shape,
                   jax.ShapeDtypeStruct((rows * ROW_SUB, LANES), F32),
                   jax.ShapeDtypeStruct((rows, LANES), F32),
                   jax.ShapeDtypeStruct((8, rows), F32),
                   jax.ShapeDtypeStruct((8, LANES), F32)],
        scratch_shapes=[pltpu.VMEM((8, LANES), F32)],
        compiler_params=_params("arbitrary"), name="post_attention_router",
    )(*args, w_router)


def _row(ref, r):
    return ref.at[pl.ds(pl.multiple_of(r * ROW_SUB, ROW_SUB), ROW_SUB)]


def _rows_to_matrix(ref, n):
    return jnp.concatenate([ref[pl.ds(s, n, stride=ROW_SUB), :] for s in range(ROW_SUB)], axis=-1)


def _tile_row(buf, j):
    return buf.at[j // 8, :, j % 8, :]


def _invert_kernel(tiles_ref, dest_ref, inv_ref):
    def clear_tile(t):
        def clear(r, carry):
            inv_ref[t * TMX + r] = -1
            return carry
        lax.fori_loop(0, TMX, clear, 0, unroll=16)

    def clear_tail(t, carry):
        clear_tile(t)
        return carry

    def place(a, carry):
        inv_ref[dest_ref[a]] = a
        return carry

    for e in range(N_EXPERTS):
        clear_tile(tiles_ref[e])
    lax.fori_loop(tiles_ref[N_EXPERTS], SORT_TILES, clear_tail, 0)
    lax.fori_loop(0, 2 * N_LAT, place, 0, unroll=16)


def _invert(clear_tiles, dest_flat):
    return pl.pallas_call(
        _invert_kernel,
        in_specs=[pl.BlockSpec(memory_space=pltpu.SMEM), pl.BlockSpec(memory_space=pltpu.SMEM)],
        out_specs=pl.BlockSpec(memory_space=pltpu.SMEM),
        out_shape=jax.ShapeDtypeStruct((SORT_ROWS,), jnp.int32),
        name="moe_invert",
    )(clear_tiles, dest_flat)


def _expert_kernel(te_ref, nu_ref, inv_ref, h_ref, wg_ref, wu_ref, wd_ref, y2_ref,
                   xbuf, ybuf, sem_g, sem_s):
    del te_ref
    t = pl.program_id(0)
    n_used = nu_ref[0]
    slot = lax.rem(t, 2)
    other = 1 - slot

    def gather_rows(tile, s):
        base = tile * TMX
        for j in range(TMX):
            token = jnp.maximum(inv_ref[base + j], 0) & (N_LAT - 1)
            pltpu.make_async_copy(_row(h_ref, token), _tile_row(xbuf.at[s], j),
                                  sem_g.at[s]).start(priority=j % 2)

    def scatter_rows(tile, s, real, scratch_block):
        base = tile * TMX
        for j in range(TMX):
            code = inv_ref[base + j]
            row = jnp.where(jnp.logical_and(code >= 0, real), code,
                            2 * N_LAT + scratch_block * TMX + j)
            pltpu.make_async_copy(_tile_row(ybuf.at[s], j), _row(y2_ref, row),
                                  sem_s.at[s]).start(priority=j % 2)

    def wait_gather(s):
        pltpu.make_async_copy(xbuf.at[s], xbuf.at[s], sem_g.at[s]).wait()

    def wait_scatter(s):
        pltpu.make_async_copy(ybuf.at[s], ybuf.at[s], sem_s.at[s]).wait()

    @pl.when(t == 0)
    def _():
        ybuf[...] = jnp.zeros_like(ybuf)
        gather_rows(0, 0)

    @pl.when(t < n_used)
    def _():
        wait_gather(slot)
        x = jnp.concatenate([xbuf[slot, :, m].reshape(TMX, LANES) for m in range(ROW_SUB)],
                            axis=-1).astype(BF16)
        gather_rows(jnp.minimum(t + 1, n_used - 1), other)
        scatter_rows(jnp.maximum(t - 1, 0), other, t > 0, other)
        y = _swiglu(x, wg_ref.at[0], wu_ref.at[0], wd_ref.at[0])

        @pl.when(t > 0)
        def _():
            wait_scatter(slot)

        for m in range(ROW_SUB):
            ybuf[slot, :, m] = y[:, m * LANES:(m + 1) * LANES].reshape(TMX // 8, 8, LANES)

    @pl.when(t == n_used)
    def _():
        wait_gather(slot)
        scatter_rows(n_used - 1, other, True, other)
        wait_scatter(other)
        wait_scatter(slot)
        scatter_rows(n_used - 1, slot, False, 0)
        wait_scatter(slot)


def _experts(tile_expert, n_used, inv, hrow, wg, wu, wd):
    one = pl.Buffered(1)
    wspec = lambda shape: pl.BlockSpec(shape, lambda t, te, nu, iv: (te[t], 0, 0), pipeline_mode=one)
    grid_spec = pltpu.PrefetchScalarGridSpec(
        num_scalar_prefetch=3,
        grid=(SORT_TILES + 1,),
        in_specs=[pl.BlockSpec(memory_space=pl.ANY),
                  wspec((1, D_MODEL, EXPERT_FF)), wspec((1, D_MODEL, EXPERT_FF)),
                  wspec((1, EXPERT_FF, D_MODEL))],
        out_specs=pl.BlockSpec(memory_space=pl.ANY),
        scratch_shapes=[pltpu.VMEM((2, TMX // 8, ROW_SUB, 8, LANES), F32),
                        pltpu.VMEM((2, TMX // 8, ROW_SUB, 8, LANES), F32),
                        pltpu.SemaphoreType.DMA((2,)), pltpu.SemaphoreType.DMA((2,))])
    return pl.pallas_call(
        _expert_kernel,
        grid_spec=grid_spec,
        out_shape=jax.ShapeDtypeStruct((Y2_ROWS * ROW_SUB, LANES), F32),
        compiler_params=_params("arbitrary"),
        name="moe_experts",
    )(tile_expert, n_used, inv, hrow, wg, wu, wd)


def _combine_kernel(ya_ref, yb_ref, x1_ref, route_ref, mod_ref, o_ref):
    route = route_ref[...]
    y = (route[:, 0:1] * _rows_to_matrix(ya_ref, TM) + route[:, 1:2] * _rows_to_matrix(yb_ref, TM))
    o_ref[...] = x1_ref[...] + mod_ref[0, 5:6, :] * y


def _combine(y2, x1, route, mod3):
    return pl.pallas_call(
        _combine_kernel,
        grid=(LAT_TILES,),
        in_specs=[pl.BlockSpec((TM * ROW_SUB, LANES), lambda i: (i, 0)),
                  pl.BlockSpec((TM * ROW_SUB, LANES), lambda i: (LAT_TILES + i, 0)),
                  pl.BlockSpec((TM, D_MODEL), lambda i: (i, 0)),
                  pl.BlockSpec((TM, LANES), lambda i: (i, 0)),
                  pl.BlockSpec((1, 6, D_MODEL), lambda i: (i // SEQ_TILES, 0, 0))],
        out_specs=pl.BlockSpec((TM, D_MODEL), lambda i: (i, 0)),
        out_shape=jax.ShapeDtypeStruct((N_LAT, D_MODEL), F32),
        compiler_params=_params("parallel"),
        name="moe_combine",
    )(y2, y2, x1, route, mod3)


def _moe_ffn(hrow, x1, route, route_t, counts, mod3, wg, wu, wd):
    expert = route_t[2:4].astype(jnp.int32)
    rank = route_t[4:6].astype(jnp.int32)
    count = counts[0, :N_EXPERTS].astype(jnp.int32)
    tiles = (count + TMX - 1) // TMX
    tile_end = jnp.cumsum(tiles)
    first_row = (tile_end - tiles) * TMX
    ids = jnp.arange(N_EXPERTS, dtype=jnp.int32)[:, None, None]
    dest = jnp.sum(jnp.where(expert[None] == ids, first_row[:, None, None], 0), axis=0) + rank
    last_tiles = jnp.maximum(tile_end - 1, 0)
    inv = _invert(jnp.concatenate([last_tiles, tile_end[N_EXPERTS - 1:]]), dest.reshape(-1))
    n_used = tile_end[N_EXPERTS - 1:]
    tile_ids = jnp.minimum(jnp.arange(SORT_TILES + 1), n_used[0] - 1)
    tile_expert = jnp.sum(tile_ids[:, None] >= tile_end[None, :], axis=1).astype(jnp.int32)
    y2 = _experts(tile_expert, n_used.astype(jnp.int32), inv, hrow, wg, wu, wd)
    return _combine(y2, x1, route, mod3)


def _swap_rotary_halves(w, start, width, half):
    lead = w.shape[:-1]
    seg = w[..., start:start + width].reshape(lead + (width // (2 * half), 2, half))
    seg = seg[..., ::-1, :].reshape(lead + (width,))
    return jnp.concatenate([w[..., :start], seg, w[..., start + width:]], axis=-1)


_partner_a = functools.partial(_swap_rotary_halves, start=MLA_NOPE, width=MLA_ROPE,
                               half=MLA_ROPE // 4)
_partner_b = functools.partial(_swap_rotary_halves, start=0, width=LANES, half=GQA_HEAD_DIM // 4)


def _rope_tables():
    f32 = np.float32
    t = np.arange(SEQ)
    row = (t // GRID_W).astype(f32)
    col = (t % GRID_W).astype(f32)

    def one_axis(pos, half):
        freqs = f32(ROPE_THETA) ** (-np.arange(half, dtype=f32) / f32(half))
        ang = pos[:, None] * freqs[None, :]
        cos, sin = np.cos(ang), np.sin(ang)
        return np.concatenate([cos, cos], -1), np.concatenate([-sin, sin], -1)

    def two_axes(half):
        r, c = one_axis(row, half), one_axis(col, half)
        return [np.concatenate([a, b], -1) for a, b in zip(r, c)]

    def pad_a(tbl, fill):
        return np.concatenate([np.full((SEQ, MLA_NOPE), fill, f32), tbl,
                               np.full((SEQ, LANES - MLA_QK), fill, f32)], -1)

    ta = two_axes(MLA_ROPE // 4)
    tb = [np.concatenate([x, x], -1) for x in two_axes(GQA_HEAD_DIM // 4)]
    tabs = [pad_a(ta[0], 1.0), pad_a(ta[1], 0.0)] + tb
    ident = [np.ones((TM, LANES), f32), np.zeros((TM, LANES), f32)]
    return np.stack([np.concatenate([tbl, idn], 0)
                     for tbl, idn in zip(tabs, ident + ident)]).astype(f32)


def _layer_weights(w_in, q_lat_norm, w_uq, kv_lat_norm, w_ukv, mla_q_gain, mla_k_gain,
                   gqa_q_gain, gqa_k_gain):
    split = MLA_Q_RANK + MLA_KV_RANK + MLA_ROPE
    w_qb = w_in[:, split:split + 512].reshape(D_MODEL, GQA_HEADS // 2, LANES)
    w_kb = w_in[:, split + 512:split + 640]
    w_in_p = jnp.concatenate([w_in[:, :split], jnp.zeros((D_MODEL, 512 - split), F32),
                              w_in[:, split:], _partner_b(w_qb).reshape(D_MODEL, 512),
                              _partner_b(w_kb)], axis=1).astype(BF16)
    w_uq_p = jnp.pad(w_uq.reshape(MLA_Q_RANK, MLA_HEADS, MLA_QK),
                     ((0, 0), (0, 0), (0, LANES - MLA_QK)))
    w_uq_p = jnp.concatenate([w_uq_p, _partner_a(w_uq_p)], axis=1).reshape(MLA_Q_RANK, -1)
    ukv = w_ukv.reshape(MLA_KV_RANK, MLA_HEADS, MLA_NOPE + MLA_V)
    w_k = jnp.pad(ukv[:, :, :MLA_NOPE], ((0, 0), (0, 0), (0, LANES - MLA_NOPE)))
    place = jnp.pad(jnp.eye(MLA_ROPE, dtype=F32), ((0, 0), (MLA_NOPE, LANES - MLA_QK)))
    place = jnp.broadcast_to(place[:, None, :], (MLA_ROPE, MLA_HEADS, LANES))
    w_k = jnp.concatenate([w_k, place,
                           jnp.zeros((256 - MLA_KV_RANK - MLA_ROPE, MLA_HEADS, LANES), F32)], 0)
    w_v = jnp.pad(ukv[:, :, MLA_NOPE:].reshape(MLA_KV_RANK, MLA_HEADS * MLA_V),
                  ((0, 256 - MLA_KV_RANK), (0, 0)))
    w_kv = jnp.concatenate([w_k.reshape(256, -1), _partner_a(w_k).reshape(256, -1), w_v], axis=1)
    glat = jnp.stack([q_lat_norm, jnp.pad(kv_lat_norm, (0, 256 - MLA_KV_RANK))])
    pad_qk = lambda g: jnp.pad(g, (0, LANES - MLA_QK))
    gains = [pad_qk(mla_q_gain) * (MLA_SCALE * LOG2E), pad_qk(mla_k_gain),
             jnp.tile(gqa_q_gain, 2) * (GQA_SCALE * LOG2E), jnp.tile(gqa_k_gain, 2)]
    partners = [_partner_a, _partner_a, _partner_b, _partner_b]
    g128 = jnp.stack([v for g, partner in zip(gains, partners) for v in (g, partner(g))])
    return w_in_p, glat, w_uq_p.astype(BF16), w_kv.astype(BF16), g128


def _mixer(x_lat, x_ctx, ctx_tile0, mod3, tab, norm_attn, w_in, q_lat_norm, w_uq, kv_lat_norm,
           w_ukv, mla_q_gain, mla_k_gain, gqa_q_gain, gqa_k_gain, with_ctx_queries, cast_a, cast_b):
    w_in_p, glat, w_uq_p, w_kv, g128 = _layer_weights(
        w_in, q_lat_norm, w_uq, kv_lat_norm, w_ukv, mla_q_gain, mla_k_gain, gqa_q_gain, gqa_k_gain)
    qa, ka, va, qb, kb, vb = _projection(x_lat, x_ctx, ctx_tile0, mod3,
                                         norm_attn.reshape(1, D_MODEL), w_in_p, glat,
                                         w_uq_p, w_kv, g128, tab)
    oa, oa_ctx, cast_a = _attention(qa, ka, va, _A_HEADS, _A_KV, with_ctx_queries, cast_a)
    ob, ob_ctx, cast_b = _attention(qb, kb, vb, _B_HEADS, _B_KV, with_ctx_queries, cast_b)
    return (oa, ob), ((oa_ctx, ob_ctx) if with_ctx_queries else None), cast_a + cast_b


def kernel(x, c, ctx, c_ctx, l0_w_mod, l0_b_mod, l0_norm_attn, l0_w_in, l0_q_lat_norm, l0_w_uq, l0_kv_lat_norm, l0_w_ukv, l0_mla_q_gain, l0_mla_k_gain, l0_gqa_q_gain, l0_gqa_k_gain, l0_w_out, l0_norm_ffn, l0_ffn_w_gate, l0_ffn_w_up, l0_ffn_w_down, l1_w_mod, l1_b_mod, l1_norm_attn, l1_w_in, l1_q_lat_norm, l1_w_uq, l1_kv_lat_norm, l1_w_ukv, l1_mla_q_gain, l1_mla_k_gain, l1_gqa_q_gain, l1_gqa_k_gain, l1_w_out, l1_norm_ffn, l1_router, l1_exp_w_gate, l1_exp_w_up, l1_exp_w_down):
    x_lat = x.reshape(N_LAT, D_MODEL)
    x_ctx = ctx.reshape(N_CTX, D_MODEL)
    cc = jnp.concatenate([c, c_ctx[None, :], jnp.zeros((16 - BATCH - 1, D_MODEL), F32)], axis=0)
    tab = jnp.asarray(_rope_tables())

    mod3 = _modulation(cc, l0_w_mod, l0_b_mod)
    o_lat, o_ctx, (wg_x, wg_0, wu_x, wu_0) = _mixer(
        x_lat, x_ctx, 0, mod3, tab, l0_norm_attn, l0_w_in, l0_q_lat_norm, l0_w_uq, l0_kv_lat_norm,
        l0_w_ukv, l0_mla_q_gain, l0_mla_k_gain, l0_gqa_q_gain, l0_gqa_k_gain, True,
        [l1_exp_w_gate.reshape(N_EXPERTS * D_MODEL, EXPERT_FF), l0_ffn_w_gate],
        [l1_exp_w_up.reshape(N_EXPERTS * D_MODEL, EXPERT_FF), l0_ffn_w_up])
    xall = _post_dense_ffn(x_lat, x_ctx, o_lat, o_ctx, mod3, l0_norm_ffn.reshape(1, D_MODEL),
                           l0_w_out.astype(BF16), wg_0, wu_0, l0_ffn_w_down.astype(BF16))

    mod3 = _modulation(cc, l1_w_mod, l1_b_mod)
    o_lat, _, (wd_x,) = _mixer(
        xall, xall, LAT_TILES, mod3, tab, l1_norm_attn, l1_w_in, l1_q_lat_norm, l1_w_uq,
        l1_kv_lat_norm, l1_w_ukv, l1_mla_q_gain, l1_mla_k_gain, l1_gqa_q_gain, l1_gqa_k_gain,
        False, [l1_exp_w_down.reshape(N_EXPERTS * EXPERT_FF, D_MODEL)], [])
    w_router = jnp.pad(l1_router, ((0, 0), (0, LANES - N_EXPERTS)))
    w_router_hi = w_router.astype(BF16)
    w_router = jnp.concatenate([w_router_hi, (w_router - w_router_hi.astype(F32)).astype(BF16)],
                               axis=1)
    x1, hrow, route, route_t, counts = _post_attention_router(xall, o_lat, mod3,
                                                     l1_norm_ffn.reshape(1, D_MODEL),
                                                     l1_w_out.astype(BF16), w_router)
    out = _moe_ffn(hrow, x1, route, route_t, counts, mod3,
                   wg_x.reshape(N_EXPERTS, D_MODEL, EXPERT_FF),
                   wu_x.reshape(N_EXPERTS, D_MODEL, EXPERT_FF),
                   wd_x.reshape(N_EXPERTS, EXPERT_FF, D_MODEL))
    return out.reshape(BATCH, SEQ, D_MODEL)
```

```python
import functools
import math

import jax
import jax.numpy as jnp
import numpy as np
from jax import lax
from jax.experimental import pallas as pl
from jax.experimental.pallas import tpu as pltpu

D_MODEL = 1024
BATCH = 8
SEQ = 2048
CTX_LEN = 256
GRID_W = 64
MLA_HEADS = 8
MLA_NOPE = 64
MLA_ROPE = 32
MLA_V = 64
MLA_QK = MLA_NOPE + MLA_ROPE
MLA_Q_RANK = 256
MLA_KV_RANK = 128
GQA_HEADS = 8
GQA_KV_HEADS = 2
GQA_HEAD_DIM = 64
DENSE_FF = 2816
N_EXPERTS = 8
EXPERT_FF = 2816
ROPE_THETA = 10000.0
NORM_EPS = 1e-6
LOG2E = math.log2(math.e)
MLA_SCALE = MLA_QK ** -0.5
GQA_SCALE = GQA_HEAD_DIM ** -0.5

LANES = 128
N_LAT = BATCH * SEQ
N_CTX = BATCH * CTX_LEN
N_TOK = N_LAT + N_CTX
TM = 512
LAT_TILES = N_LAT // TM
TOK_TILES = N_TOK // TM
SEQ_TILES = SEQ // TM
Q_CHUNK = 256
PROJ_CHUNK = 512
ROUTE_CHUNK = 512
V_WIDTH = 2 * LANES
IN_PAD = 1920
FF_SPLITS = (0, 1536, 2816)
ROW_SUB = D_MODEL // LANES
TMX = 512
SORT_TILES = 2 * N_LAT // TMX + N_EXPERTS
SORT_ROWS = SORT_TILES * TMX
Y2_ROWS = 2 * N_LAT + 2 * TMX
assert N_LAT & (N_LAT - 1) == 0
VMEM_LIMIT = 56 * 1024 * 1024

F32 = jnp.float32
BF16 = jnp.bfloat16


def _silu(x):
    return x / (1.0 + jnp.exp(-x))


def _params(*sem):
    return pltpu.CompilerParams(dimension_semantics=sem, vmem_limit_bytes=VMEM_LIMIT)


def _mod_kernel(c_ref, w_ref, b_ref, o_ref):
    s = _silu(c_ref[...])
    w = w_ref[...]
    s_hi = s.astype(BF16)
    s_lo = (s - s_hi.astype(F32)).astype(BF16)
    w_hi = w.astype(BF16)
    w_lo = (w - w_hi.astype(F32)).astype(BF16)
    dot = functools.partial(jnp.dot, preferred_element_type=F32)
    o_ref[...] = dot(s_hi, w_hi) + dot(s_hi, w_lo) + dot(s_lo, w_hi) + b_ref[...]


def _modulation(cc, w_mod, b_mod):
    n = w_mod.shape[1]
    bn = 1024
    out = pl.pallas_call(
        _mod_kernel,
        grid=(n // bn,),
        in_specs=[pl.BlockSpec((16, D_MODEL), lambda j: (0, 0)),
                  pl.BlockSpec((D_MODEL, bn), lambda j: (0, j)),
                  pl.BlockSpec((1, bn), lambda j: (0, j))],
        out_specs=pl.BlockSpec((16, bn), lambda j: (0, j)),
        out_shape=jax.ShapeDtypeStruct((16, n), F32),
        compiler_params=_params("parallel"),
        name="modulation",
    )(cc, w_mod, b_mod.reshape(1, n))
    return out.reshape(16, 6, D_MODEL)


def _tile_rows(lat_ref, ctx_ref):
    return jnp.where(pl.program_id(0) < LAT_TILES, lat_ref[...], ctx_ref[...])


def _proj_kernel(xl_ref, xc_ref, mod_ref, gn_ref, win_ref, glat_ref, wuq_ref, wkv_ref, g128_ref,
                 tab_ref, qa_ref, ka_ref, va_ref, qb_ref, kb_ref, vb_ref):
    latent = pl.program_id(0) < LAT_TILES
    n = PROJ_CHUNK
    lane = lax.broadcasted_iota(jnp.int32, (n, LANES), 1)
    lo = lane < 64
    ones_col = jnp.where(lane == 0, 1.0, 0.0).astype(BF16)

    for r0 in range(0, TM, n):
        rows = slice(r0, r0 + n)
        x = jnp.where(latent, xl_ref[rows, :], xc_ref[rows, :])
        r = lax.rsqrt(jnp.mean(x * x, axis=-1, keepdims=True) + NORM_EPS)
        h = (x * r * gn_ref[...]) * (1.0 + mod_ref[0, 1:2, :]) + mod_ref[0, 0:1, :]
        p = jnp.dot(h.astype(BF16), win_ref[...], preferred_element_type=F32)

        def tables(base, row):
            return (tab_ref[base, rows, :] * g128_ref[row:row + 1, :],
                    tab_ref[base + 1, rows, :] * g128_ref[row + 1:row + 2, :])

        cq = p[:, 0:256]
        rq = lax.rsqrt(jnp.mean(cq * cq, axis=-1, keepdims=True) + NORM_EPS)
        qa = jnp.dot((cq * rq * glat_ref[0:1, :]).astype(BF16), wuq_ref[...],
                     preferred_element_type=F32)
        ckv = p[:, 256:384]
        rkv = lax.rsqrt(jnp.mean(ckv * ckv, axis=-1, keepdims=True) + NORM_EPS)
        slab = jnp.concatenate([ckv * rkv * glat_ref[1:2, 0:128], p[:, 384:512]], axis=-1)
        kv = jnp.dot(slab.astype(BF16), wkv_ref[...], preferred_element_type=F32)

        def head_a(src, hd, cos_g, sin_g):
            blk = src[:, hd * LANES:(hd + 1) * LANES]
            partner = src[:, 1024 + hd * LANES:1024 + (hd + 1) * LANES]
            rr = lax.rsqrt(jnp.sum(blk * blk, axis=-1, keepdims=True) * (1.0 / MLA_QK) + NORM_EPS)
            return ((blk * cos_g + partner * sin_g) * rr).astype(BF16)

        cq_g, sq_g = tables(0, 0)
        ck_g, sk_g = tables(0, 2)
        for hd in range(MLA_HEADS):
            qa_ref[hd, rows, :] = head_a(qa, hd, cq_g, sq_g)
            ka_ref[hd, rows, :] = head_a(kv, hd, ck_g, sk_g)
        for pr in range(MLA_HEADS // 2):
            blk = kv[:, 2048 + pr * LANES:2048 + (pr + 1) * LANES]
            va_ref[2 * pr, rows, 0:LANES] = jnp.where(lo, blk, 0.0).astype(BF16)
            va_ref[2 * pr + 1, rows, 0:LANES] = jnp.where(lo, 0.0, blk).astype(BF16)
        for hd in range(MLA_HEADS):
            va_ref[hd, rows, LANES:V_WIDTH] = ones_col

        def pair_b(blk, partner, cos_g, sin_g):
            sq = blk * blk
            s_lo = jnp.sum(jnp.where(lo, sq, 0.0), axis=-1, keepdims=True)
            s_hi = jnp.sum(jnp.where(lo, 0.0, sq), axis=-1, keepdims=True)
            rr = jnp.where(lo, lax.rsqrt(s_lo * (1.0 / GQA_HEAD_DIM) + NORM_EPS),
                           lax.rsqrt(s_hi * (1.0 / GQA_HEAD_DIM) + NORM_EPS))
            return (blk * cos_g + partner * sin_g) * rr

        cq_g, sq_g = tables(2, 4)
        ck_g, sk_g = tables(2, 6)
        for pr in range(GQA_HEADS // 2):
            blk = p[:, 512 + pr * LANES:512 + (pr + 1) * LANES]
            partner = p[:, 1280 + pr * LANES:1280 + (pr + 1) * LANES]
            qb_ref[pr, rows, :] = pair_b(blk, partner, cq_g, sq_g).astype(BF16)
        kb = pair_b(p[:, 1024:1152], p[:, 1792:1920], ck_g, sk_g)
        kb_sw = pltpu.roll(kb, 64, 1)
        kb_ref[0, rows, :] = jnp.where(lo, kb, 0.0).astype(BF16)
        kb_ref[1, rows, :] = jnp.where(lo, 0.0, kb_sw).astype(BF16)
        kb_ref[2, rows, :] = jnp.where(lo, kb_sw, 0.0).astype(BF16)
        kb_ref[3, rows, :] = jnp.where(lo, 0.0, kb).astype(BF16)
        vb = p[:, 1152:1280]
        vb_sw = pltpu.roll(vb, 64, 1)
        vb_ref[0, rows, 0:LANES] = jnp.where(lo, vb, 0.0).astype(BF16)
        vb_ref[1, rows, 0:LANES] = jnp.where(lo, 0.0, vb_sw).astype(BF16)
        vb_ref[2, rows, 0:LANES] = jnp.where(lo, vb_sw, 0.0).astype(BF16)
        vb_ref[3, rows, 0:LANES] = jnp.where(lo, 0.0, vb).astype(BF16)
        for j in range(2 * GQA_KV_HEADS):
            vb_ref[j, rows, LANES:V_WIDTH] = ones_col


def _mod_index(i):
    return jnp.where(i < LAT_TILES, i // SEQ_TILES, BATCH)


def _lat_ctx_specs(width, ctx_tile0):
    return [pl.BlockSpec((TM, width), lambda i: (jnp.minimum(i, LAT_TILES - 1), 0)),
            pl.BlockSpec((TM, width), lambda i: (ctx_tile0 + jnp.maximum(i - LAT_TILES, 0), 0))]


def _projection(x_lat, x_ctx, ctx_tile0, mod3, gn, w_in, glat, w_uq, w_kv, g128, tab):
    const2 = lambda i: (0, 0)
    head_out = lambda n, w: pl.BlockSpec((n, TM, w), lambda i: (0, i, 0))
    head_shape = lambda n, w: jax.ShapeDtypeStruct((n, N_TOK, w), BF16)
    return pl.pallas_call(
        _proj_kernel,
        grid=(TOK_TILES,),
        in_specs=_lat_ctx_specs(D_MODEL, ctx_tile0) + [
                  pl.BlockSpec((1, 6, D_MODEL), lambda i: (_mod_index(i), 0, 0)),
                  pl.BlockSpec((1, D_MODEL), const2),
                  pl.BlockSpec((D_MODEL, IN_PAD), const2, pipeline_mode=pl.Buffered(1)),
                  pl.BlockSpec((2, 256), const2),
                  pl.BlockSpec((MLA_Q_RANK, 2 * MLA_HEADS * LANES), const2,
                               pipeline_mode=pl.Buffered(1)),
                  pl.BlockSpec((256, 2 * MLA_HEADS * LANES + MLA_HEADS * MLA_V), const2,
                               pipeline_mode=pl.Buffered(1)),
                  pl.BlockSpec((8, LANES), const2),
                  pl.BlockSpec((4, TM, LANES),
                               lambda i: (0, jnp.where(i < LAT_TILES, i % SEQ_TILES, SEQ_TILES), 0))],
        out_specs=[head_out(8, LANES), head_out(8, LANES), head_out(8, V_WIDTH),
                   head_out(4, LANES), head_out(4, LANES), head_out(4, V_WIDTH)],
        out_shape=[head_shape(8, LANES), head_shape(8, LANES), head_shape(8, V_WIDTH),
                   head_shape(4, LANES), head_shape(4, LANES), head_shape(4, V_WIDTH)],
        compiler_params=_params("parallel"),
        name="projection",
    )(x_lat, x_ctx, mod3, gn, w_in, glat, w_uq, w_kv, g128, tab)


def _pair_rows(q_refs, r0, n, kv_refs):
    nt = (((1,), (1,)), ((), ()))
    out = None
    for u, q_ref in enumerate(q_refs):
        q = q_ref[r0:r0 + n, :]
        scores = [lax.dot_general(q, k_ref[u], nt, preferred_element_type=F32)
                  for k_ref, _ in kv_refs]
        m = functools.reduce(jnp.maximum, [jnp.max(sc, axis=-1, keepdims=True) for sc in scores])
        acc = None
        for sc, (_, v_ref) in zip(scores, kv_refs):
            part = jnp.dot(jnp.exp2((sc - m).astype(BF16)), v_ref[u], preferred_element_type=F32)
            acc = part if acc is None else acc + part
        o = acc[:, 0:LANES] / acc[:, LANES:LANES + 1]
        out = o if out is None else out + o
    return out


def _attn_kernel(*refs, ctx_queries, n_casts):
    q0_ref, q1_ref, kl_ref, kc_ref, vl_ref, vc_ref = refs[:6]
    rest = list(refs[6:])
    qc_refs = (rest.pop(0), rest.pop(0)) if ctx_queries else None
    src_refs = [rest.pop(0) for _ in range(n_casts)]
    o_ref = rest.pop(0)
    oc_ref = rest.pop(0) if ctx_queries else None
    for src_ref in src_refs:
        rest.pop(0)[...] = src_ref[...].astype(BF16)
    for r0 in range(0, SEQ, Q_CHUNK):
        out = _pair_rows((q0_ref, q1_ref), r0, Q_CHUNK, ((kc_ref, vc_ref), (kl_ref, vl_ref)))
        o_ref[r0:r0 + Q_CHUNK, :] = out.astype(o_ref.dtype)
    if ctx_queries:
        oc_ref[...] = _pair_rows(qc_refs, 0, CTX_LEN, ((kc_ref, vc_ref),)).astype(oc_ref.dtype)


def _attention(q, k, v, q_heads, kv_pair, ctx_queries, cast_srcs):
    n_units = 4
    ctx_blk0 = N_LAT // CTX_LEN

    def qspec(which, rows, blk0):
        return pl.BlockSpec((None, rows, LANES), lambda b, p: (q_heads(p)[which], blk0 + b, 0))

    def kvspec(rows, blk0, width):
        return pl.BlockSpec((2, rows, width), lambda b, p: (kv_pair(p), blk0 + b, 0))

    in_specs = [qspec(0, SEQ, 0), qspec(1, SEQ, 0), kvspec(SEQ, 0, LANES),
                kvspec(CTX_LEN, ctx_blk0, LANES), kvspec(SEQ, 0, V_WIDTH),
                kvspec(CTX_LEN, ctx_blk0, V_WIDTH)]
    args = [q, q, k, k, v, v]
    out_specs = [pl.BlockSpec((SEQ, LANES), lambda b, p: (b, p))]
    out_shape = [jax.ShapeDtypeStruct((N_LAT, n_units * LANES), BF16)]
    if ctx_queries:
        in_specs += [qspec(0, CTX_LEN, ctx_blk0), qspec(1, CTX_LEN, ctx_blk0)]
        args += [q, q]
        out_specs.append(pl.BlockSpec((CTX_LEN, LANES), lambda b, p: (b, p)))
        out_shape.append(jax.ShapeDtypeStruct((N_CTX, n_units * LANES), BF16))
    for src in cast_srcs:
        rows, cols = src.shape
        slab = pl.BlockSpec((rows // (BATCH * n_units), cols), lambda b, p: (b * n_units + p, 0))
        in_specs.append(slab)
        args.append(src)
        out_specs.append(slab)
        out_shape.append(jax.ShapeDtypeStruct((rows, cols), BF16))
    outs = list(pl.pallas_call(
        functools.partial(_attn_kernel, ctx_queries=ctx_queries, n_casts=len(cast_srcs)),
        grid=(BATCH, n_units),
        in_specs=in_specs, out_specs=out_specs, out_shape=out_shape,
        compiler_params=_params("parallel", "parallel"),
        name="attention",
    )(*args))
    o_lat = outs.pop(0)
    o_ctx = outs.pop(0) if ctx_queries else None
    return o_lat, o_ctx, outs


_A_HEADS = lambda p: (2 * p, 2 * p + 1)
_A_KV = lambda p: p
_B_HEADS = lambda p: (p, p)
_B_KV = lambda p: p // 2


def _post_body(x, oa, ob, mod_ref, gn_ref, wo_ref):
    y = (jnp.dot(oa, wo_ref[0:512, :], preferred_element_type=F32)
         + jnp.dot(ob, wo_ref[512:1024, :], preferred_element_type=F32))
    x1 = x + mod_ref[0, 2:3, :] * y
    r = lax.rsqrt(jnp.mean(x1 * x1, axis=-1, keepdims=True) + NORM_EPS)
    h2 = (x1 * r * gn_ref[...]) * (1.0 + mod_ref[0, 4:5, :]) + mod_ref[0, 3:4, :]
    return x1, h2


def _swiglu(h, wg_ref, wu_ref, wd_ref):
    y = None
    for lo, hi in zip(FF_SPLITS[:-1], FF_SPLITS[1:]):
        g = jnp.dot(h, wg_ref[:, lo:hi], preferred_element_type=F32)
        u = jnp.dot(h, wu_ref[:, lo:hi], preferred_element_type=F32)
        yc = jnp.dot((_silu(g) * u).astype(BF16), wd_ref[lo:hi, :], preferred_element_type=F32)
        y = yc if y is None else y + yc
    return y


def _post_ffn_kernel(xl_ref, xc_ref, oa_ref, oac_ref, ob_ref, obc_ref, mod_ref, gn_ref, wo_ref,
                     wg_ref, wu_ref, wd_ref, o_ref):
    x1, h2 = _post_body(_tile_rows(xl_ref, xc_ref), _tile_rows(oa_ref, oac_ref),
                        _tile_rows(ob_ref, obc_ref), mod_ref, gn_ref, wo_ref)
    y = _swiglu(h2.astype(BF16), wg_ref, wu_ref, wd_ref)
    o_ref[...] = x1 + mod_ref[0, 5:6, :] * y


def _post_router_kernel(x_ref, oa_ref, ob_ref, mod_ref, gn_ref, wo_ref, wr_ref,
                        x1_ref, hrow_ref, route_ref, route_t_ref, count_ref, carry_ref):
    @pl.when(pl.program_id(0) == 0)
    def _():
        carry_ref[...] = jnp.zeros_like(carry_ref)

    n = ROUTE_CHUNK
    lane = lax.broadcasted_iota(jnp.int32, (n, LANES), 1)
    valid = lane < N_EXPERTS
    row = lax.broadcasted_iota(jnp.int32, (n, n), 0)
    col = lax.broadcasted_iota(jnp.int32, (n, n), 1)
    before = jnp.where(col < row, 1.0, 0.0).astype(BF16)
    carry = carry_ref[0:1, :]

    for r0 in range(0, TM, n):
        rows = slice(r0, r0 + n)
        x1, h2 = _post_body(x_ref[rows, :], oa_ref[rows, :], ob_ref[rows, :], mod_ref, gn_ref,
                            wo_ref)
        x1_ref[rows, :] = x1
        for s in range(ROW_SUB):
            hrow_ref[pl.ds(r0 * ROW_SUB + s, n, stride=ROW_SUB), :] = h2[:, s * LANES:(s + 1) * LANES]

        h_hi = h2.astype(BF16)
        h_lo = (h2 - h_hi.astype(F32)).astype(BF16)
        part = jnp.dot(h_hi, wr_ref[...], preferred_element_type=F32)
        logits = (part[:, 0:LANES] + part[:, LANES:2 * LANES]
                  + jnp.dot(h_lo, wr_ref[:, 0:LANES], preferred_element_type=F32))
        lg = jnp.where(valid, logits, -jnp.inf)
        e = jnp.exp(lg - jnp.max(lg, axis=-1, keepdims=True))
        probs = e / jnp.sum(e, axis=-1, keepdims=True)
        v1 = jnp.max(probs, axis=-1, keepdims=True)
        i1 = jnp.min(jnp.where(probs == v1, lane, LANES), axis=-1, keepdims=True)
        rest = jnp.where(valid & (lane != i1), probs, -1.0)
        v2 = jnp.max(rest, axis=-1, keepdims=True)
        i2 = jnp.min(jnp.where(rest == v2, lane, LANES), axis=-1, keepdims=True)

        pick1 = lane == i1
        pick2 = lane == i2
        onehot = jnp.where(pick1, 1.0, 0.0) + jnp.where(pick2, 1.0, 0.0)
        seen = jnp.dot(before, onehot.astype(BF16), preferred_element_type=F32) + carry
        rank1 = jnp.sum(jnp.where(pick1, seen, 0.0), axis=-1, keepdims=True)
        rank2 = jnp.sum(jnp.where(pick2, seen, 0.0), axis=-1, keepdims=True)
        carry = carry + jnp.sum(onehot, axis=0, keepdims=True)

        fields = (v1 / (v1 + v2), v2 / (v1 + v2), i1.astype(F32), i2.astype(F32), rank1, rank2)
        route = jnp.zeros((n, LANES), F32)
        for k, val in enumerate(fields):
            route = jnp.where(lane == k, val, route)
        route_ref[rows, :] = route
        route_t_ref[:, rows] = jnp.transpose(route)[0:8, :]

    carry_ref[...] = jnp.broadcast_to(carry, carry_ref.shape)
    count_ref[...] = jnp.broadcast_to(carry, count_ref.shape)


def _post_specs():
    const2 = lambda i: (0, 0)
    return [pl.BlockSpec((1, 6, D_MODEL), lambda i: (_mod_index(i), 0, 0)),
            pl.BlockSpec((1, D_MODEL), const2),
            pl.BlockSpec((D_MODEL, D_MODEL), const2, pipeline_mode=pl.Buffered(1))]


def _post_dense_ffn(x_lat, x_ctx, o_lat, o_ctx, mod3, gn, w_out, wg, wu, wd):
    resident = lambda shape: pl.BlockSpec(shape, lambda i: (0, 0), pipeline_mode=pl.Buffered(1))
    return pl.pallas_call(
        _post_ffn_kernel, grid=(TOK_TILES,),
        in_specs=(_lat_ctx_specs(D_MODEL, 0) + _lat_ctx_specs(512, 0) + _lat_ctx_specs(512, 0)
                  + _post_specs() + [resident((D_MODEL, DENSE_FF)), resident((D_MODEL, DENSE_FF)),
                                     resident((DENSE_FF, D_MODEL))]),
        out_specs=pl.BlockSpec((TM, D_MODEL), lambda i: (i, 0)),
        out_shape=jax.ShapeDtypeStruct((N_TOK, D_MODEL), F32),
        compiler_params=_params("parallel"), name="post_attention_ffn",
    )(x_lat, x_ctx, o_lat[0], o_ctx[0], o_lat[1], o_ctx[1], mod3, gn, w_out, wg, wu, wd)


def _post_attention_router(x_lat, o_lat, mod3, gn, w_out, w_router):
    const2 = lambda i: (0, 0)
    n_tiles = LAT_TILES
    rows = n_tiles * TM
    x1_spec = pl.BlockSpec((TM, D_MODEL), lambda i: (i, 0))
    x1_shape = jax.ShapeDtypeStruct((rows, D_MODEL), F32)
    half_spec = pl.BlockSpec((TM, 512), lambda i: (i, 0))
    in_specs = [x1_spec, half_spec, half_spec] + _post_specs()
    args = [x_lat, *o_lat, mod3, gn, w_out]
    return pl.pallas_call(
        _post_router_kernel, grid=(n_tiles,),
        in_specs=in_specs + [pl.BlockSpec((D_MODEL, 2 * LANES), const2)],
        out_specs=[x1_spec,
                   pl.BlockSpec((TM * ROW_SUB, LANES), lambda i: (i, 0)),
                   pl.BlockSpec((TM, LANES), lambda i: (i, 0)),
                   pl.BlockSpec((8, TM), lambda i: (0, i)),
                   pl.BlockSpec((8, LANES), const2)],
        out_shape=[x1_shape,
                   jax.ShapeDtypeStruct((rows * ROW_SUB, LANES), F32),
                   jax.ShapeDtypeStruct((rows, LANES), F32),
                   jax.ShapeDtypeStruct((8, rows), F32),
                   jax.ShapeDtypeStruct((8, LANES), F32)],
        scratch_shapes=[pltpu.VMEM((8, LANES), F32)],
        compiler_params=_params("arbitrary"), name="post_attention_router",
    )(*args, w_router)


def _row(ref, r):
    return ref.at[pl.ds(pl.multiple_of(r * ROW_SUB, ROW_SUB), ROW_SUB)]


def _rows_to_matrix(ref, n):
    return jnp.concatenate([ref[pl.ds(s, n, stride=ROW_SUB), :] for s in range(ROW_SUB)], axis=-1)


def _tile_row(buf, j):
    return buf.at[j // 8, :, j % 8, :]


def _invert_kernel(tiles_ref, dest_ref, inv_ref):
    def clear_tile(t):
        def clear(r, carry):
            inv_ref[t * TMX + r] = -1
            return carry
        lax.fori_loop(0, TMX, clear, 0, unroll=16)

    def clear_tail(t, carry):
        clear_tile(t)
        return carry

    def place(a, carry):
        inv_ref[dest_ref[a]] = a
        return carry

    for e in range(N_EXPERTS):
        clear_tile(tiles_ref[e])
    lax.fori_loop(tiles_ref[N_EXPERTS], SORT_TILES, clear_tail, 0)
    lax.fori_loop(0, 2 * N_LAT, place, 0, unroll=16)


def _invert(clear_tiles, dest_flat):
    return pl.pallas_call(
        _invert_kernel,
        in_specs=[pl.BlockSpec(memory_space=pltpu.SMEM), pl.BlockSpec(memory_space=pltpu.SMEM)],
        out_specs=pl.BlockSpec(memory_space=pltpu.SMEM),
        out_shape=jax.ShapeDtypeStruct((SORT_ROWS,), jnp.int32),
        name="moe_invert",
    )(clear_tiles, dest_flat)


def _expert_kernel(te_ref, nu_ref, inv_ref, h_ref, wg_ref, wu_ref, wd_ref, y2_ref,
                   xbuf, ybuf, sem_g, sem_s):
    del te_ref
    t = pl.program_id(0)
    n_used = nu_ref[0]
    slot = lax.rem(t, 2)
    other = 1 - slot

    def gather_rows(tile, s):
        base = tile * TMX
        for j in range(TMX):
            token = jnp.maximum(inv_ref[base + j], 0) & (N_LAT - 1)
            pltpu.make_async_copy(_row(h_ref, token), _tile_row(xbuf.at[s], j),
                                  sem_g.at[s]).start(priority=j % 2)

    def scatter_rows(tile, s, real, scratch_block):
        base = tile * TMX
        for j in range(TMX):
            code = inv_ref[base + j]
            row = jnp.where(jnp.logical_and(code >= 0, real), code,
                            2 * N_LAT + scratch_block * TMX + j)
            pltpu.make_async_copy(_tile_row(ybuf.at[s], j), _row(y2_ref, row),
                                  sem_s.at[s]).start(priority=j % 2)

    def wait_gather(s):
        pltpu.make_async_copy(xbuf.at[s], xbuf.at[s], sem_g.at[s]).wait()

    def wait_scatter(s):
        pltpu.make_async_copy(ybuf.at[s], ybuf.at[s], sem_s.at[s]).wait()

    @pl.when(t == 0)
    def _():
        ybuf[...] = jnp.zeros_like(ybuf)
        gather_rows(0, 0)

    @pl.when(t < n_used)
    def _():
        wait_gather(slot)
        x = jnp.concatenate([xbuf[slot, :, m].reshape(TMX, LANES) for m in range(ROW_SUB)],
                            axis=-1).astype(BF16)
        gather_rows(jnp.minimum(t + 1, n_used - 1), other)
        scatter_rows(jnp.maximum(t - 1, 0), other, t > 0, other)
        y = _swiglu(x, wg_ref.at[0], wu_ref.at[0], wd_ref.at[0])

        @pl.when(t > 0)
        def _():
            wait_scatter(slot)

        for m in range(ROW_SUB):
            ybuf[slot, :, m] = y[:, m * LANES:(m + 1) * LANES].reshape(TMX // 8, 8, LANES)

    @pl.when(t == n_used)
    def _():
        wait_gather(slot)
        scatter_rows(n_used - 1, other, True, other)
        wait_scatter(other)
        wait_scatter(slot)
        scatter_rows(n_used - 1, slot, False, 0)
        wait_scatter(slot)


def _experts(tile_expert, n_used, inv, hrow, wg, wu, wd):
    one = pl.Buffered(1)
    wspec = lambda shape: pl.BlockSpec(shape, lambda t, te, nu, iv: (te[t], 0, 0), pipeline_mode=one)
    grid_spec = pltpu.PrefetchScalarGridSpec(
        num_scalar_prefetch=3,
        grid=(SORT_TILES + 1,),
        in_specs=[pl.BlockSpec(memory_space=pl.ANY),
                  wspec((1, D_MODEL, EXPERT_FF)), wspec((1, D_MODEL, EXPERT_FF)),
                  wspec((1, EXPERT_FF, D_MODEL))],
        out_specs=pl.BlockSpec(memory_space=pl.ANY),
        scratch_shapes=[pltpu.VMEM((2, TMX // 8, ROW_SUB, 8, LANES), F32),
                        pltpu.VMEM((2, TMX // 8, ROW_SUB, 8, LANES), F32),
                        pltpu.SemaphoreType.DMA((2,)), pltpu.SemaphoreType.DMA((2,))])
    return pl.pallas_call(
        _expert_kernel,
        grid_spec=grid_spec,
        out_shape=jax.ShapeDtypeStruct((Y2_ROWS * ROW_SUB, LANES), F32),
        compiler_params=_params("arbitrary"),
        name="moe_experts",
    )(tile_expert, n_used, inv, hrow, wg, wu, wd)


def _combine_kernel(ya_ref, yb_ref, x1_ref, route_ref, mod_ref, o_ref):
    route = route_ref[...]
    y = (route[:, 0:1] * _rows_to_matrix(ya_ref, TM) + route[:, 1:2] * _rows_to_matrix(yb_ref, TM))
    o_ref[...] = x1_ref[...] + mod_ref[0, 5:6, :] * y


def _combine(y2, x1, route, mod3):
    return pl.pallas_call(
        _combine_kernel,
        grid=(LAT_TILES,),
        in_specs=[pl.BlockSpec((TM * ROW_SUB, LANES), lambda i: (i, 0)),
                  pl.BlockSpec((TM * ROW_SUB, LANES), lambda i: (LAT_TILES + i, 0)),
                  pl.BlockSpec((TM, D_MODEL), lambda i: (i, 0)),
                  pl.BlockSpec((TM, LANES), lambda i: (i, 0)),
                  pl.BlockSpec((1, 6, D_MODEL), lambda i: (i // SEQ_TILES, 0, 0))],
        out_specs=pl.BlockSpec((TM, D_MODEL), lambda i: (i, 0)),
        out_shape=jax.ShapeDtypeStruct((N_LAT, D_MODEL), F32),
        compiler_params=_params("parallel"),
        name="moe_combine",
    )(y2, y2, x1, route, mod3)


def _moe_ffn(hrow, x1, route, route_t, counts, mod3, wg, wu, wd):
    expert = route_t[2:4].astype(jnp.int32)
    rank = route_t[4:6].astype(jnp.int32)
    count = counts[0, :N_EXPERTS].astype(jnp.int32)
    tiles = (count + TMX - 1) // TMX
    tile_end = jnp.cumsum(tiles)
    first_row = (tile_end - tiles) * TMX
    ids = jnp.arange(N_EXPERTS, dtype=jnp.int32)[:, None, None]
    dest = jnp.sum(jnp.where(expert[None] == ids, first_row[:, None, None], 0), axis=0) + rank
    last_tiles = jnp.maximum(tile_end - 1, 0)
    inv = _invert(jnp.concatenate([last_tiles, tile_end[N_EXPERTS - 1:]]), dest.reshape(-1))
    n_used = tile_end[N_EXPERTS - 1:]
    tile_ids = jnp.minimum(jnp.arange(SORT_TILES + 1), n_used[0] - 1)
    tile_expert = jnp.sum(tile_ids[:, None] >= tile_end[None, :], axis=1).astype(jnp.int32)
    y2 = _experts(tile_expert, n_used.astype(jnp.int32), inv, hrow, wg, wu, wd)
    return _combine(y2, x1, route, mod3)


def _swap_rotary_halves(w, start, width, half):
    lead = w.shape[:-1]
    seg = w[..., start:start + width].reshape(lead + (width // (2 * half), 2, half))
    seg = seg[..., ::-1, :].reshape(lead + (width,))
    return jnp.concatenate([w[..., :start], seg, w[..., start + width:]], axis=-1)


_partner_a = functools.partial(_swap_rotary_halves, start=MLA_NOPE, width=MLA_ROPE,
                               half=MLA_ROPE // 4)
_partner_b = functools.partial(_swap_rotary_halves, start=0, width=LANES, half=GQA_HEAD_DIM // 4)


def _rope_tables():
    f32 = np.float32
    t = np.arange(SEQ)
    row = (t // GRID_W).astype(f32)
    col = (t % GRID_W).astype(f32)

    def one_axis(pos, half):
        freqs = f32(ROPE_THETA) ** (-np.arange(half, dtype=f32) / f32(half))
        ang = pos[:, None] * freqs[None, :]
        cos, sin = np.cos(ang), np.sin(ang)
        return np.concatenate([cos, cos], -1), np.concatenate([-sin, sin], -1)

    def two_axes(half):
        r, c = one_axis(row, half), one_axis(col, half)
        return [np.concatenate([a, b], -1) for a, b in zip(r, c)]

    def pad_a(tbl, fill):
        return np.concatenate([np.full((SEQ, MLA_NOPE), fill, f32), tbl,
                               np.full((SEQ, LANES - MLA_QK), fill, f32)], -1)

    ta = two_axes(MLA_ROPE // 4)
    tb = [np.concatenate([x, x], -1) for x in two_axes(GQA_HEAD_DIM // 4)]
    tabs = [pad_a(ta[0], 1.0), pad_a(ta[1], 0.0)] + tb
    ident = [np.ones((TM, LANES), f32), np.zeros((TM, LANES), f32)]
    return np.stack([np.concatenate([tbl, idn], 0)
                     for tbl, idn in zip(tabs, ident + ident)]).astype(f32)


def _layer_weights(w_in, q_lat_norm, w_uq, kv_lat_norm, w_ukv, mla_q_gain, mla_k_gain,
                   gqa_q_gain, gqa_k_gain):
    split = MLA_Q_RANK + MLA_KV_RANK + MLA_ROPE
    w_qb = w_in[:, split:split + 512].reshape(D_MODEL, GQA_HEADS // 2, LANES)
    w_kb = w_in[:, split + 512:split + 640]
    w_in_p = jnp.concatenate([w_in[:, :split], jnp.zeros((D_MODEL, 512 - split), F32),
                              w_in[:, split:], _partner_b(w_qb).reshape(D_MODEL, 512),
                              _partner_b(w_kb)], axis=1).astype(BF16)
    w_uq_p = jnp.pad(w_uq.reshape(MLA_Q_RANK, MLA_HEADS, MLA_QK),
                     ((0, 0), (0, 0), (0, LANES - MLA_QK)))
    w_uq_p = jnp.concatenate([w_uq_p, _partner_a(w_uq_p)], axis=1).reshape(MLA_Q_RANK, -1)
    ukv = w_ukv.reshape(MLA_KV_RANK, MLA_HEADS, MLA_NOPE + MLA_V)
    w_k = jnp.pad(ukv[:, :, :MLA_NOPE], ((0, 0), (0, 0), (0, LANES - MLA_NOPE)))
    place = jnp.pad(jnp.eye(MLA_ROPE, dtype=F32), ((0, 0), (MLA_NOPE, LANES - MLA_QK)))
    place = jnp.broadcast_to(place[:, None, :], (MLA_ROPE, MLA_HEADS, LANES))
    w_k = jnp.concatenate([w_k, place,
                           jnp.zeros((256 - MLA_KV_RANK - MLA_ROPE, MLA_HEADS, LANES), F32)], 0)
    w_v = jnp.pad(ukv[:, :, MLA_NOPE:].reshape(MLA_KV_RANK, MLA_HEADS * MLA_V),
                  ((0, 256 - MLA_KV_RANK), (0, 0)))
    w_kv = jnp.concatenate([w_k.reshape(256, -1), _partner_a(w_k).reshape(256, -1), w_v], axis=1)
    glat = jnp.stack([q_lat_norm, jnp.pad(kv_lat_norm, (0, 256 - MLA_KV_RANK))])
    pad_qk = lambda g: jnp.pad(g, (0, LANES - MLA_QK))
    gains = [pad_qk(mla_q_gain) * (MLA_SCALE * LOG2E), pad_qk(mla_k_gain),
             jnp.tile(gqa_q_gain, 2) * (GQA_SCALE * LOG2E), jnp.tile(gqa_k_gain, 2)]
    partners = [_partner_a, _partner_a, _partner_b, _partner_b]
    g128 = jnp.stack([v for g, partner in zip(gains, partners) for v in (g, partner(g))])
    return w_in_p, glat, w_uq_p.astype(BF16), w_kv.astype(BF16), g128


def _mixer(x_lat, x_ctx, ctx_tile0, mod3, tab, norm_attn, w_in, q_lat_norm, w_uq, kv_lat_norm,
           w_ukv, mla_q_gain, mla_k_gain, gqa_q_gain, gqa_k_gain, with_ctx_queries, cast_a, cast_b):
    w_in_p, glat, w_uq_p, w_kv, g128 = _layer_weights(
        w_in, q_lat_norm, w_uq, kv_lat_norm, w_ukv, mla_q_gain, mla_k_gain, gqa_q_gain, gqa_k_gain)
    qa, ka, va, qb, kb, vb = _projection(x_lat, x_ctx, ctx_tile0, mod3,
                                         norm_attn.reshape(1, D_MODEL), w_in_p, glat,
                                         w_uq_p, w_kv, g128, tab)
    oa, oa_ctx, cast_a = _attention(qa, ka, va, _A_HEADS, _A_KV, with_ctx_queries, cast_a)
    ob, ob_ctx, cast_b = _attention(qb, kb, vb, _B_HEADS, _B_KV, with_ctx_queries, cast_b)
    return (oa, ob), ((oa_ctx, ob_ctx) if with_ctx_queries else None), cast_a + cast_b


def kernel(x, c, ctx, c_ctx, l0_w_mod, l0_b_mod, l0_norm_attn, l0_w_in, l0_q_lat_norm, l0_w_uq, l0_kv_lat_norm, l0_w_ukv, l0_mla_q_gain, l0_mla_k_gain, l0_gqa_q_gain, l0_gqa_k_gain, l0_w_out, l0_norm_ffn, l0_ffn_w_gate, l0_ffn_w_up, l0_ffn_w_down, l1_w_mod, l1_b_mod, l1_norm_attn, l1_w_in, l1_q_lat_norm, l1_w_uq, l1_kv_lat_norm, l1_w_ukv, l1_mla_q_gain, l1_mla_k_gain, l1_gqa_q_gain, l1_gqa_k_gain, l1_w_out, l1_norm_ffn, l1_router, l1_exp_w_gate, l1_exp_w_up, l1_exp_w_down):
    x_lat = x.reshape(N_LAT, D_MODEL)
    x_ctx = ctx.reshape(N_CTX, D_MODEL)
    cc = jnp.concatenate([c, c_ctx[None, :], jnp.zeros((16 - BATCH - 1, D_MODEL), F32)], axis=0)
    tab = jnp.asarray(_rope_tables())

    mod3 = _modulation(cc, l0_w_mod, l0_b_mod)
    o_lat, o_ctx, (wg_x, wg_0, wu_x, wu_0, wd_0) = _mixer(
        x_lat, x_ctx, 0, mod3, tab, l0_norm_attn, l0_w_in, l0_q_lat_norm, l0_w_uq, l0_kv_lat_norm,
        l0_w_ukv, l0_mla_q_gain, l0_mla_k_gain, l0_gqa_q_gain, l0_gqa_k_gain, True,
        [l1_exp_w_gate.reshape(N_EXPERTS * D_MODEL, EXPERT_FF), l0_ffn_w_gate],
        [l1_exp_w_up.reshape(N_EXPERTS * D_MODEL, EXPERT_FF), l0_ffn_w_up, l0_ffn_w_down])
    xall = _post_dense_ffn(x_lat, x_ctx, o_lat, o_ctx, mod3, l0_norm_ffn.reshape(1, D_MODEL),
                           l0_w_out.astype(BF16), wg_0, wu_0, wd_0)

    mod3 = _modulation(cc, l1_w_mod, l1_b_mod)
    o_lat, _, (wd_x,) = _mixer(
        xall, xall, LAT_TILES, mod3, tab, l1_norm_attn, l1_w_in, l1_q_lat_norm, l1_w_uq,
        l1_kv_lat_norm, l1_w_ukv, l1_mla_q_gain, l1_mla_k_gain, l1_gqa_q_gain, l1_gqa_k_gain,
        False, [l1_exp_w_down.reshape(N_EXPERTS * EXPERT_FF, D_MODEL)], [])
    w_router = jnp.pad(l1_router, ((0, 0), (0, LANES - N_EXPERTS)))
    w_router_hi = w_router.astype(BF16)
    w_router = jnp.concatenate([w_router_hi, (w_router - w_router_hi.astype(F32)).astype(BF16)],
                               axis=1)
    x1, hrow, route, route_t, counts = _post_attention_router(xall, o_lat, mod3,
                                                     l1_norm_ffn.reshape(1, D_MODEL),
                                                     l1_w_out.astype(BF16), w_router)
    out = _moe_ffn(hrow, x1, route, route_t, counts, mod3,
                   wg_x.reshape(N_EXPERTS, D_MODEL, EXPERT_FF),
                   wu_x.reshape(N_EXPERTS, D_MODEL, EXPERT_FF),
                   wd_x.reshape(N_EXPERTS, EXPERT_FF, D_MODEL))
    return out.reshape(BATCH, SEQ, D_MODEL)
```

```python
import functools
import math

import jax
import jax.numpy as jnp
import numpy as np
from jax import lax
from jax.experimental import pallas as pl
from jax.experimental.pallas import tpu as pltpu

D_MODEL = 1024
BATCH = 8
SEQ = 2048
CTX_LEN = 256
GRID_W = 64
MLA_HEADS = 8
MLA_NOPE = 64
MLA_ROPE = 32
MLA_V = 64
MLA_QK = MLA_NOPE + MLA_ROPE
MLA_Q_RANK = 256
MLA_KV_RANK = 128
GQA_HEADS = 8
GQA_KV_HEADS = 2
GQA_HEAD_DIM = 64
DENSE_FF = 2816
N_EXPERTS = 8
EXPERT_FF = 2816
ROPE_THETA = 10000.0
NORM_EPS = 1e-6
LOG2E = math.log2(math.e)
MLA_SCALE = MLA_QK ** -0.5
GQA_SCALE = GQA_HEAD_DIM ** -0.5

LANES = 128
N_LAT = BATCH * SEQ
N_CTX = BATCH * CTX_LEN
N_TOK = N_LAT + N_CTX
TM = 512
LAT_TILES = N_LAT // TM
TOK_TILES = N_TOK // TM
SEQ_TILES = SEQ // TM
Q_CHUNK = 256
PROJ_CHUNK = 512
ROUTE_CHUNK = 512
V_WIDTH = 2 * LANES
IN_PAD = 1920
FF_SPLITS = (0, 1536, 2816)
ROW_SUB = D_MODEL // LANES
TMX = 512
SORT_TILES = 2 * N_LAT // TMX + N_EXPERTS
SORT_ROWS = SORT_TILES * TMX
Y2_ROWS = 2 * N_LAT + 2 * TMX
assert N_LAT & (N_LAT - 1) == 0
VMEM_LIMIT = 56 * 1024 * 1024

F32 = jnp.float32
BF16 = jnp.bfloat16


def _silu(x):
    return x / (1.0 + jnp.exp(-x))


def _params(*sem):
    return pltpu.CompilerParams(dimension_semantics=sem, vmem_limit_bytes=VMEM_LIMIT)


def _mod_kernel(c_ref, w_ref, b_ref, o_ref):
    s = _silu(c_ref[...])
    w = w_ref[...]
    s_hi = s.astype(BF16)
    s_lo = (s - s_hi.astype(F32)).astype(BF16)
    w_hi = w.astype(BF16)
    w_lo = (w - w_hi.astype(F32)).astype(BF16)
    dot = functools.partial(jnp.dot, preferred_element_type=F32)
    o_ref[...] = dot(s_hi, w_hi) + dot(s_hi, w_lo) + dot(s_lo, w_hi) + b_ref[...]


def _modulation(cc, w_mod, b_mod):
    n = w_mod.shape[1]
    bn = 1024
    out = pl.pallas_call(
        _mod_kernel,
        grid=(n // bn,),
        in_specs=[pl.BlockSpec((16, D_MODEL), lambda j: (0, 0)),
                  pl.BlockSpec((D_MODEL, bn), lambda j: (0, j)),
                  pl.BlockSpec((1, bn), lambda j: (0, j))],
        out_specs=pl.BlockSpec((16, bn), lambda j: (0, j)),
        out_shape=jax.ShapeDtypeStruct((16, n), F32),
        compiler_params=_params("parallel"),
        name="modulation",
    )(cc, w_mod, b_mod.reshape(1, n))
    return out.reshape(16, 6, D_MODEL)


def _tile_rows(lat_ref, ctx_ref):
    return jnp.where(pl.program_id(0) < LAT_TILES, lat_ref[...], ctx_ref[...])


def _proj_kernel(xl_ref, xc_ref, mod_ref, gn_ref, win_ref, glat_ref, wuq_ref, wkv_ref, g128_ref,
                 tab_ref, qa_ref, ka_ref, va_ref, qb_ref, kb_ref, vb_ref):
    latent = pl.program_id(0) < LAT_TILES
    n = PROJ_CHUNK
    lane = lax.broadcasted_iota(jnp.int32, (n, LANES), 1)
    lo = lane < 64
    ones_col = jnp.where(lane == 0, 1.0, 0.0).astype(BF16)

    for r0 in range(0, TM, n):
        rows = slice(r0, r0 + n)
        x = jnp.where(latent, xl_ref[rows, :], xc_ref[rows, :])
        r = lax.rsqrt(jnp.mean(x * x, axis=-1, keepdims=True) + NORM_EPS)
        h = (x * r * gn_ref[...]) * (1.0 + mod_ref[0, 1:2, :]) + mod_ref[0, 0:1, :]
        p = jnp.dot(h.astype(BF16), win_ref[...], preferred_element_type=F32)

        def tables(base, row):
            return (tab_ref[base, rows, :] * g128_ref[row:row + 1, :],
                    tab_ref[base + 1, rows, :] * g128_ref[row + 1:row + 2, :])

        cq = p[:, 0:256]
        rq = lax.rsqrt(jnp.mean(cq * cq, axis=-1, keepdims=True) + NORM_EPS)
        qa = jnp.dot((cq * rq * glat_ref[0:1, :]).astype(BF16), wuq_ref[...],
                     preferred_element_type=F32)
        ckv = p[:, 256:384]
        rkv = lax.rsqrt(jnp.mean(ckv * ckv, axis=-1, keepdims=True) + NORM_EPS)
        slab = jnp.concatenate([ckv * rkv * glat_ref[1:2, 0:128], p[:, 384:512]], axis=-1)
        kv = jnp.dot(slab.astype(BF16), wkv_ref[...], preferred_element_type=F32)

        def head_a(src, hd, cos_g, sin_g):
            blk = src[:, hd * LANES:(hd + 1) * LANES]
            partner = src[:, 1024 + hd * LANES:1024 + (hd + 1) * LANES]
            rr = lax.rsqrt(jnp.sum(blk * blk, axis=-1, keepdims=True) * (1.0 / MLA_QK) + NORM_EPS)
            return ((blk * cos_g + partner * sin_g) * rr).astype(BF16)

        cq_g, sq_g = tables(0, 0)
        ck_g, sk_g = tables(0, 2)
        for hd in range(MLA_HEADS):
            qa_ref[hd, rows, :] = head_a(qa, hd, cq_g, sq_g)
            ka_ref[hd, rows, :] = head_a(kv, hd, ck_g, sk_g)
        for pr in range(MLA_HEADS // 2):
            blk = kv[:, 2048 + pr * LANES:2048 + (pr + 1) * LANES]
            va_ref[2 * pr, rows, 0:LANES] = jnp.where(lo, blk, 0.0).astype(BF16)
            va_ref[2 * pr + 1, rows, 0:LANES] = jnp.where(lo, 0.0, blk).astype(BF16)
        for hd in range(MLA_HEADS):
            va_ref[hd, rows, LANES:V_WIDTH] = ones_col

        def pair_b(blk, partner, cos_g, sin_g):
            sq = blk * blk
            s_lo = jnp.sum(jnp.where(lo, sq, 0.0), axis=-1, keepdims=True)
            s_hi = jnp.sum(jnp.where(lo, 0.0, sq), axis=-1, keepdims=True)
            rr = jnp.where(lo, lax.rsqrt(s_lo * (1.0 / GQA_HEAD_DIM) + NORM_EPS),
                           lax.rsqrt(s_hi * (1.0 / GQA_HEAD_DIM) + NORM_EPS))
            return (blk * cos_g + partner * sin_g) * rr

        cq_g, sq_g = tables(2, 4)
        ck_g, sk_g = tables(2, 6)
        for pr in range(GQA_HEADS // 2):
            blk = p[:, 512 + pr * LANES:512 + (pr + 1) * LANES]
            partner = p[:, 1280 + pr * LANES:1280 + (pr + 1) * LANES]
            qb_ref[pr, rows, :] = pair_b(blk, partner, cq_g, sq_g).astype(BF16)
        kb = pair_b(p[:, 1024:1152], p[:, 1792:1920], ck_g, sk_g)
        kb_sw = pltpu.roll(kb, 64, 1)
        kb_ref[0, rows, :] = jnp.where(lo, kb, 0.0).astype(BF16)
        kb_ref[1, rows, :] = jnp.where(lo, 0.0, kb_sw).astype(BF16)
        kb_ref[2, rows, :] = jnp.where(lo, kb_sw, 0.0).astype(BF16)
        kb_ref[3, rows, :] = jnp.where(lo, 0.0, kb).astype(BF16)
        vb = p[:, 1152:1280]
        vb_sw = pltpu.roll(vb, 64, 1)
        vb_ref[0, rows, 0:LANES] = jnp.where(lo, vb, 0.0).astype(BF16)
        vb_ref[1, rows, 0:LANES] = jnp.where(lo, 0.0, vb_sw).astype(BF16)
        vb_ref[2, rows, 0:LANES] = jnp.where(lo, vb_sw, 0.0).astype(BF16)
        vb_ref[3, rows, 0:LANES] = jnp.where(lo, 0.0, vb).astype(BF16)
        for j in range(2 * GQA_KV_HEADS):
            vb_ref[j, rows, LANES:V_WIDTH] = ones_col


def _mod_index(i):
    return jnp.where(i < LAT_TILES, i // SEQ_TILES, BATCH)


def _lat_ctx_specs(width, ctx_tile0):
    return [pl.BlockSpec((TM, width), lambda i: (jnp.minimum(i, LAT_TILES - 1), 0)),
            pl.BlockSpec((TM, width), lambda i: (ctx_tile0 + jnp.maximum(i - LAT_TILES, 0), 0))]


def _projection(x_lat, x_ctx, ctx_tile0, mod3, gn, w_in, glat, w_uq, w_kv, g128, tab):
    const2 = lambda i: (0, 0)
    head_out = lambda n, w: pl.BlockSpec((n, TM, w), lambda i: (0, i, 0))
    head_shape = lambda n, w: jax.ShapeDtypeStruct((n, N_TOK, w), BF16)
    return pl.pallas_call(
        _proj_kernel,
        grid=(TOK_TILES,),
        in_specs=_lat_ctx_specs(D_MODEL, ctx_tile0) + [
                  pl.BlockSpec((1, 6, D_MODEL), lambda i: (_mod_index(i), 0, 0)),
                  pl.BlockSpec((1, D_MODEL), const2),
                  pl.BlockSpec((D_MODEL, IN_PAD), const2, pipeline_mode=pl.Buffered(1)),
                  pl.BlockSpec((2, 256), const2),
                  pl.BlockSpec((MLA_Q_RANK, 2 * MLA_HEADS * LANES), const2,
                               pipeline_mode=pl.Buffered(1)),
                  pl.BlockSpec((256, 2 * MLA_HEADS * LANES + MLA_HEADS * MLA_V), const2,
                               pipeline_mode=pl.Buffered(1)),
                  pl.BlockSpec((8, LANES), const2),
                  pl.BlockSpec((4, TM, LANES),
                               lambda i: (0, jnp.where(i < LAT_TILES, i % SEQ_TILES, SEQ_TILES), 0))],
        out_specs=[head_out(8, LANES), head_out(8, LANES), head_out(8, V_WIDTH),
                   head_out(4, LANES), head_out(4, LANES), head_out(4, V_WIDTH)],
        out_shape=[head_shape(8, LANES), head_shape(8, LANES), head_shape(8, V_WIDTH),
                   head_shape(4, LANES), head_shape(4, LANES), head_shape(4, V_WIDTH)],
        compiler_params=_params("parallel"),
        name="projection",
    )(x_lat, x_ctx, mod3, gn, w_in, glat, w_uq, w_kv, g128, tab)


def _pair_rows(q_refs, r0, n, kv_refs):
    nt = (((1,), (1,)), ((), ()))
    out = None
    for u, q_ref in enumerate(q_refs):
        q = q_ref[r0:r0 + n, :]
        scores = [lax.dot_general(q, k_ref[u], nt, preferred_element_type=F32)
                  for k_ref, _ in kv_refs]
        m = functools.reduce(jnp.maximum, [jnp.max(sc, axis=-1, keepdims=True) for sc in scores])
        acc = None
        for sc, (_, v_ref) in zip(scores, kv_refs):
            part = jnp.dot(jnp.exp2(sc - m).astype(BF16), v_ref[u], preferred_element_type=F32)
            acc = part if acc is None else acc + part
        o = acc[:, 0:LANES] / acc[:, LANES:LANES + 1]
        out = o if out is None else out + o
    return out


def _attn_kernel(*refs, ctx_queries, n_casts):
    q0_ref, q1_ref, kl_ref, kc_ref, vl_ref, vc_ref = refs[:6]
    rest = list(refs[6:])
    qc_refs = (rest.pop(0), rest.pop(0)) if ctx_queries else None
    src_refs = [rest.pop(0) for _ in range(n_casts)]
    o_ref = rest.pop(0)
    oc_ref = rest.pop(0) if ctx_queries else None
    for src_ref in src_refs:
        rest.pop(0)[...] = src_ref[...].astype(BF16)
    for r0 in range(0, SEQ, Q_CHUNK):
        out = _pair_rows((q0_ref, q1_ref), r0, Q_CHUNK, ((kc_ref, vc_ref), (kl_ref, vl_ref)))
        o_ref[r0:r0 + Q_CHUNK, :] = out.astype(o_ref.dtype)
    if ctx_queries:
        oc_ref[...] = _pair_rows(qc_refs, 0, CTX_LEN, ((kc_ref, vc_ref),)).astype(oc_ref.dtype)


def _attention(q, k, v, q_heads, kv_pair, ctx_queries, cast_srcs):
    n_units = 4
    ctx_blk0 = N_LAT // CTX_LEN

    def qspec(which, rows, blk0):
        return pl.BlockSpec((None, rows, LANES), lambda b, p: (q_heads(p)[which], blk0 + b, 0))

    def kvspec(rows, blk0, width):
        return pl.BlockSpec((2, rows, width), lambda b, p: (kv_pair(p), blk0 + b, 0))

    in_specs = [qspec(0, SEQ, 0), qspec(1, SEQ, 0), kvspec(SEQ, 0, LANES),
                kvspec(CTX_LEN, ctx_blk0, LANES), kvspec(SEQ, 0, V_WIDTH),
                kvspec(CTX_LEN, ctx_blk0, V_WIDTH)]
    args = [q, q, k, k, v, v]
    out_specs = [pl.BlockSpec((SEQ, LANES), lambda b, p: (b, p))]
    out_shape = [jax.ShapeDtypeStruct((N_LAT, n_units * LANES), BF16)]
    if ctx_queries:
        in_specs += [qspec(0, CTX_LEN, ctx_blk0), qspec(1, CTX_LEN, ctx_blk0)]
        args += [q, q]
        out_specs.append(pl.BlockSpec((CTX_LEN, LANES), lambda b, p: (b, p)))
        out_shape.append(jax.ShapeDtypeStruct((N_CTX, n_units * LANES), BF16))
    for src in cast_srcs:
        rows, cols = src.shape
        slab = pl.BlockSpec((rows // (BATCH * n_units), cols), lambda b, p: (b * n_units + p, 0))
        in_specs.append(slab)
        args.append(src)
        out_specs.append(slab)
        out_shape.append(jax.ShapeDtypeStruct((rows, cols), BF16))
    outs = list(pl.pallas_call(
        functools.partial(_attn_kernel, ctx_queries=ctx_queries, n_casts=len(cast_srcs)),
        grid=(BATCH, n_units),
        in_specs=in_specs, out_specs=out_specs, out_shape=out_shape,
        compiler_params=_params("parallel", "parallel"),
        name="attention",
    )(*args))
    o_lat = outs.pop(0)
    o_ctx = outs.pop(0) if ctx_queries else None
    return o_lat, o_ctx, outs


_A_HEADS = lambda p: (2 * p, 2 * p + 1)
_A_KV = lambda p: p
_B_HEADS = lambda p: (p, p)
_B_KV = lambda p: p // 2


def _post_body(x, oa, ob, mod_ref, gn_ref, wo_ref):
    y = (jnp.dot(oa, wo_ref[0:512, :], preferred_element_type=F32)
         + jnp.dot(ob, wo_ref[512:1024, :], preferred_element_type=F32))
    x1 = x + mod_ref[0, 2:3, :] * y
    r = lax.rsqrt(jnp.mean(x1 * x1, axis=-1, keepdims=True) + NORM_EPS)
    h2 = (x1 * r * gn_ref[...]) * (1.0 + mod_ref[0, 4:5, :]) + mod_ref[0, 3:4, :]
    return x1, h2


def _swiglu(h, wg_ref, wu_ref, wd_ref):
    y = None
    for lo, hi in zip(FF_SPLITS[:-1], FF_SPLITS[1:]):
        g = jnp.dot(h, wg_ref[:, lo:hi], preferred_element_type=F32)
        u = jnp.dot(h, wu_ref[:, lo:hi], preferred_element_type=F32)
        yc = jnp.dot((_silu(g) * u).astype(BF16), wd_ref[lo:hi, :], preferred_element_type=F32)
        y = yc if y is None else y + yc
    return y


def _post_ffn_kernel(xl_ref, xc_ref, oa_ref, oac_ref, ob_ref, obc_ref, mod_ref, gn_ref, wo_ref,
                     wg_ref, wu_ref, wd_ref, o_ref):
    x1, h2 = _post_body(_tile_rows(xl_ref, xc_ref), _tile_rows(oa_ref, oac_ref),
                        _tile_rows(ob_ref, obc_ref), mod_ref, gn_ref, wo_ref)
    y = _swiglu(h2.astype(BF16), wg_ref, wu_ref, wd_ref)
    o_ref[...] = x1 + mod_ref[0, 5:6, :] * y


def _post_router_kernel(x_ref, oa_ref, ob_ref, mod_ref, gn_ref, wo_ref, wr_ref,
                        x1_ref, hrow_ref, route_ref, route_t_ref, count_ref, carry_ref):
    @pl.when(pl.program_id(0) == 0)
    def _():
        carry_ref[...] = jnp.zeros_like(carry_ref)

    n = ROUTE_CHUNK
    lane = lax.broadcasted_iota(jnp.int32, (n, LANES), 1)
    valid = lane < N_EXPERTS
    row = lax.broadcasted_iota(jnp.int32, (n, n), 0)
    col = lax.broadcasted_iota(jnp.int32, (n, n), 1)
    before = jnp.where(col < row, 1.0, 0.0).astype(BF16)
    carry = carry_ref[0:1, :]

    for r0 in range(0, TM, n):
        rows = slice(r0, r0 + n)
        x1, h2 = _post_body(x_ref[rows, :], oa_ref[rows, :], ob_ref[rows, :], mod_ref, gn_ref,
                            wo_ref)
        x1_ref[rows, :] = x1
        for s in range(ROW_SUB):
            hrow_ref[pl.ds(r0 * ROW_SUB + s, n, stride=ROW_SUB), :] = h2[:, s * LANES:(s + 1) * LANES]

        h_hi = h2.astype(BF16)
        h_lo = (h2 - h_hi.astype(F32)).astype(BF16)
        part = jnp.dot(h_hi, wr_ref[...], preferred_element_type=F32)
        logits = (part[:, 0:LANES] + part[:, LANES:2 * LANES]
                  + jnp.dot(h_lo, wr_ref[:, 0:LANES], preferred_element_type=F32))
        lg = jnp.where(valid, logits, -jnp.inf)
        e = jnp.exp(lg - jnp.max(lg, axis=-1, keepdims=True))
        probs = e / jnp.sum(e, axis=-1, keepdims=True)
        v1 = jnp.max(probs, axis=-1, keepdims=True)
        i1 = jnp.min(jnp.where(probs == v1, lane, LANES), axis=-1, keepdims=True)
        rest = jnp.where(valid & (lane != i1), probs, -1.0)
        v2 = jnp.max(rest, axis=-1, keepdims=True)
        i2 = jnp.min(jnp.where(rest == v2, lane, LANES), axis=-1, keepdims=True)

        pick1 = lane == i1
        pick2 = lane == i2
        onehot = jnp.where(pick1, 1.0, 0.0) + jnp.where(pick2, 1.0, 0.0)
        seen = jnp.dot(before, onehot.astype(BF16), preferred_element_type=F32) + carry
        rank1 = jnp.sum(jnp.where(pick1, seen, 0.0), axis=-1, keepdims=True)
        rank2 = jnp.sum(jnp.where(pick2, seen, 0.0), axis=-1, keepdims=True)
        carry = carry + jnp.sum(onehot, axis=0, keepdims=True)

        fields = (v1 / (v1 + v2), v2 / (v1 + v2), i1.astype(F32), i2.astype(F32), rank1, rank2)
        route = jnp.zeros((n, LANES), F32)
        for k, val in enumerate(fields):
            route = jnp.where(lane == k, val, route)
        route_ref[rows, :] = route
        route_t_ref[:, rows] = jnp.transpose(route)[0:8, :]

    carry_ref[...] = jnp.broadcast_to(carry, carry_ref.shape)
    count_ref[...] = jnp.broadcast_to(carry, count_ref.shape)


def _post_specs():
    const2 = lambda i: (0, 0)
    return [pl.BlockSpec((1, 6, D_MODEL), lambda i: (_mod_index(i), 0, 0)),
            pl.BlockSpec((1, D_MODEL), const2),
            pl.BlockSpec((D_MODEL, D_MODEL), const2, pipeline_mode=pl.Buffered(1))]


def _post_dense_ffn(x_lat, x_ctx, o_lat, o_ctx, mod3, gn, w_out, wg, wu, wd):
    resident = lambda shape: pl.BlockSpec(shape, lambda i: (0, 0), pipeline_mode=pl.Buffered(1))
    return pl.pallas_call(
        _post_ffn_kernel, grid=(TOK_TILES,),
        in_specs=(_lat_ctx_specs(D_MODEL, 0) + _lat_ctx_specs(512, 0) + _lat_ctx_specs(512, 0)
                  + _post_specs() + [resident((D_MODEL, DENSE_FF)), resident((D_MODEL, DENSE_FF)),
                                     resident((DENSE_FF, D_MODEL))]),
        out_specs=pl.BlockSpec((TM, D_MODEL), lambda i: (i, 0)),
        out_shape=jax.ShapeDtypeStruct((N_TOK, D_MODEL), F32),
        compiler_params=_params("parallel"), name="post_attention_ffn",
    )(x_lat, x_ctx, o_lat[0], o_ctx[0], o_lat[1], o_ctx[1], mod3, gn, w_out, wg, wu, wd)


def _post_attention_router(x_lat, o_lat, mod3, gn, w_out, w_router):
    const2 = lambda i: (0, 0)
    n_tiles = LAT_TILES
    rows = n_tiles * TM
    x1_spec = pl.BlockSpec((TM, D_MODEL), lambda i: (i, 0))
    x1_shape = jax.ShapeDtypeStruct((rows, D_MODEL), F32)
    half_spec = pl.BlockSpec((TM, 512), lambda i: (i, 0))
    in_specs = [x1_spec, half_spec, half_spec] + _post_specs()
    args = [x_lat, *o_lat, mod3, gn, w_out]
    return pl.pallas_call(
        _post_router_kernel, grid=(n_tiles,),
        in_specs=in_specs + [pl.BlockSpec((D_MODEL, 2 * LANES), const2)],
        out_specs=[x1_spec,
                   pl.BlockSpec((TM * ROW_SUB, LANES), lambda i: (i, 0)),
                   pl.BlockSpec((TM, LANES), lambda i: (i, 0)),
                   pl.BlockSpec((8, TM), lambda i: (0, i)),
                   pl.BlockSpec((8, LANES), const2)],
        out_shape=[x1_shape,
                   jax.ShapeDtypeStruct((rows * ROW_SUB, LANES), F32),
                   jax.ShapeDtypeStruct((rows, LANES), F32),
                   jax.ShapeDtypeStruct((8, rows), F32),
                   jax.ShapeDtypeStruct((8, LANES), F32)],
        scratch_shapes=[pltpu.VMEM((8, LANES), F32)],
        compiler_params=_params("arbitrary"), name="post_attention_router",
    )(*args, w_router)


def _row(ref, r):
    return ref.at[pl.ds(pl.multiple_of(r * ROW_SUB, ROW_SUB), ROW_SUB)]


def _rows_to_matrix(ref, n):
    return jnp.concatenate([ref[pl.ds(s, n, stride=ROW_SUB), :] for s in range(ROW_SUB)], axis=-1)


def _tile_row(buf, j):
    return buf.at[j // 8, :, j % 8, :]


def _invert_kernel(tiles_ref, dest_ref, inv_ref):
    def clear_tile(t):
        def clear(r, carry):
            inv_ref[t * TMX + r] = -1
            return carry
        lax.fori_loop(0, TMX, clear, 0, unroll=16)

    def clear_tail(t, carry):
        clear_tile(t)
        return carry

    def place(a, carry):
        inv_ref[dest_ref[a]] = a
        return carry

    for e in range(N_EXPERTS):
        clear_tile(tiles_ref[e])
    lax.fori_loop(tiles_ref[N_EXPERTS], SORT_TILES, clear_tail, 0)
    lax.fori_loop(0, 2 * N_LAT, place, 0, unroll=16)


def _invert(clear_tiles, dest_flat):
    return pl.pallas_call(
        _invert_kernel,
        in_specs=[pl.BlockSpec(memory_space=pltpu.SMEM), pl.BlockSpec(memory_space=pltpu.SMEM)],
        out_specs=pl.BlockSpec(memory_space=pltpu.SMEM),
        out_shape=jax.ShapeDtypeStruct((SORT_ROWS,), jnp.int32),
        name="moe_invert",
    )(clear_tiles, dest_flat)


def _expert_kernel(te_ref, nu_ref, inv_ref, h_ref, wg_ref, wu_ref, wd_ref, y2_ref,
                   xbuf, ybuf, sem_g, sem_s):
    del te_ref
    t = pl.program_id(0)
    n_used = nu_ref[0]
    slot = lax.rem(t, 2)
    other = 1 - slot

    def gather_rows(tile, s):
        base = tile * TMX
        for j in range(TMX):
            token = jnp.maximum(inv_ref[base + j], 0) & (N_LAT - 1)
            pltpu.make_async_copy(_row(h_ref, token), _tile_row(xbuf.at[s], j),
                                  sem_g.at[s]).start(priority=j % 2)

    def scatter_rows(tile, s, real, scratch_block):
        base = tile * TMX
        for j in range(TMX):
            code = inv_ref[base + j]
            row = jnp.where(jnp.logical_and(code >= 0, real), code,
                            2 * N_LAT + scratch_block * TMX + j)
            pltpu.make_async_copy(_tile_row(ybuf.at[s], j), _row(y2_ref, row),
                                  sem_s.at[s]).start(priority=j % 2)

    def wait_gather(s):
        pltpu.make_async_copy(xbuf.at[s], xbuf.at[s], sem_g.at[s]).wait()

    def wait_scatter(s):
        pltpu.make_async_copy(ybuf.at[s], ybuf.at[s], sem_s.at[s]).wait()

    @pl.when(t == 0)
    def _():
        ybuf[...] = jnp.zeros_like(ybuf)
        gather_rows(0, 0)

    @pl.when(t < n_used)
    def _():
        wait_gather(slot)
        x = jnp.concatenate([xbuf[slot, :, m].reshape(TMX, LANES) for m in range(ROW_SUB)],
                            axis=-1).astype(BF16)
        gather_rows(jnp.minimum(t + 1, n_used - 1), other)
        scatter_rows(jnp.maximum(t - 1, 0), other, t > 0, other)
        y = _swiglu(x, wg_ref.at[0], wu_ref.at[0], wd_ref.at[0])

        @pl.when(t > 0)
        def _():
            wait_scatter(slot)

        for m in range(ROW_SUB):
            ybuf[slot, :, m] = y[:, m * LANES:(m + 1) * LANES].reshape(TMX // 8, 8, LANES)

    @pl.when(t == n_used)
    def _():
        wait_gather(slot)
        scatter_rows(n_used - 1, other, True, other)
        wait_scatter(other)
        wait_scatter(slot)
        scatter_rows(n_used - 1, slot, False, 0)
        wait_scatter(slot)


def _experts(tile_expert, n_used, inv, hrow, wg, wu, wd):
    one = pl.Buffered(1)
    wspec = lambda shape: pl.BlockSpec(shape, lambda t, te, nu, iv: (te[t], 0, 0), pipeline_mode=one)
    grid_spec = pltpu.PrefetchScalarGridSpec(
        num_scalar_prefetch=3,
        grid=(SORT_TILES + 1,),
        in_specs=[pl.BlockSpec(memory_space=pl.ANY),
                  wspec((1, D_MODEL, EXPERT_FF)), wspec((1, D_MODEL, EXPERT_FF)),
                  wspec((1, EXPERT_FF, D_MODEL))],
        out_specs=pl.BlockSpec(memory_space=pl.ANY),
        scratch_shapes=[pltpu.VMEM((2, TMX // 8, ROW_SUB, 8, LANES), F32),
                        pltpu.VMEM((2, TMX // 8, ROW_SUB, 8, LANES), F32),
                        pltpu.SemaphoreType.DMA((2,)), pltpu.SemaphoreType.DMA((2,))])
    return pl.pallas_call(
        _expert_kernel,
        grid_spec=grid_spec,
        out_shape=jax.ShapeDtypeStruct((Y2_ROWS * ROW_SUB, LANES), F32),
        compiler_params=_params("arbitrary"),
        name="moe_experts",
    )(tile_expert, n_used, inv, hrow, wg, wu, wd)


def _combine_kernel(ya_ref, yb_ref, x1_ref, route_ref, mod_ref, o_ref):
    route = route_ref[...]
    y = (route[:, 0:1] * _rows_to_matrix(ya_ref, TM) + route[:, 1:2] * _rows_to_matrix(yb_ref, TM))
    o_ref[...] = x1_ref[...] + mod_ref[0, 5:6, :] * y


def _combine(y2, x1, route, mod3):
    return pl.pallas_call(
        _combine_kernel,
        grid=(LAT_TILES,),
        in_specs=[pl.BlockSpec((TM * ROW_SUB, LANES), lambda i: (i, 0)),
                  pl.BlockSpec((TM * ROW_SUB, LANES), lambda i: (LAT_TILES + i, 0)),
                  pl.BlockSpec((TM, D_MODEL), lambda i: (i, 0)),
                  pl.BlockSpec((TM, LANES), lambda i: (i, 0)),
                  pl.BlockSpec((1, 6, D_MODEL), lambda i: (i // SEQ_TILES, 0, 0))],
        out_specs=pl.BlockSpec((TM, D_MODEL), lambda i: (i, 0)),
        out_shape=jax.ShapeDtypeStruct((N_LAT, D_MODEL), F32),
        compiler_params=_params("parallel"),
        name="moe_combine",
    )(y2, y2, x1, route, mod3)


def _moe_ffn(hrow, x1, route, route_t, counts, mod3, wg, wu, wd):
    expert = route_t[2:4].astype(jnp.int32)
    rank = route_t[4:6].astype(jnp.int32)
    count = counts[0, :N_EXPERTS].astype(jnp.int32)
    tiles = (count + TMX - 1) // TMX
    tile_end = jnp.cumsum(tiles)
    first_row = (tile_end - tiles) * TMX
    ids = jnp.arange(N_EXPERTS, dtype=jnp.int32)[:, None, None]
    dest = jnp.sum(jnp.where(expert[None] == ids, first_row[:, None, None], 0), axis=0) + rank
    last_tiles = jnp.maximum(tile_end - 1, 0)
    inv = _invert(jnp.concatenate([last_tiles, tile_end[N_EXPERTS - 1:]]), dest.reshape(-1))
    n_used = tile_end[N_EXPERTS - 1:]
    tile_ids = jnp.minimum(jnp.arange(SORT_TILES + 1), n_used[0] - 1)
    tile_expert = jnp.sum(tile_ids[:, None] >= tile_end[None, :], axis=1).astype(jnp.int32)
    y2 = _experts(tile_expert, n_used.astype(jnp.int32), inv, hrow, wg, wu, wd)
    return _combine(y2, x1, route, mod3)


def _swap_rotary_halves(w, start, width, half):
    lead = w.shape[:-1]
    seg = w[..., start:start + width].reshape(lead + (width // (2 * half), 2, half))
    seg = seg[..., ::-1, :].reshape(lead + (width,))
    return jnp.concatenate([w[..., :start], seg, w[..., start + width:]], axis=-1)


_partner_a = functools.partial(_swap_rotary_halves, start=MLA_NOPE, width=MLA_ROPE,
                               half=MLA_ROPE // 4)
_partner_b = functools.partial(_swap_rotary_halves, start=0, width=LANES, half=GQA_HEAD_DIM // 4)


def _rope_tables():
    f32 = np.float32
    t = np.arange(SEQ)
    row = (t // GRID_W).astype(f32)
    col = (t % GRID_W).astype(f32)

    def one_axis(pos, half):
        freqs = f32(ROPE_THETA) ** (-np.arange(half, dtype=f32) / f32(half))
        ang = pos[:, None] * freqs[None, :]
        cos, sin = np.cos(ang), np.sin(ang)
        return np.concatenate([cos, cos], -1), np.concatenate([-sin, sin], -1)

    def two_axes(half):
        r, c = one_axis(row, half), one_axis(col, half)
        return [np.concatenate([a, b], -1) for a, b in zip(r, c)]

    def pad_a(tbl, fill):
        return np.concatenate([np.full((SEQ, MLA_NOPE), fill, f32), tbl,
                               np.full((SEQ, LANES - MLA_QK), fill, f32)], -1)

    ta = two_axes(MLA_ROPE // 4)
    tb = [np.concatenate([x, x], -1) for x in two_axes(GQA_HEAD_DIM // 4)]
    tabs = [pad_a(ta[0], 1.0), pad_a(ta[1], 0.0)] + tb
    ident = [np.ones((TM, LANES), f32), np.zeros((TM, LANES), f32)]
    return np.stack([np.concatenate([tbl, idn], 0)
                     for tbl, idn in zip(tabs, ident + ident)]).astype(f32)


def _layer_weights(w_in, q_lat_norm, w_uq, kv_lat_norm, w_ukv, mla_q_gain, mla_k_gain,
                   gqa_q_gain, gqa_k_gain):
    split = MLA_Q_RANK + MLA_KV_RANK + MLA_ROPE
    w_qb = w_in[:, split:split + 512].reshape(D_MODEL, GQA_HEADS // 2, LANES)
    w_kb = w_in[:, split + 512:split + 640]
    w_in_p = jnp.concatenate([w_in[:, :split], jnp.zeros((D_MODEL, 512 - split), F32),
                              w_in[:, split:], _partner_b(w_qb).reshape(D_MODEL, 512),
                              _partner_b(w_kb)], axis=1).astype(BF16)
    w_uq_p = jnp.pad(w_uq.reshape(MLA_Q_RANK, MLA_HEADS, MLA_QK),
                     ((0, 0), (0, 0), (0, LANES - MLA_QK)))
    w_uq_p = jnp.concatenate([w_uq_p, _partner_a(w_uq_p)], axis=1).reshape(MLA_Q_RANK, -1)
    ukv = w_ukv.reshape(MLA_KV_RANK, MLA_HEADS, MLA_NOPE + MLA_V)
    w_k = jnp.pad(ukv[:, :, :MLA_NOPE], ((0, 0), (0, 0), (0, LANES - MLA_NOPE)))
    place = jnp.pad(jnp.eye(MLA_ROPE, dtype=F32), ((0, 0), (MLA_NOPE, LANES - MLA_QK)))
    place = jnp.broadcast_to(place[:, None, :], (MLA_ROPE, MLA_HEADS, LANES))
    w_k = jnp.concatenate([w_k, place,
                           jnp.zeros((256 - MLA_KV_RANK - MLA_ROPE, MLA_HEADS, LANES), F32)], 0)
    w_v = jnp.pad(ukv[:, :, MLA_NOPE:].reshape(MLA_KV_RANK, MLA_HEADS * MLA_V),
                  ((0, 256 - MLA_KV_RANK), (0, 0)))
    w_kv = jnp.concatenate([w_k.reshape(256, -1), _partner_a(w_k).reshape(256, -1), w_v], axis=1)
    glat = jnp.stack([q_lat_norm, jnp.pad(kv_lat_norm, (0, 256 - MLA_KV_RANK))])
    pad_qk = lambda g: jnp.pad(g, (0, LANES - MLA_QK))
    gains = [pad_qk(mla_q_gain) * (MLA_SCALE * LOG2E), pad_qk(mla_k_gain),
             jnp.tile(gqa_q_gain, 2) * (GQA_SCALE * LOG2E), jnp.tile(gqa_k_gain, 2)]
    partners = [_partner_a, _partner_a, _partner_b, _partner_b]
    g128 = jnp.stack([v for g, partner in zip(gains, partners) for v in (g, partner(g))])
    return w_in_p, glat, w_uq_p.astype(BF16), w_kv.astype(BF16), g128


def _mixer(x_lat, x_ctx, ctx_tile0, mod3, tab, norm_attn, w_in, q_lat_norm, w_uq, kv_lat_norm,
           w_ukv, mla_q_gain, mla_k_gain, gqa_q_gain, gqa_k_gain, with_ctx_queries, cast_a, cast_b):
    w_in_p, glat, w_uq_p, w_kv, g128 = _layer_weights(
        w_in, q_lat_norm, w_uq, kv_lat_norm, w_ukv, mla_q_gain, mla_k_gain, gqa_q_gain, gqa_k_gain)
    qa, ka, va, qb, kb, vb = _projection(x_lat, x_ctx, ctx_tile0, mod3,
                                         norm_attn.reshape(1, D_MODEL), w_in_p, glat,
                                         w_uq_p, w_kv, g128, tab)
    oa, oa_ctx, cast_a = _attention(qa, ka, va, _A_HEADS, _A_KV, with_ctx_queries, cast_a)
    ob, ob_ctx, cast_b = _attention(qb, kb, vb, _B_HEADS, _B_KV, with_ctx_queries, cast_b)
    return (oa, ob), ((oa_ctx, ob_ctx) if with_ctx_queries else None), cast_a + cast_b


def kernel(x, c, ctx, c_ctx, l0_w_mod, l0_b_mod, l0_norm_attn, l0_w_in, l0_q_lat_norm, l0_w_uq, l0_kv_lat_norm, l0_w_ukv, l0_mla_q_gain, l0_mla_k_gain, l0_gqa_q_gain, l0_gqa_k_gain, l0_w_out, l0_norm_ffn, l0_ffn_w_gate, l0_ffn_w_up, l0_ffn_w_down, l1_w_mod, l1_b_mod, l1_norm_attn, l1_w_in, l1_q_lat_norm, l1_w_uq, l1_kv_lat_norm, l1_w_ukv, l1_mla_q_gain, l1_mla_k_gain, l1_gqa_q_gain, l1_gqa_k_gain, l1_w_out, l1_norm_ffn, l1_router, l1_exp_w_gate, l1_exp_w_up, l1_exp_w_down):
    x_lat = x.reshape(N_LAT, D_MODEL)
    x_ctx = ctx.reshape(N_CTX, D_MODEL)
    cc = jnp.concatenate([c, c_ctx[None, :], jnp.zeros((16 - BATCH - 1, D_MODEL), F32)], axis=0)
    tab = jnp.asarray(_rope_tables())

    mod3 = _modulation(cc, l0_w_mod, l0_b_mod)
    o_lat, o_ctx, (wg_x, wg_0, wu_x, wu_0, wd_0) = _mixer(
        x_lat, x_ctx, 0, mod3, tab, l0_norm_attn, l0_w_in, l0_q_lat_norm, l0_w_uq, l0_kv_lat_norm,
        l0_w_ukv, l0_mla_q_gain, l0_mla_k_gain, l0_gqa_q_gain, l0_gqa_k_gain, True,
        [l1_exp_w_gate.reshape(N_EXPERTS * D_MODEL, EXPERT_FF), l0_ffn_w_gate],
        [l1_exp_w_up.reshape(N_EXPERTS * D_MODEL, EXPERT_FF), l0_ffn_w_up, l0_ffn_w_down])
    xall = _post_dense_ffn(x_lat, x_ctx, o_lat, o_ctx, mod3, l0_norm_ffn.reshape(1, D_MODEL),
                           l0_w_out.astype(BF16), wg_0, wu_0, wd_0)

    mod3 = _modulation(cc, l1_w_mod, l1_b_mod)
    o_lat, _, (wd_x,) = _mixer(
        xall, xall, LAT_TILES, mod3, tab, l1_norm_attn, l1_w_in, l1_q_lat_norm, l1_w_uq,
        l1_kv_lat_norm, l1_w_ukv, l1_mla_q_gain, l1_mla_k_gain, l1_gqa_q_gain, l1_gqa_k_gain,
        False, [l1_exp_w_down.reshape(N_EXPERTS * EXPERT_FF, D_MODEL)], [])
    w_router = jnp.pad(l1_router, ((0, 0), (0, LANES - N_EXPERTS)))
    w_router_hi = w_router.astype(BF16)
    w_router = jnp.concatenate([w_router_hi, (w_router - w_router_hi.astype(F32)).astype(BF16)],
                               axis=1)
    x1, hrow, route, route_t, counts = _post_attention_router(xall, o_lat, mod3,
                                                     l1_norm_ffn.reshape(1, D_MODEL),
                                                     l1_w_out.astype(BF16), w_router)
    out = _moe_ffn(hrow, x1, route, route_t, counts, mod3,
                   wg_x.reshape(N_EXPERTS, D_MODEL, EXPERT_FF),
                   wu_x.reshape(N_EXPERTS, D_MODEL, EXPERT_FF),
                   wd_x.reshape(N_EXPERTS, EXPERT_FF, D_MODEL))
    return out.reshape(BATCH, SEQ, D_MODEL)
```
